```python
import math
import jax, jax.numpy as jnp
from jax import lax
import numpy as np

D_MODEL = 1024
BATCH = 2
SEQ = 16384
DEPTH = 1
DEC_BATCH = 32
DEC_SEQ = 16
PAST_LEN = 1024

CHUNK = 64
N_META = 16
N_HEADS = 8
HEAD_DIM = 64
D_ATT = N_HEADS * HEAD_DIM
SSM_GROUP = 16
D_SSM = 512
N_SSM_GROUPS = D_SSM // SSM_GROUP
SSM_STATE = 64
D_MIX = D_ATT + D_SSM
Q_BLOCK = 128
EPS = 1e-6
NEG_INF = -1e30
D_IN = 3 * D_ATT + N_HEADS + D_ATT + 2 * D_SSM
SPLITS = (D_ATT, 2 * D_ATT, 3 * D_ATT, 3 * D_ATT + N_HEADS, 4 * D_ATT + N_HEADS,
          4 * D_ATT + N_HEADS + D_SSM)

kernel_name = 'hymba_fox_s5_streaming_step'


def rms_norm(x, g):
    xf = x.astype(jnp.float32)
    y = xf * lax.rsqrt(jnp.mean(xf * xf, axis=-1, keepdims=True) + EPS)
    return (y * g.astype(jnp.float32)).astype(x.dtype)


def mixer_inputs(x, norm_g, w_in, b_f, q_norm_g, k_norm_g):
    bsz, L = x.shape[0], x.shape[1]
    z = rms_norm(x, norm_g) @ w_in
    q, k, v, f_logit, gate_a, u, gate_s = jnp.split(z, SPLITS, axis=-1)
    q = rms_norm(q.reshape(bsz, L, N_HEADS, HEAD_DIM), q_norm_g)
    k = rms_norm(k.reshape(bsz, L, N_HEADS, HEAD_DIM), k_norm_g)
    v = v.reshape(bsz, L, N_HEADS, HEAD_DIM)
    logf = jax.nn.log_sigmoid(f_logit.astype(jnp.float32) + b_f.astype(jnp.float32))
    u = u.reshape(bsz, L, N_SSM_GROUPS, SSM_GROUP)
    return q, k, v, logf, gate_a, u, gate_s


def fox_attend(q, k, v, cum_q, cum_k, q_pos):
    s = jnp.einsum('bthd,bshd->bhts', q, k, preferred_element_type=jnp.float32) * (HEAD_DIM ** -0.5)
    bias = jnp.swapaxes(cum_q, 1, 2)[:, :, :, None] - jnp.swapaxes(cum_k, 1, 2)[:, :, None, :]
    mask = jnp.arange(k.shape[1])[None, :] <= q_pos[:, None]
    p = jax.nn.softmax(jnp.where(mask, s + bias, NEG_INF), axis=-1)
    return jnp.einsum('bhts,bshd->bthd', p.astype(v.dtype), v)


def fox_prompt(q, k, v, logf):
    bsz, L = q.shape[0], q.shape[1]
    n_blk = -(-L // Q_BLOCK)
    pad = n_blk * Q_BLOCK - L
    cum = jnp.cumsum(logf, axis=1)
    q_p = jnp.pad(q, ((0, 0), (0, pad), (0, 0), (0, 0)))
    cum_p = jnp.pad(cum, ((0, 0), (0, pad), (0, 0)))

    def one_block(i):
        start = i * Q_BLOCK
        qb = lax.dynamic_slice_in_dim(q_p, start, Q_BLOCK, axis=1)
        cb = lax.dynamic_slice_in_dim(cum_p, start, Q_BLOCK, axis=1)
        return fox_attend(qb, k, v, cb, cum, start + jnp.arange(Q_BLOCK))

    out = lax.map(one_block, jnp.arange(n_blk))
    out = jnp.moveaxis(out, 0, 1).reshape(bsz, n_blk * Q_BLOCK, N_HEADS, HEAD_DIM)
    return out[:, :L]


def fox_sample(q, k, v, logf, cache_k, cache_v, cache_logf):
    past = cache_k.shape[1]
    T = q.shape[1]
    k_all = jnp.concatenate([cache_k.astype(k.dtype), k], axis=1)
    v_all = jnp.concatenate([cache_v.astype(v.dtype), v], axis=1)
    cum = jnp.cumsum(jnp.concatenate([cache_logf.astype(jnp.float32), logf], axis=1), axis=1)
    return fox_attend(q, k_all, v_all, cum[:, past:], cum, past + jnp.arange(T))


def s5_scan(u, x0, a_re, a_im, log_dt, b_re, b_im, c_re, c_im, d):
    f32 = jnp.float32
    A = lax.complex(a_re.astype(f32), a_im.astype(f32))
    dt = jnp.exp(log_dt.astype(f32))[:, None]
    a_bar = jnp.exp(A * dt)
    b_bar = ((a_bar - 1.0) / A)[..., None] * lax.complex(b_re.astype(f32), b_im.astype(f32))
    c = lax.complex(c_re.astype(f32), c_im.astype(f32))
    uf = u.astype(f32)
    bu = jnp.einsum('gph,blgh->blgp', b_bar, uf)
    a_seq = jnp.broadcast_to(a_bar, bu.shape)

    def combine(left, right):
        a1, b1 = left
        a2, b2 = right
        return a2 * a1, a2 * b1 + b2

    a_cum, xs = lax.associative_scan(combine, (a_seq, bu), axis=1)
    xs = xs + a_cum * x0[:, None]
    y = jnp.einsum('ghp,blgp->blgh', c, xs).real + d.astype(f32) * uf
    return y, xs[:, -1]


def mixer_output(att, gate_a, ys, gate_s, w_glu, b_glu, w_out):
    bsz, L = att.shape[0], att.shape[1]
    att = att.reshape(bsz, L, D_ATT) * jax.nn.silu(gate_a)
    z = jax.nn.gelu(ys.reshape(bsz, L, D_SSM)).astype(gate_s.dtype)
    s5 = z * jax.nn.sigmoid(z @ w_glu + b_glu) * jax.nn.silu(gate_s)
    return jnp.concatenate([att, s5], axis=-1) @ w_out


def setup_inputs(seed: int = 0) -> dict:
    key = jax.random.key(seed)
    ks = jax.random.split(key, 24)
    f32 = jnp.float32

    def nrm(k, shape, s):
        return s * jax.random.normal(k, shape, f32)

    G, P = N_SSM_GROUPS, SSM_STATE
    return {
        'x_prompt': nrm(ks[0], (BATCH, SEQ, D_MODEL), 1.0),
        'x_sample': nrm(ks[1], (DEC_BATCH, DEC_SEQ, D_MODEL), 1.0),
        'cache_k': nrm(ks[2], (DEPTH, DEC_BATCH, PAST_LEN, N_HEADS, HEAD_DIM), 1.0),
        'cache_v': nrm(ks[3], (DEPTH, DEC_BATCH, PAST_LEN, N_HEADS, HEAD_DIM), 1.0),
        'cache_logf': jax.nn.log_sigmoid(2.5 + jax.random.normal(ks[4], (DEPTH, DEC_BATCH, PAST_LEN, N_HEADS), f32)),
        'state_s5_re': nrm(ks[5], (DEPTH, DEC_BATCH, G, P), 0.1),
        'state_s5_im': nrm(ks[6], (DEPTH, DEC_BATCH, G, P), 0.1),
        'meta_tokens': nrm(ks[7], (N_META, D_MODEL), 1.0),
        'norm_g': 1.0 + nrm(ks[8], (DEPTH, D_MODEL), 0.02),
        'w_in': nrm(ks[9], (DEPTH, D_MODEL, D_IN), D_MODEL ** -0.5),
        'b_f': jax.random.uniform(ks[10], (DEPTH, N_HEADS), f32, 1.0, 4.0),
        'q_norm_g': 1.0 + nrm(ks[11], (DEPTH, HEAD_DIM), 0.02),
        'k_norm_g': 1.0 + nrm(ks[12], (DEPTH, HEAD_DIM), 0.02),
        's5_a_re': -0.5 + nrm(ks[13], (DEPTH, G, P), 0.01),
        's5_a_im': math.pi * jnp.arange(P, dtype=f32) + nrm(ks[14], (DEPTH, G, P), 0.01),
        's5_log_dt': jax.random.uniform(ks[15], (DEPTH, G), f32, math.log(1e-3), math.log(1e-1)),
        's5_b_re': nrm(ks[16], (DEPTH, G, P, SSM_GROUP), (2 * SSM_GROUP) ** -0.5),
        's5_b_im': nrm(ks[17], (DEPTH, G, P, SSM_GROUP), (2 * SSM_GROUP) ** -0.5),
        's5_c_re': nrm(ks[18], (DEPTH, G, SSM_GROUP, P), (2 * P) ** -0.5),
        's5_c_im': nrm(ks[19], (DEPTH, G, SSM_GROUP, P), (2 * P) ** -0.5),
        's5_d': nrm(ks[20], (DEPTH, G, SSM_GROUP), 1.0),
        'w_glu': nrm(ks[21], (DEPTH, D_SSM, D_SSM), D_SSM ** -0.5),
        'b_glu': nrm(ks[22], (DEPTH, D_SSM), 0.01),
        'w_out': nrm(ks[23], (DEPTH, D_MIX, D_MODEL), D_MIX ** -0.5),
    }


def reference(x_prompt, x_sample, cache_k, cache_v, cache_logf, state_s5_re, state_s5_im,
              meta_tokens, norm_g, w_in, b_f, q_norm_g, k_norm_g, s5_a_re, s5_a_im, s5_log_dt,
              s5_b_re, s5_b_im, s5_c_re, s5_c_im, s5_d, w_glu, b_glu, w_out):
    bp = x_prompt.shape[0]
    meta = jnp.broadcast_to(meta_tokens.astype(x_prompt.dtype)[None], (bp, N_META, D_MODEL))
    hp = jnp.concatenate([meta, x_prompt], axis=1)
    hs = x_sample
    kp, vp, fp, srp, sip = [], [], [], [], []
    kss, vss, fss, srs, sis = [], [], [], [], []
    for l in range(DEPTH):
        proj = (norm_g[l], w_in[l], b_f[l], q_norm_g[l], k_norm_g[l])
        ssm = (s5_a_re[l], s5_a_im[l], s5_log_dt[l], s5_b_re[l], s5_b_im[l],
               s5_c_re[l], s5_c_im[l], s5_d[l])
        outp = (w_glu[l], b_glu[l], w_out[l])

        q, k, v, logf, ga, u, gs = mixer_inputs(hp, *proj)
        att = fox_prompt(q, k, v, logf)
        x0 = jnp.zeros((bp, N_SSM_GROUPS, SSM_STATE), jnp.complex64)
        ys, xl = s5_scan(u, x0, *ssm)
        hp = hp + mixer_output(att, ga, ys, gs, *outp)
        kp.append(k); vp.append(v); fp.append(logf)
        srp.append(xl.real); sip.append(xl.imag)

        q, k, v, logf, ga, u, gs = mixer_inputs(hs, *proj)
        att = fox_sample(q, k, v, logf, cache_k[l], cache_v[l], cache_logf[l])
        x0 = lax.complex(state_s5_re[l].astype(jnp.float32), state_s5_im[l].astype(jnp.float32))
        ys, xl = s5_scan(u, x0, *ssm)
        hs = hs + mixer_output(att, ga, ys, gs, *outp)
        kss.append(k); vss.append(v); fss.append(logf)
        srs.append(xl.real); sis.append(xl.imag)

    y_prompt = hp[:, N_META:]
    y_sample = hs
    new_k_prompt = jnp.stack(kp)
    new_v_prompt = jnp.stack(vp)
    new_logf_prompt = jnp.stack(fp)
    new_s5_re_prompt = jnp.stack(srp)
    new_s5_im_prompt = jnp.stack(sip)
    new_k_sample = jnp.stack(kss)
    new_v_sample = jnp.stack(vss)
    new_logf_sample = jnp.stack(fss)
    new_s5_re_sample = jnp.stack(srs)
    new_s5_im_sample = jnp.stack(sis)
    return (y_prompt, y_sample, new_k_prompt, new_v_prompt, new_logf_prompt,
            new_s5_re_prompt, new_s5_im_prompt, new_k_sample, new_v_sample,
            new_logf_sample, new_s5_re_sample, new_s5_im_sample)
```

```python
import functools

import jax
import jax.numpy as jnp
from jax import lax
from jax.experimental import pallas as pl
from jax.experimental.pallas import tpu as pltpu

N_HEADS = 8
HEAD_DIM = 64
D_ATT = N_HEADS * HEAD_DIM
SSM_GROUP = 16
SSM_STATE = 64
D_SSM = 512
N_GROUPS = D_SSM // SSM_GROUP
EPS = 1e-6
NEG_INF = -1e30

LANES = 128
AUG = LANES // N_HEADS
GROUPS_PER_BLOCK = LANES // SSM_GROUP
STATE_BLOCK = GROUPS_PER_BLOCK * SSM_STATE
N_SSM_BLOCKS = D_SSM // LANES
VMEM_LIMIT = 56 * 1024 * 1024
EXP_ZERO = 110.0

ROW_TILE = 512
ATT_TILE = 512
PROMPT_SEGMENTS = 256

bf16 = jnp.bfloat16
f32 = jnp.float32


def _cparams(*sem):
    return pltpu.CompilerParams(dimension_semantics=sem, vmem_limit_bytes=VMEM_LIMIT)


def _split3(x):
    hi = x.astype(bf16).astype(f32)
    r1 = x - hi
    mid = r1.astype(bf16).astype(f32)
    lo = (r1 - mid).astype(bf16).astype(f32)
    return hi, mid, lo


def _dot(a, b):
    return jnp.dot(a, b, preferred_element_type=f32)


def _dot_nt(a, b):
    return lax.dot_general(a, b, (((1,), (1,)), ((), ())), preferred_element_type=f32)


def _inproj_kernel(chain, n_blk, x_ref, c0_ref, g_ref, wm_ref, wf_ref, bf_ref, qg_ref, kg_ref, eh_ref,
                   tri_ref, negm_ref, q_ref, kb_ref, vb_ref, kf_ref, vf_ref, lf_ref, cum_ref, qcat_ref,
                   kcat_ref, ga_ref, u_ref, gs_ref, carry_ref):
    x = x_ref[0]
    ms = jnp.mean(x * x, axis=-1, keepdims=True)
    xn = (x * lax.rsqrt(ms + EPS) * g_ref[...]).astype(bf16)
    z = _dot(xn, wm_ref[...])
    q = z[:, 0 * D_ATT:1 * D_ATT]
    k = z[:, 1 * D_ATT:2 * D_ATT]
    v = z[:, 2 * D_ATT:3 * D_ATT]
    eh = eh_ref[...]
    qn = q * lax.rsqrt(_dot((q * q).astype(bf16), eh) + EPS) * qg_ref[...]
    kn = k * lax.rsqrt(_dot((k * k).astype(bf16), eh) + EPS) * kg_ref[...]
    q_ref[0] = qn.astype(bf16)
    kf_ref[0] = kn
    kb_ref[0] = kn.astype(bf16)
    vf_ref[0] = v
    vb_ref[0] = v.astype(bf16)
    ga_ref[0] = z[:, 3 * D_ATT:4 * D_ATT].astype(bf16)
    u_ref[0] = z[:, 4 * D_ATT:4 * D_ATT + D_SSM]
    gs_ref[0] = z[:, 4 * D_ATT + D_SSM:].astype(bf16)

    zf = _dot(xn, wf_ref[...]) + bf_ref[...]
    lf = jnp.minimum(zf, 0.0) - jnp.log1p(jnp.exp(-jnp.abs(zf)))
    lf_ref[0] = lf

    if chain:
        @pl.when(pl.program_id(1) == 0)
        def _():
            carry_ref[...] = c0_ref[0]
        carry = carry_ref[...]
    else:
        carry = jnp.zeros((1, LANES), f32)
    tri = tri_ref[...]
    tb = x.shape[0] // n_blk
    cums = []
    for r in range(n_blk):
        hi, mid, lo = _split3(lf[r * tb:(r + 1) * tb])
        c = _dot(tri, hi.astype(bf16)) + _dot(tri, mid.astype(bf16)) + _dot(tri, lo.astype(bf16)) + carry
        cums.append(c)
        carry = c[tb - 1:tb, :]
    cum = cums[0] if n_blk == 1 else jnp.concatenate(cums, axis=0)
    if chain:
        carry_ref[...] = carry
    cum_ref[0] = cum

    slot = lax.broadcasted_iota(jnp.int32, (1, LANES), 1) % AUG
    hi, mid, lo = _split3(cum)
    qc = jnp.where(slot == 0, hi, jnp.where(slot == 1, mid, jnp.where(slot == 2, lo,
                   jnp.where(slot <= 6, 1.0, 0.0))))
    kc = jnp.where(slot <= 2, 1.0, jnp.where(slot == 3, -hi, jnp.where(slot == 4, -mid,
                   jnp.where(slot == 5, -lo, jnp.where(slot == 6, negm_ref[...], 0.0)))))
    qcat_ref[0] = qc.astype(bf16)
    kcat_ref[0] = kc.astype(bf16)


def _inproj(x, c0, params, tile, tri, chain, name):
    bx, lx, d = x.shape
    n_tiles = lx // tile
    n_blk = tile // tri.shape[0]
    full = lambda a: pl.BlockSpec(a.shape, lambda b, j: (0,) * a.ndim)
    row = lambda w: pl.BlockSpec((1, tile, w), lambda b, j: (b, j, 0))
    widths = [D_ATT, D_ATT, D_ATT, D_ATT, D_ATT, LANES, LANES, LANES, LANES, D_ATT, D_SSM, D_SSM]
    dtypes = [bf16, bf16, bf16, f32, f32, f32, f32, bf16, bf16, bf16, f32, bf16]
    return pl.pallas_call(
        functools.partial(_inproj_kernel, chain, n_blk),
        grid=(bx, n_tiles),
        in_specs=([row(d), pl.BlockSpec((1, 1, LANES), lambda b, j: (b, 0, 0))] + [full(a) for a in params[:7]]
                  + [full(tri), full(params[7])]),
        out_specs=[row(w) for w in widths],
        out_shape=[jax.ShapeDtypeStruct((bx, lx, w), t) for w, t in zip(widths, dtypes)],
        scratch_shapes=[pltpu.VMEM((1, LANES), f32)],
        compiler_params=_cparams("parallel", "arbitrary"),
        name=name,
    )(x, c0, *params[:7], tri, params[7])


def _attn_kernel(tile, nq, n_meta, jmin_ref, mflag_ref, q_ref, qc_ref, kb_ref, vb_ref, kc_ref,
                 kbm_ref, vbm_ref, kcm_ref, o_ref, acc_ref):
    b = pl.program_id(0)
    p = pl.program_id(1)
    i = pl.program_id(2)
    lane = lax.broadcasted_iota(jnp.int32, (1, LANES), 1)
    qt = q_ref[0].astype(f32)
    qct = qc_ref[0].astype(f32)
    row = lax.broadcasted_iota(jnp.int32, (tile, tile), 0)
    col = lax.broadcasted_iota(jnp.int32, (tile, tile), 1)
    mcol = lax.broadcasted_iota(jnp.int32, (tile, LANES), 1)
    outs = []
    for hh in range(2):
        h = 2 * p + hh
        qm = jnp.where(lane // HEAD_DIM == hh, qt, 0.0).astype(bf16)
        cm = jnp.where(lane // AUG == h, qct, 0.0).astype(bf16)
        q_op = jnp.concatenate([qm, cm], axis=1)

        def scores_pv(ks, kc, vs, mask):
            s = _dot_nt(q_op, jnp.concatenate([ks, kc], axis=1))
            if mask is not None:
                s = jnp.where(mask, s, NEG_INF)
            pe = jnp.exp(s).astype(bf16)
            v_op = jnp.concatenate([vs, jnp.ones(vs.shape, bf16)], axis=1)
            return _dot(pe, v_op)

        def tile_at(j):
            off = pl.multiple_of(j * tile, tile)
            sl = pl.ds(off, tile)
            return kb_ref[0, sl, :], kc_ref[0, sl, :], vb_ref[0, sl, :]

        acc_ref[...] = scores_pv(*tile_at(i), col <= row)
        idx = (b * N_HEADS + h) * nq + i

        def body(j, carry):
            acc_ref[...] += scores_pv(*tile_at(j), None)
            return carry

        lax.fori_loop(jmin_ref[idx], i, body, 0)

        @pl.when(mflag_ref[idx] > 0)
        def _():
            acc_ref[...] += scores_pv(kbm_ref[...], kcm_ref[...], vbm_ref[...], mcol < n_meta)

        a = acc_ref[...]
        outs.append(a[:, :LANES] / a[:, LANES:])
    o_ref[0] = jnp.where(lane < HEAD_DIM, outs[0], outs[1]).astype(bf16)


def _prompt_attention(q, qcat, kb, vb, kcat, kbm, vbm, kcm, jmin, mflag, n_meta):
    b, l, _ = q.shape
    tile = min(ATT_TILE, l)
    nq = l // tile
    grid_spec = pltpu.PrefetchScalarGridSpec(
        num_scalar_prefetch=2,
        grid=(b, N_HEADS // 2, nq),
        in_specs=[
            pl.BlockSpec((1, tile, LANES), lambda b, p, i, *_: (b, i, p)),
            pl.BlockSpec((1, tile, LANES), lambda b, p, i, *_: (b, i, 0)),
            pl.BlockSpec((1, l, LANES), lambda b, p, i, *_: (b, 0, p)),
            pl.BlockSpec((1, l, LANES), lambda b, p, i, *_: (b, 0, p)),
            pl.BlockSpec((1, l, LANES), lambda b, p, i, *_: (b, 0, 0)),
            pl.BlockSpec((LANES, LANES), lambda b, p, i, *_: (0, p)),
            pl.BlockSpec((LANES, LANES), lambda b, p, i, *_: (0, p)),
            pl.BlockSpec((LANES, LANES), lambda b, p, i, *_: (0, 0)),
        ],
        out_specs=pl.BlockSpec((1, tile, LANES), lambda b, p, i, *_: (b, i, p)),
        scratch_shapes=[pltpu.VMEM((tile, 2 * LANES), f32)],
    )
    return pl.pallas_call(
        functools.partial(_attn_kernel, tile, nq, n_meta),
        grid_spec=grid_spec,
        out_shape=jax.ShapeDtypeStruct((b, l, D_ATT), bf16),
        compiler_params=_cparams("parallel", "parallel", "arbitrary"),
        name="prompt_attention",
    )(jmin, mflag, q, qcat, kb, vb, kcat, kbm, vbm, kcm)


def _sattn_kernel(past, tn, q_ref, kn_ref, vn_ref, ck_ref, cv_ref, lft_ref, o_ref):
    s_len = lft_ref.shape[-1]
    rows = N_HEADS * tn
    c = lft_ref[0]
    lane_h = lax.broadcasted_iota(jnp.int32, c.shape, 1)
    sh = 1
    while sh < s_len:
        c = c + jnp.where(lane_h >= sh, pltpu.roll(c, sh, axis=1), 0.0)
        sh *= 2
    cexp = jnp.concatenate([jnp.broadcast_to(c[h:h + 1, :], (tn, s_len)) for h in range(N_HEADS)], axis=0)
    rix = lax.broadcasted_iota(jnp.int32, (rows, s_len), 0)
    lix = lax.broadcasted_iota(jnp.int32, (rows, s_len), 1)
    qpos = past + rix % tn
    cq = jnp.sum(jnp.where(lix == qpos, cexp, 0.0), axis=1, keepdims=True)
    pad = jnp.zeros((s_len - past - tn, D_ATT), bf16)
    k_all = jnp.concatenate([ck_ref[0].astype(bf16), kn_ref[0], pad], axis=0)
    v_all = jnp.concatenate([cv_ref[0].astype(bf16), vn_ref[0], pad], axis=0)
    qrep = jnp.concatenate([q_ref[0].astype(f32)] * N_HEADS, axis=0)
    r5 = lax.broadcasted_iota(jnp.int32, (rows, D_ATT), 0)
    l5 = lax.broadcasted_iota(jnp.int32, (rows, D_ATT), 1)
    qrows = jnp.where(l5 // HEAD_DIM == r5 // tn, qrep, 0.0).astype(bf16)
    s = _dot_nt(qrows, k_all) + (cq - cexp)
    s = jnp.where(lix <= qpos, s, NEG_INF)
    m = jnp.max(s, axis=1, keepdims=True)
    pe = jnp.exp(s - m)
    den = jnp.sum(pe, axis=1, keepdims=True)
    o = _dot(pe.astype(bf16), v_all) / den
    lo = lax.broadcasted_iota(jnp.int32, (tn, D_ATT), 1)
    out = jnp.zeros((tn, D_ATT), f32)
    for h in range(N_HEADS):
        out = out + jnp.where(lo // HEAD_DIM == h, o[h * tn:(h + 1) * tn, :], 0.0)
    o_ref[0] = out.astype(bf16)


def _sample_attention(q, kn, vn, ck, cv, lft):
    bs, tn, _ = q.shape
    past = ck.shape[1]
    s_len = lft.shape[-1]
    blk = lambda n, w: pl.BlockSpec((1, n, w), lambda b: (b, 0, 0))
    return pl.pallas_call(
        functools.partial(_sattn_kernel, past, tn),
        grid=(bs,),
        in_specs=[blk(tn, D_ATT), blk(tn, D_ATT), blk(tn, D_ATT), blk(past, D_ATT), blk(past, D_ATT),
                  blk(N_HEADS, s_len)],
        out_specs=blk(tn, D_ATT),
        out_shape=jax.ShapeDtypeStruct((bs, tn, D_ATT), bf16),
        compiler_params=_cparams("parallel"),
        name="sample_attention",
    )(q, kn, vn, ck, cv, lft)


def _s5prep_kernel(n_sq, are_ref, aim_ref, ldt_ref, btr_ref, bti_ref,
                   abr_ref, abi_ref, apr_ref, api_ref, bbr_ref, bbi_ref):
    are = are_ref[...]
    aim = aim_ref[...]
    dt = jnp.exp(ldt_ref[...])
    mag = jnp.exp(are * dt)
    ph = aim * dt
    abr = mag * jnp.cos(ph)
    abi = mag * jnp.sin(ph)
    nr = abr - 1.0
    den = are * are + aim * aim
    cr = (nr * are + abi * aim) / den
    ci = (abi * are - nr * aim) / den
    abr_ref[...] = abr
    abi_ref[...] = abi
    for g in range(are.shape[0]):
        btr = btr_ref[g]
        bti = bti_ref[g]
        bbr_ref[g] = cr[g:g + 1, :] * btr - ci[g:g + 1, :] * bti
        bbi_ref[g] = cr[g:g + 1, :] * bti + ci[g:g + 1, :] * btr
    pr, pi = abr, abi
    for _ in range(n_sq):
        pr, pi = pr * pr - pi * pi, 2.0 * pr * pi
    apr_ref[...] = pr
    api_ref[...] = pi


def _s5_prep(a_re, a_im, log_dt, bt_re, bt_im, n_sq):
    g, p = a_re.shape
    sd = jax.ShapeDtypeStruct
    return pl.pallas_call(
        functools.partial(_s5prep_kernel, n_sq),
        out_shape=[sd((g, p), f32)] * 4 + [sd(bt_re.shape, f32)] * 2,
        name="s5_prep",
    )(a_re, a_im, log_dt, bt_re, bt_im)


def _s5_kernel(chain, n_seg, seg_len, u_ref, bblk_ref, cblk_ref, ar_ref, ai_ref, apr_ref, api_ref, d_ref,
               x0r_ref, x0i_ref, y_ref, sr_ref, si_ref, xr_ref, xi_ref, cr_ref, ci_ref):
    ar = ar_ref[0]
    ai = ai_ref[0]
    bblk = bblk_ref[0]
    cblk = cblk_ref[0]
    dd = d_ref[0]

    def step(with_y, s, carry):
        rows = pl.ds(s, n_seg, stride=seg_len)
        us = u_ref[0, rows, :]
        bu = _dot(us.astype(bf16), bblk)
        xr = xr_ref[...]
        xi = xi_ref[...]
        nxr = ar * xr - ai * xi + bu[:, :STATE_BLOCK]
        nxi = ar * xi + ai * xr + bu[:, STATE_BLOCK:]
        xr_ref[...] = nxr
        xi_ref[...] = nxi
        if with_y:
            xc = jnp.concatenate([nxr, nxi], axis=1).astype(bf16)
            y_ref[0, rows, :] = _dot(xc, cblk) + dd * us
        return carry

    if chain:
        xr_ref[...] = jnp.zeros(xr_ref.shape, f32)
        xi_ref[...] = jnp.zeros(xi_ref.shape, f32)
        lax.fori_loop(0, seg_len, functools.partial(step, False), 0)
        apr = apr_ref[0]
        api = api_ref[0]

        def scan(c, carry):
            kr, ki = carry
            cr_ref[pl.ds(c, 1), :] = kr
            ci_ref[pl.ds(c, 1), :] = ki
            er = xr_ref[pl.ds(c, 1), :]
            ei = xi_ref[pl.ds(c, 1), :]
            return apr * kr - api * ki + er, apr * ki + api * kr + ei

        kr, ki = lax.fori_loop(0, n_seg, scan, (x0r_ref[0], x0i_ref[0]))
        sr_ref[0] = kr
        si_ref[0] = ki
        xr_ref[...] = cr_ref[...]
        xi_ref[...] = ci_ref[...]
        lax.fori_loop(0, seg_len, functools.partial(step, True), 0)
    else:
        xr_ref[...] = x0r_ref[0]
        xi_ref[...] = x0i_ref[0]
        lax.fori_loop(0, seg_len, functools.partial(step, True), 0)
        sr_ref[0] = xr_ref[...]
        si_ref[0] = xi_ref[...]


def _s5(u, x0r, x0i, bblk, cblk, ar, ai, apr, api, dblk, n_seg, chain, name):
    bq, l, _ = u.shape
    seg_len = l // n_seg
    r0 = x0r.shape[1]
    per_k = lambda a: pl.BlockSpec((1,) + a.shape[1:], lambda b, k: (k, 0, 0))
    st = pl.BlockSpec((1, r0, STATE_BLOCK), lambda b, k: (b, 0, k))
    tok = pl.BlockSpec((1, l, LANES), lambda b, k: (b, 0, k))
    sd = jax.ShapeDtypeStruct
    return pl.pallas_call(
        functools.partial(_s5_kernel, chain, n_seg, seg_len),
        grid=(bq, N_SSM_BLOCKS),
        in_specs=[tok, per_k(bblk), per_k(cblk), per_k(ar), per_k(ai), per_k(apr), per_k(api), per_k(dblk), st, st],
        out_specs=[tok, st, st],
        out_shape=[sd((bq, l, D_SSM), f32), sd(x0r.shape, f32), sd(x0i.shape, f32)],
        scratch_shapes=[pltpu.VMEM((n_seg, STATE_BLOCK), f32)] * 4,
        compiler_params=_cparams("parallel", "parallel"),
        name=name,
    )(u, bblk, cblk, ar, ai, apr, api, dblk, x0r, x0i)


def _out_kernel(x_ref, att_ref, ga_ref, ys_ref, gs_ref, wg_ref, bg_ref, wo_ref, y_ref):
    ga = ga_ref[0].astype(f32)
    a = att_ref[0].astype(f32) * (ga * jax.nn.sigmoid(ga))
    ys = ys_ref[0]
    z = 0.5 * ys * (1.0 + jnp.tanh(0.7978845608028654 * (ys + 0.044715 * (ys * ys * ys))))
    gate = jax.nn.sigmoid(_dot(z.astype(bf16), wg_ref[...]) + bg_ref[...])
    gs = gs_ref[0].astype(f32)
    s5 = z * gate * (gs * jax.nn.sigmoid(gs))
    cat = jnp.concatenate([a, s5], axis=1).astype(bf16)
    y_ref[0] = x_ref[0] + _dot(cat, wo_ref[...])


def _outproj(x, att, ga, ys, gs, wg, bg, wo, tile, name):
    bx, lx, d = x.shape
    full = lambda a: pl.BlockSpec(a.shape, lambda b, j: (0,) * a.ndim)
    row = lambda w: pl.BlockSpec((1, tile, w), lambda b, j: (b, j, 0))
    return pl.pallas_call(
        _out_kernel,
        grid=(bx, lx // tile),
        in_specs=[row(d), row(D_ATT), row(D_ATT), row(D_SSM), row(D_SSM), full(wg), full(bg), full(wo)],
        out_specs=row(d),
        out_shape=jax.ShapeDtypeStruct(x.shape, f32),
        compiler_params=_cparams("parallel", "parallel"),
        name=name,
    )(x, att, ga, ys, gs, wg, bg, wo)


def _lower_tri(n):
    r = jnp.arange(n)
    return (r[None, :] <= r[:, None]).astype(bf16)


def _block_diag(blocks):
    k, g8, r, c = blocks.shape
    eye = jnp.eye(g8, dtype=blocks.dtype)
    return jnp.einsum("kgrc,gh->kgrhc", blocks, eye).reshape(k, g8 * r, g8 * c)


def kernel(x_prompt, x_sample, cache_k, cache_v, cache_logf, state_s5_re, state_s5_im, meta_tokens, norm_g,
           w_in, b_f, q_norm_g, k_norm_g, s5_a_re, s5_a_im, s5_log_dt, s5_b_re, s5_b_im, s5_c_re, s5_c_im,
           s5_d, w_glu, b_glu, w_out):
    assert norm_g.shape[0] == 1, "single-layer step"
    bp, lp, d = x_prompt.shape
    bs, ts, _ = x_sample.shape
    n_meta = meta_tokens.shape[0]
    past = cache_k.shape[2]
    G, P = N_GROUPS, SSM_STATE

    w = w_in[0]
    o1, o2, o3 = D_ATT, 2 * D_ATT, 3 * D_ATT
    o4 = o3 + N_HEADS
    o5 = o4 + D_ATT
    o6 = o5 + D_SSM
    wm = jnp.concatenate([w[:, :o3], w[:, o4:]], axis=1).astype(bf16)
    wf = jnp.repeat(w[:, o3:o4], AUG, axis=1).astype(bf16)
    bfr = jnp.repeat(b_f[0], AUG)[None, :].astype(f32)
    g_in = norm_g[0][None, :].astype(f32)
    qg = (jnp.tile(q_norm_g[0], N_HEADS) * (HEAD_DIM ** -0.5))[None, :].astype(f32)
    kg = jnp.tile(k_norm_g[0], N_HEADS)[None, :].astype(f32)
    hid = jnp.arange(D_ATT) // HEAD_DIM
    eh = ((hid[:, None] == hid[None, :]).astype(f32) / HEAD_DIM).astype(bf16)
    m_bound = (8.0 * jnp.max(jnp.abs(q_norm_g[0])) * jnp.max(jnp.abs(k_norm_g[0])) * 1.02).astype(bf16).astype(f32)
    negm = jnp.full((1, LANES), -1.0, f32) * m_bound
    params = (g_in, wm, wf, bfr, qg, kg, eh, negm)

    tm = n_meta
    meta = _inproj(meta_tokens[None].astype(f32), jnp.zeros((1, 1, LANES), f32), params, tm, _lower_tri(tm), True,
                   "inproj_meta")
    (_, kb_m, vb_m, kf_m, vf_m, lf_m, cum_m, _, kcat_m, _, u_m, _) = meta
    c0 = jnp.broadcast_to(cum_m[:, tm - 1:tm, :], (bp, 1, LANES))
    tile_p = min(ROW_TILE, lp)
    tri_p = _lower_tri(min(LANES, tile_p))
    (q_p, kb_p, vb_p, kf_p, vf_p, lf_p, cum_p, qcat_p, kcat_p, ga_p, u_p, gs_p) = _inproj(
        x_prompt, c0, params, tile_p, tri_p, True, "inproj_prompt")
    ls = bs * ts
    tile_s = min(ROW_TILE, ls)
    (q_s, kb_s, vb_s, kf_s, vf_s, lf_s, _, _, _, ga_s, u_s, gs_s) = _inproj(
        x_sample.reshape(1, ls, d), jnp.zeros((1, 1, LANES), f32), params, tile_s,
        _lower_tri(min(LANES, tile_s)), False, "inproj_sample")

    tile_a = min(ATT_TILE, lp)
    nq = lp // tile_a
    thr = EXP_ZERO + 2.0 * m_bound
    cum_h = cum_p[:, :, ::AUG]
    q_start = cum_h[:, 0::tile_a]
    k_end = cum_h[:, tile_a - 1::tile_a]
    dmax = q_start[:, :, None, :] - k_end[:, None, :, :]
    jmin = jnp.sum((dmax < -thr).astype(jnp.int32), axis=2)
    jmin = jnp.minimum(jmin, jnp.arange(nq, dtype=jnp.int32)[None, :, None])
    jmin = jnp.transpose(jmin, (0, 2, 1)).reshape(-1)
    cum_meta_end = cum_m[0, tm - 1, ::AUG]
    mflag = (q_start - cum_meta_end[None, None, :] >= -thr).astype(jnp.int32)
    mflag = jnp.transpose(mflag, (0, 2, 1)).reshape(-1)
    padm = lambda a: jnp.pad(a[0], ((0, LANES - tm), (0, 0)))
    att_p = _prompt_attention(q_p, qcat_p, kb_p, vb_p, kcat_p, padm(kb_m), padm(vb_m), padm(kcat_m),
                              jmin, mflag, n_meta)

    lf_new = lf_s[0, :, ::AUG].reshape(bs, ts, N_HEADS)
    s_len = -(-(past + ts) // LANES) * LANES
    lft = jnp.concatenate([cache_logf[0].astype(f32), lf_new], axis=1)
    lft = jnp.pad(jnp.transpose(lft, (0, 2, 1)), ((0, 0), (0, 0), (0, s_len - past - ts)))
    att_s = _sample_attention(q_s.reshape(bs, ts, D_ATT), kb_s.reshape(bs, ts, D_ATT), vb_s.reshape(bs, ts, D_ATT),
                              cache_k[0].reshape(bs, past, D_ATT), cache_v[0].reshape(bs, past, D_ATT), lft)

    n_seg_p = min(PROMPT_SEGMENTS, lp)
    seg_len_p = lp // n_seg_p
    n_sq = seg_len_p.bit_length() - 1
    assert seg_len_p == 1 << n_sq
    bt_re = jnp.transpose(s5_b_re[0], (0, 2, 1)).astype(f32)
    bt_im = jnp.transpose(s5_b_im[0], (0, 2, 1)).astype(f32)
    abr, abi, apr, api, bbr, bbi = _s5_prep(s5_a_re[0].astype(f32), s5_a_im[0].astype(f32),
                                            s5_log_dt[0][:, None].astype(f32), bt_re, bt_im, n_sq)
    K8 = (N_SSM_BLOCKS, GROUPS_PER_BLOCK)
    bblk = jnp.concatenate([_block_diag(bbr.reshape(K8 + (SSM_GROUP, P))),
                            _block_diag(bbi.reshape(K8 + (SSM_GROUP, P)))], axis=2).astype(bf16)
    ct_re = jnp.transpose(s5_c_re[0], (0, 2, 1)).astype(f32)
    ct_im = jnp.transpose(s5_c_im[0], (0, 2, 1)).astype(f32)
    cblk = jnp.concatenate([_block_diag(ct_re.reshape(K8 + (P, SSM_GROUP))),
                            _block_diag(-ct_im.reshape(K8 + (P, SSM_GROUP)))], axis=1).astype(bf16)
    lane_k = lambda a: a.reshape(N_SSM_BLOCKS, 1, -1)
    ar, ai, aprk, apik = lane_k(abr), lane_k(abi), lane_k(apr), lane_k(api)
    dblk = lane_k(s5_d[0].astype(f32))

    n_small = -(-(bs + 1) // 8) * 8
    u_small = jnp.concatenate([u_s, jnp.broadcast_to(u_m, (1, tm, D_SSM)),
                               jnp.zeros((1, (n_small - bs) * ts - tm, D_SSM), f32)], axis=1)
    assert tm == ts
    pad_state = lambda a: jnp.pad(a[0].reshape(bs, G * P).astype(f32), ((0, n_small - bs), (0, 0)))[None]
    ys_small, sr_small, si_small = _s5(u_small, pad_state(state_s5_re), pad_state(state_s5_im), bblk, cblk,
                                       ar, ai, ar, ai, dblk, n_small, False, "s5_short")
    x0r = jnp.broadcast_to(sr_small[:, bs:bs + 1, :], (bp, 1, G * P))
    x0i = jnp.broadcast_to(si_small[:, bs:bs + 1, :], (bp, 1, G * P))
    ys_p, sr_p, si_p = _s5(u_p, x0r, x0i, bblk, cblk, ar, ai, aprk, apik, dblk, n_seg_p, True, "s5_prompt")

    wg = w_glu[0].astype(bf16)
    bg = b_glu[0][None, :].astype(f32)
    wo = w_out[0].astype(bf16)
    y_prompt = _outproj(x_prompt, att_p, ga_p, ys_p, gs_p, wg, bg, wo, tile_p, "outproj_prompt")
    y_sample = _outproj(x_sample.reshape(1, ls, d), att_s.reshape(1, ls, D_ATT), ga_s, ys_small[:, :ls], gs_s,
                        wg, bg, wo, tile_s, "outproj_sample").reshape(bs, ts, d)

    heads = lambda a, b_, n: a.reshape(1, b_, n, N_HEADS, HEAD_DIM)
    with_meta = lambda m, x: jnp.concatenate([jnp.broadcast_to(m, (bp,) + m.shape[1:]), x], axis=1)
    new_k_prompt = heads(with_meta(kf_m, kf_p), bp, lp + n_meta)
    new_v_prompt = heads(with_meta(vf_m, vf_p), bp, lp + n_meta)
    new_logf_prompt = with_meta(lf_m[:, :, ::AUG], lf_p[:, :, ::AUG])[None]
    new_s5_re_prompt = sr_p.reshape(1, bp, G, P)
    new_s5_im_prompt = si_p.reshape(1, bp, G, P)
    new_k_sample = heads(kf_s, bs, ts)
    new_v_sample = heads(vf_s, bs, ts)
    new_logf_sample = lf_new[None]
    new_s5_re_sample = sr_small[:, :bs].reshape(1, bs, G, P)
    new_s5_im_sample = si_small[:, :bs].reshape(1, bs, G, P)
    return (y_prompt, y_sample, new_k_prompt, new_v_prompt, new_logf_prompt, new_s5_re_prompt, new_s5_im_prompt,
            new_k_sample, new_v_sample, new_logf_sample, new_s5_re_sample, new_s5_im_sample)
```

```python
import functools

import jax
import jax.numpy as jnp
from jax import lax
from jax.experimental import pallas as pl
from jax.experimental.pallas import tpu as pltpu

N_HEADS = 8
HEAD_DIM = 64
D_ATT = N_HEADS * HEAD_DIM
SSM_GROUP = 16
SSM_STATE = 64
D_SSM = 512
N_GROUPS = D_SSM // SSM_GROUP
EPS = 1e-6
NEG_INF = -1e30

LANES = 128
SUBLANES = 8
AUG = LANES // N_HEADS
GROUPS_PER_BLOCK = LANES // SSM_GROUP
STATE_BLOCK = GROUPS_PER_BLOCK * SSM_STATE
N_SSM_BLOCKS = D_SSM // LANES
VMEM_LIMIT = 56 * 1024 * 1024
EXP_ZERO = 105.0

ROW_TILE = 512
ATT_TILE = 512
PROMPT_SEGMENTS = 256
S5_UNROLL = 8

bf16 = jnp.bfloat16
f32 = jnp.float32


def _cparams(*sem):
    return pltpu.CompilerParams(dimension_semantics=sem, vmem_limit_bytes=VMEM_LIMIT)


def _split3(x):
    hi = x.astype(bf16).astype(f32)
    r1 = x - hi
    mid = r1.astype(bf16).astype(f32)
    lo = (r1 - mid).astype(bf16).astype(f32)
    return hi, mid, lo


def _dot(a, b):
    return jnp.dot(a, b, preferred_element_type=f32)


def _dot_nt(a, b):
    return lax.dot_general(a, b, (((1,), (1,)), ((), ())), preferred_element_type=f32)


def _inproj_kernel(chain, n_blk, x_ref, c0_ref, g_ref, wm_ref, wf_ref, bf_ref, qg_ref, kg_ref, eh_ref,
                   tri_ref, negm_ref, sel_ref, q_ref, kb_ref, vb_ref, kf_ref, vf_ref, lft_ref, edge_ref,
                   qcat_ref, kcat_ref, ga_ref, u_ref, gs_ref, carry_ref):
    x = x_ref[0]
    ms = jnp.mean(x * x, axis=-1, keepdims=True)
    xn = (x * lax.rsqrt(ms + EPS) * g_ref[...]).astype(bf16)
    z = _dot(xn, wm_ref[...])
    q = z[:, 0 * D_ATT:1 * D_ATT]
    k = z[:, 1 * D_ATT:2 * D_ATT]
    v = z[:, 2 * D_ATT:3 * D_ATT]
    eh = eh_ref[...]
    qn = q * lax.rsqrt(_dot((q * q).astype(bf16), eh) + EPS) * qg_ref[...]
    kn = k * lax.rsqrt(_dot((k * k).astype(bf16), eh) + EPS) * kg_ref[...]
    q_ref[0] = qn.astype(bf16)
    kf_ref[0] = kn
    kb_ref[0] = kn.astype(bf16)
    vf_ref[0] = v
    vb_ref[0] = v.astype(bf16)
    ga_ref[0] = z[:, 3 * D_ATT:4 * D_ATT].astype(bf16)
    u_ref[0] = z[:, 4 * D_ATT:4 * D_ATT + D_SSM]
    gs_ref[0] = z[:, 4 * D_ATT + D_SSM:].astype(bf16)

    zf = _dot(xn, wf_ref[...]) + bf_ref[...]
    lf = jnp.minimum(zf, 0.0) - jnp.log1p(jnp.exp(-jnp.abs(zf)))
    sel = sel_ref[...]
    lft_ref[0] = sum(_dot_nt(sel, part.astype(bf16)) for part in _split3(lf))

    if chain:
        @pl.when(pl.program_id(1) == 0)
        def _():
            carry_ref[...] = c0_ref[0]
        carry = carry_ref[...]
    else:
        carry = jnp.zeros((1, LANES), f32)
    tri = tri_ref[...]
    tb = x.shape[0] // n_blk
    cums = []
    for r in range(n_blk):
        hi, mid, lo = _split3(lf[r * tb:(r + 1) * tb])
        c = _dot(tri, hi.astype(bf16)) + _dot(tri, mid.astype(bf16)) + _dot(tri, lo.astype(bf16)) + carry
        cums.append(c)
        carry = c[tb - 1:tb, :]
        edge_ref[0, r] = jnp.concatenate([c[0:1, :], carry, jnp.zeros((SUBLANES - 2, LANES), f32)], axis=0)
    cum = cums[0] if n_blk == 1 else jnp.concatenate(cums, axis=0)
    if chain:
        carry_ref[...] = carry

    slot = lax.broadcasted_iota(jnp.int32, (1, LANES), 1) % AUG
    hi, mid, lo = _split3(cum)
    qc = jnp.where(slot == 0, hi, jnp.where(slot == 1, mid, jnp.where(slot == 2, lo,
                   jnp.where(slot <= 6, 1.0, 0.0))))
    kc = jnp.where(slot <= 2, 1.0, jnp.where(slot == 3, -hi, jnp.where(slot == 4, -mid,
                   jnp.where(slot == 5, -lo, jnp.where(slot == 6, negm_ref[...], 0.0)))))
    qcat_ref[0] = qc.astype(bf16)
    kcat_ref[0] = kc.astype(bf16)


def _inproj(x, c0, params, tile, tri, chain, name):
    bx, lx, d = x.shape
    n_tiles = lx // tile
    n_blk = tile // tri.shape[0]
    full = lambda a: pl.BlockSpec(a.shape, lambda b, j: (0,) * a.ndim)
    row = lambda w: pl.BlockSpec((1, tile, w), lambda b, j: (b, j, 0))
    sd = jax.ShapeDtypeStruct
    rows = lambda w, t: (row(w), sd((bx, lx, w), t))
    outs = [rows(D_ATT, bf16), rows(D_ATT, bf16), rows(D_ATT, bf16), rows(D_ATT, f32), rows(D_ATT, f32),
            (pl.BlockSpec((1, N_HEADS, tile), lambda b, j: (b, 0, j)), sd((bx, N_HEADS, lx), f32)),
            (pl.BlockSpec((1, n_blk, SUBLANES, LANES), lambda b, j: (b, j, 0, 0)),
             sd((bx, n_tiles * n_blk, SUBLANES, LANES), f32)),
            rows(LANES, bf16), rows(LANES, bf16), rows(D_ATT, bf16), rows(D_SSM, f32), rows(D_SSM, bf16)]
    g_in, wm, wf, bfr, qg, kg, eh, negm, sel = params
    operands = (g_in, wm, wf, bfr, qg, kg, eh, tri, negm, sel)
    return pl.pallas_call(
        functools.partial(_inproj_kernel, chain, n_blk),
        grid=(bx, n_tiles),
        in_specs=[row(d), pl.BlockSpec((1, 1, LANES), lambda b, j: (b, 0, 0))] + [full(a) for a in operands],
        out_specs=[o[0] for o in outs],
        out_shape=[o[1] for o in outs],
        scratch_shapes=[pltpu.VMEM((1, LANES), f32)],
        compiler_params=_cparams("parallel", "arbitrary"),
        name=name,
    )(x, c0, *operands)


def _attn_kernel(tile, nq, n_meta, jmin_ref, mflag_ref, q_ref, qc_ref, kb_ref, vb_ref, kc_ref,
                 kbm_ref, vbm_ref, kcm_ref, o_ref, acc_ref):
    b = pl.program_id(0)
    p = pl.program_id(1)
    i = pl.program_id(2)
    lane = lax.broadcasted_iota(jnp.int32, (1, LANES), 1)
    qt = q_ref[0].astype(f32)
    qct = qc_ref[0].astype(f32)
    row = lax.broadcasted_iota(jnp.int32, (tile, tile), 0)
    col = lax.broadcasted_iota(jnp.int32, (tile, tile), 1)
    mcol = lax.broadcasted_iota(jnp.int32, (tile, LANES), 1)
    heads = range(2)

    def rows_of(j):
        return pl.ds(pl.multiple_of(j * tile, tile), tile)

    def probs(q_op, ks, kc, mask):
        s = _dot_nt(q_op, jnp.concatenate([ks, kc], axis=1))
        if mask is not None:
            s = jnp.where(mask, s, NEG_INF)
        return jnp.exp(s).astype(bf16)

    def probs_at(q_op, j, mask):
        return probs(q_op, kb_ref[0, rows_of(j), :], kc_ref[0, rows_of(j), :], mask)

    def pv(pe, vs):
        v_op = jnp.concatenate([vs, jnp.ones(vs.shape, bf16)], axis=1)
        return _dot(pe, v_op)

    q_ops, idxs = [], []
    for hh in heads:
        h = 2 * p + hh
        qm = jnp.where(lane // HEAD_DIM == hh, qt, 0.0).astype(bf16)
        cm = jnp.where(lane // AUG == h, qct, 0.0).astype(bf16)
        q_ops.append(jnp.concatenate([qm, cm], axis=1))
        idxs.append((b * N_HEADS + h) * nq + i)

    first = [probs_at(q_ops[hh], i, col <= row) for hh in heads]
    last = []
    for hh in heads:
        acc_ref[hh] = jnp.zeros(acc_ref.shape[1:], f32)

        def body(j, carry, hh=hh):
            pe, jprev = carry
            pn = probs_at(q_ops[hh], j, None)
            acc_ref[hh] += pv(pe, vb_ref[0, rows_of(jprev), :])
            return pn, j

        last.append(lax.fori_loop(jmin_ref[idxs[hh]], i, body, (first[hh], i)))
    for hh in heads:
        pe, jlast = last[hh]
        acc_ref[hh] += pv(pe, vb_ref[0, rows_of(jlast), :])
    for hh in heads:
        @pl.when(mflag_ref[idxs[hh]] > 0)
        def _(hh=hh):
            acc_ref[hh] += pv(probs(q_ops[hh], kbm_ref[...], kcm_ref[...], mcol < n_meta), vbm_ref[...])

    outs = [acc_ref[hh][:, :LANES] / acc_ref[hh][:, LANES:] for hh in heads]
    o_ref[0] = jnp.where(lane < HEAD_DIM, outs[0], outs[1]).astype(bf16)


def _prompt_attention(q, qcat, kb, vb, kcat, kbm, vbm, kcm, jmin, mflag, n_meta):
    b, l, _ = q.shape
    tile = min(ATT_TILE, l)
    nq = l // tile
    grid_spec = pltpu.PrefetchScalarGridSpec(
        num_scalar_prefetch=2,
        grid=(b, N_HEADS // 2, nq),
        in_specs=[
            pl.BlockSpec((1, tile, LANES), lambda b, p, i, *_: (b, i, p)),
            pl.BlockSpec((1, tile, LANES), lambda b, p, i, *_: (b, i, 0)),
            pl.BlockSpec((1, l, LANES), lambda b, p, i, *_: (b, 0, p)),
            pl.BlockSpec((1, l, LANES), lambda b, p, i, *_: (b, 0, p)),
            pl.BlockSpec((1, l, LANES), lambda b, p, i, *_: (b, 0, 0)),
            pl.BlockSpec((LANES, LANES), lambda b, p, i, *_: (0, p)),
            pl.BlockSpec((LANES, LANES), lambda b, p, i, *_: (0, p)),
            pl.BlockSpec((LANES, LANES), lambda b, p, i, *_: (0, 0)),
        ],
        out_specs=pl.BlockSpec((1, tile, LANES), lambda b, p, i, *_: (b, i, p)),
        scratch_shapes=[pltpu.VMEM((2, tile, 2 * LANES), f32)],
    )
    return pl.pallas_call(
        functools.partial(_attn_kernel, tile, nq, n_meta),
        grid_spec=grid_spec,
        out_shape=jax.ShapeDtypeStruct((b, l, D_ATT), bf16),
        compiler_params=_cparams("parallel", "parallel", "arbitrary"),
        name="prompt_attention",
    )(jmin, mflag, q, qcat, kb, vb, kcat, kbm, vbm, kcm)


def _sattn_kernel(past, tn, q_ref, kn_ref, vn_ref, ck_ref, cv_ref, lft_ref, o_ref):
    s_len = lft_ref.shape[-1]
    rows = N_HEADS * tn
    c = lft_ref[0]
    lane_h = lax.broadcasted_iota(jnp.int32, c.shape, 1)
    sh = 1
    while sh < s_len:
        c = c + jnp.where(lane_h >= sh, pltpu.roll(c, sh, axis=1), 0.0)
        sh *= 2
    cexp = jnp.concatenate([jnp.broadcast_to(c[h:h + 1, :], (tn, s_len)) for h in range(N_HEADS)], axis=0)
    rix = lax.broadcasted_iota(jnp.int32, (rows, s_len), 0)
    lix = lax.broadcasted_iota(jnp.int32, (rows, s_len), 1)
    qpos = past + rix % tn
    cq = jnp.sum(jnp.where(lix == qpos, cexp, 0.0), axis=1, keepdims=True)
    pad = jnp.zeros((s_len - past - tn, D_ATT), bf16)
    k_all = jnp.concatenate([ck_ref[0].astype(bf16), kn_ref[0], pad], axis=0)
    v_all = jnp.concatenate([cv_ref[0].astype(bf16), vn_ref[0], pad], axis=0)
    qrep = jnp.concatenate([q_ref[0].astype(f32)] * N_HEADS, axis=0)
    r5 = lax.broadcasted_iota(jnp.int32, (rows, D_ATT), 0)
    l5 = lax.broadcasted_iota(jnp.int32, (rows, D_ATT), 1)
    qrows = jnp.where(l5 // HEAD_DIM == r5 // tn, qrep, 0.0).astype(bf16)
    s = _dot_nt(qrows, k_all) + (cq - cexp)
    s = jnp.where(lix <= qpos, s, NEG_INF)
    m = jnp.max(s, axis=1, keepdims=True)
    pe = jnp.exp(s - m)
    den = jnp.sum(pe, axis=1, keepdims=True)
    o = _dot(pe.astype(bf16), v_all) / den
    lo = lax.broadcasted_iota(jnp.int32, (tn, D_ATT), 1)
    out = jnp.zeros((tn, D_ATT), f32)
    for h in range(N_HEADS):
        out = out + jnp.where(lo // HEAD_DIM == h, o[h * tn:(h + 1) * tn, :], 0.0)
    o_ref[0] = out.astype(bf16)


def _sample_attention(q, kn, vn, ck, cv, lft):
    bs, tn, _ = q.shape
    past = ck.shape[1]
    s_len = lft.shape[-1]
    blk = lambda n, w: pl.BlockSpec((1, n, w), lambda b: (b, 0, 0))
    return pl.pallas_call(
        functools.partial(_sattn_kernel, past, tn),
        grid=(bs,),
        in_specs=[blk(tn, D_ATT), blk(tn, D_ATT), blk(tn, D_ATT), blk(past, D_ATT), blk(past, D_ATT),
                  blk(N_HEADS, s_len)],
        out_specs=blk(tn, D_ATT),
        out_shape=jax.ShapeDtypeStruct((bs, tn, D_ATT), bf16),
        compiler_params=_cparams("parallel"),
        name="sample_attention",
    )(q, kn, vn, ck, cv, lft)


def _s5prep_kernel(n_sq, are_ref, aim_ref, ldt_ref, btr_ref, bti_ref,
                   abr_ref, abi_ref, apr_ref, api_ref, bbr_ref, bbi_ref):
    are = are_ref[...]
    aim = aim_ref[...]
    dt = jnp.exp(ldt_ref[...])
    mag = jnp.exp(are * dt)
    ph = aim * dt
    abr = mag * jnp.cos(ph)
    abi = mag * jnp.sin(ph)
    nr = abr - 1.0
    den = are * are + aim * aim
    cr = (nr * are + abi * aim) / den
    ci = (abi * are - nr * aim) / den
    abr_ref[...] = abr
    abi_ref[...] = abi
    for g in range(are.shape[0]):
        btr = btr_ref[g]
        bti = bti_ref[g]
        bbr_ref[g] = cr[g:g + 1, :] * btr - ci[g:g + 1, :] * bti
        bbi_ref[g] = cr[g:g + 1, :] * bti + ci[g:g + 1, :] * btr
    pr, pi = abr, abi
    for _ in range(n_sq):
        pr, pi = pr * pr - pi * pi, 2.0 * pr * pi
    apr_ref[...] = pr
    api_ref[...] = pi


def _s5_prep(a_re, a_im, log_dt, bt_re, bt_im, n_sq):
    g, p = a_re.shape
    sd = jax.ShapeDtypeStruct
    return pl.pallas_call(
        functools.partial(_s5prep_kernel, n_sq),
        out_shape=[sd((g, p), f32)] * 4 + [sd(bt_re.shape, f32)] * 2,
        name="s5_prep",
    )(a_re, a_im, log_dt, bt_re, bt_im)


def _s5_pitch(seg_len):
    return seg_len + SUBLANES if (seg_len // SUBLANES) % 2 == 0 else seg_len


def _s5_kernel(chain, n_seg, seg_len, u_ref, bblk_ref, cblk_ref, ar_ref, ai_ref, apr_ref, api_ref, d_ref,
               x0r_ref, x0i_ref, y_ref, sr_ref, si_ref, ub_ref, yb_ref, xr_ref, xi_ref, cr_ref, ci_ref):
    pitch = _s5_pitch(seg_len)
    ar = ar_ref[0]
    ai = ai_ref[0]
    bblk = bblk_ref[0]
    cblk = cblk_ref[0]
    dd = d_ref[0]

    def seg_rows(c, n):
        return pl.ds(pl.multiple_of(c * n, SUBLANES), seg_len)

    def stage_in(c, carry):
        ub_ref[seg_rows(c, pitch), :] = u_ref[0, seg_rows(c, seg_len), :]
        return carry

    def stage_out(c, carry):
        y_ref[0, seg_rows(c, seg_len), :] = yb_ref[seg_rows(c, pitch), :]
        return carry

    def step(with_y, s):
        rows = pl.ds(s, n_seg, stride=pitch)
        us = ub_ref[rows, :]
        bu = _dot(us.astype(bf16), bblk)
        xr = xr_ref[...]
        xi = xi_ref[...]
        nxr = ar * xr - ai * xi + bu[:, :STATE_BLOCK]
        nxi = ar * xi + ai * xr + bu[:, STATE_BLOCK:]
        xr_ref[...] = nxr
        xi_ref[...] = nxi
        if with_y:
            xc = jnp.concatenate([nxr, nxi], axis=1).astype(bf16)
            yb_ref[rows, :] = _dot(xc, cblk) + dd * us

    unroll = min(S5_UNROLL, seg_len)

    def steps(with_y, g, carry):
        for sg in range(unroll):
            step(with_y, g * unroll + sg)
        return carry

    lax.fori_loop(0, n_seg, stage_in, 0)
    if chain:
        xr_ref[...] = jnp.zeros(xr_ref.shape, f32)
        xi_ref[...] = jnp.zeros(xi_ref.shape, f32)
        lax.fori_loop(0, seg_len // unroll, functools.partial(steps, False), 0)
        apr = apr_ref[0]
        api = api_ref[0]

        def scan(c, carry):
            kr, ki = carry
            cr_ref[pl.ds(c, 1), :] = kr
            ci_ref[pl.ds(c, 1), :] = ki
            er = xr_ref[pl.ds(c, 1), :]
            ei = xi_ref[pl.ds(c, 1), :]
            return apr * kr - api * ki + er, apr * ki + api * kr + ei

        kr, ki = lax.fori_loop(0, n_seg, scan, (x0r_ref[0], x0i_ref[0]))
        sr_ref[0] = kr
        si_ref[0] = ki
        xr_ref[...] = cr_ref[...]
        xi_ref[...] = ci_ref[...]
        lax.fori_loop(0, seg_len // unroll, functools.partial(steps, True), 0)
    else:
        xr_ref[...] = x0r_ref[0]
        xi_ref[...] = x0i_ref[0]
        lax.fori_loop(0, seg_len // unroll, functools.partial(steps, True), 0)
        sr_ref[0] = xr_ref[...]
        si_ref[0] = xi_ref[...]
    lax.fori_loop(0, n_seg, stage_out, 0)


def _s5(u, x0r, x0i, bblk, cblk, ar, ai, apr, api, dblk, n_seg, chain, name):
    bq, l, _ = u.shape
    seg_len = l // n_seg
    r0 = x0r.shape[1]
    per_k = lambda a: pl.BlockSpec((1,) + a.shape[1:], lambda b, k: (k, 0, 0))
    st = pl.BlockSpec((1, r0, STATE_BLOCK), lambda b, k: (b, 0, k))
    tok = lambda **kw: pl.BlockSpec((1, l, LANES), lambda b, k: (b, 0, k), **kw)
    sd = jax.ShapeDtypeStruct
    stage = pltpu.VMEM((n_seg * _s5_pitch(seg_len), LANES), f32)
    return pl.pallas_call(
        functools.partial(_s5_kernel, chain, n_seg, seg_len),
        grid=(bq, N_SSM_BLOCKS),
        in_specs=[tok(pipeline_mode=pl.Buffered(1)), per_k(bblk), per_k(cblk), per_k(ar), per_k(ai), per_k(apr),
                  per_k(api), per_k(dblk), st, st],
        out_specs=[tok(), st, st],
        out_shape=[sd((bq, l, D_SSM), f32), sd(x0r.shape, f32), sd(x0i.shape, f32)],
        scratch_shapes=[stage, stage] + [pltpu.VMEM((n_seg, STATE_BLOCK), f32)] * 4,
        compiler_params=_cparams("parallel", "parallel"),
        name=name,
    )(u, bblk, cblk, ar, ai, apr, api, dblk, x0r, x0i)


def _out_kernel(x_ref, att_ref, ga_ref, ys_ref, gs_ref, wg_ref, bg_ref, wo_ref, y_ref):
    ga = ga_ref[0].astype(f32)
    a = att_ref[0].astype(f32) * (ga * jax.nn.sigmoid(ga))
    ys = ys_ref[0]
    z = 0.5 * ys * (1.0 + jnp.tanh(0.7978845608028654 * (ys + 0.044715 * (ys * ys * ys))))
    gate = jax.nn.sigmoid(_dot(z.astype(bf16), wg_ref[...]) + bg_ref[...])
    gs = gs_ref[0].astype(f32)
    s5 = z * gate * (gs * jax.nn.sigmoid(gs))
    cat = jnp.concatenate([a, s5], axis=1).astype(bf16)
    y_ref[0] = x_ref[0] + _dot(cat, wo_ref[...])


def _outproj(x, att, ga, ys, gs, wg, bg, wo, tile, name):
    bx, lx, d = x.shape
    full = lambda a: pl.BlockSpec(a.shape, lambda b, j: (0,) * a.ndim)
    row = lambda w: pl.BlockSpec((1, tile, w), lambda b, j: (b, j, 0))
    return pl.pallas_call(
        _out_kernel,
        grid=(bx, lx // tile),
        in_specs=[row(d), row(D_ATT), row(D_ATT), row(D_SSM), row(D_SSM), full(wg), full(bg), full(wo)],
        out_specs=row(d),
        out_shape=jax.ShapeDtypeStruct(x.shape, f32),
        compiler_params=_cparams("parallel", "parallel"),
        name=name,
    )(x, att, ga, ys, gs, wg, bg, wo)


def _lower_tri(n):
    r = jnp.arange(n)
    return (r[None, :] <= r[:, None]).astype(bf16)


def _block_diag(blocks):
    k, g8, r, c = blocks.shape
    eye = jnp.eye(g8, dtype=blocks.dtype)
    return jnp.einsum("kgrc,gh->kgrhc", blocks, eye).reshape(k, g8 * r, g8 * c)


def kernel(x_prompt, x_sample, cache_k, cache_v, cache_logf, state_s5_re, state_s5_im, meta_tokens, norm_g,
           w_in, b_f, q_norm_g, k_norm_g, s5_a_re, s5_a_im, s5_log_dt, s5_b_re, s5_b_im, s5_c_re, s5_c_im,
           s5_d, w_glu, b_glu, w_out):
    assert norm_g.shape[0] == 1, "single-layer step"
    bp, lp, d = x_prompt.shape
    bs, ts, _ = x_sample.shape
    n_meta = meta_tokens.shape[0]
    past = cache_k.shape[2]
    G, P = N_GROUPS, SSM_STATE

    w = w_in[0]
    o3 = 3 * D_ATT
    o4 = o3 + N_HEADS
    wm = jnp.concatenate([w[:, :o3], w[:, o4:]], axis=1).astype(bf16)
    wf = jnp.repeat(w[:, o3:o4], AUG, axis=1).astype(bf16)
    bfr = jnp.repeat(b_f[0], AUG)[None, :].astype(f32)
    g_in = norm_g[0][None, :].astype(f32)
    qg = (jnp.tile(q_norm_g[0], N_HEADS) * (HEAD_DIM ** -0.5))[None, :].astype(f32)
    kg = jnp.tile(k_norm_g[0], N_HEADS)[None, :].astype(f32)
    hid = jnp.arange(D_ATT) // HEAD_DIM
    eh = ((hid[:, None] == hid[None, :]).astype(f32) / HEAD_DIM).astype(bf16)
    sel = (jnp.arange(LANES)[None, :] == AUG * jnp.arange(N_HEADS)[:, None]).astype(bf16)
    m_bound = (8.0 * jnp.max(jnp.abs(q_norm_g[0])) * jnp.max(jnp.abs(k_norm_g[0])) * 1.02).astype(bf16).astype(f32)
    negm = jnp.full((1, LANES), -1.0, f32) * m_bound
    params = (g_in, wm, wf, bfr, qg, kg, eh, negm, sel)

    tm = n_meta
    meta = _inproj(meta_tokens[None].astype(f32), jnp.zeros((1, 1, LANES), f32), params, tm, _lower_tri(tm), True,
                   "inproj_meta")
    (_, kb_m, vb_m, kf_m, vf_m, lft_m, edge_m, _, kcat_m, _, u_m, _) = meta
    cum_meta_end = edge_m[:, -1, 1:2, :]
    c0 = jnp.broadcast_to(cum_meta_end, (bp, 1, LANES))
    tile_p = min(ROW_TILE, lp)
    tb_p = min(LANES, tile_p)
    (q_p, kb_p, vb_p, kf_p, vf_p, lft_p, edge_p, qcat_p, kcat_p, ga_p, u_p, gs_p) = _inproj(
        x_prompt, c0, params, tile_p, _lower_tri(tb_p), True, "inproj_prompt")
    ls = bs * ts
    tile_s = min(ROW_TILE, ls)
    (q_s, kb_s, vb_s, kf_s, vf_s, lft_s, _, _, _, ga_s, u_s, gs_s) = _inproj(
        x_sample.reshape(1, ls, d), jnp.zeros((1, 1, LANES), f32), params, tile_s,
        _lower_tri(min(LANES, tile_s)), False, "inproj_sample")

    tile_a = min(ATT_TILE, lp)
    nq = lp // tile_a
    per_tile = tile_a // tb_p
    edges = edge_p[:, :, :2, ::AUG]
    q_start = edges[:, 0::per_tile, 0, :]
    k_end = edges[:, per_tile - 1::per_tile, 1, :]
    dmax = q_start[:, :, None, :] - k_end[:, None, :, :]
    jmin = jnp.sum((dmax < -EXP_ZERO).astype(jnp.int32), axis=2)
    jmin = jnp.minimum(jmin, jnp.arange(nq, dtype=jnp.int32)[None, :, None])
    jmin = jnp.transpose(jmin, (0, 2, 1)).reshape(-1)
    mflag = (q_start - cum_meta_end[:, :, ::AUG] >= -EXP_ZERO).astype(jnp.int32)
    mflag = jnp.transpose(mflag, (0, 2, 1)).reshape(-1)
    padm = lambda a: jnp.pad(a[0], ((0, LANES - tm), (0, 0)))
    att_p = _prompt_attention(q_p, qcat_p, kb_p, vb_p, kcat_p, padm(kb_m), padm(vb_m), padm(kcat_m),
                              jmin, mflag, n_meta)

    lft_new = jnp.transpose(lft_s.reshape(N_HEADS, bs, ts), (1, 0, 2))
    s_len = -(-(past + ts) // LANES) * LANES
    lft = jnp.concatenate([jnp.transpose(cache_logf[0].astype(f32), (0, 2, 1)), lft_new,
                           jnp.zeros((bs, N_HEADS, s_len - past - ts), f32)], axis=2)
    att_s = _sample_attention(q_s.reshape(bs, ts, D_ATT), kb_s.reshape(bs, ts, D_ATT), vb_s.reshape(bs, ts, D_ATT),
                              cache_k[0].reshape(bs, past, D_ATT), cache_v[0].reshape(bs, past, D_ATT), lft)

    n_seg_p = min(PROMPT_SEGMENTS, lp)
    seg_len_p = lp // n_seg_p
    n_sq = seg_len_p.bit_length() - 1
    assert seg_len_p == 1 << n_sq
    bt_re = jnp.transpose(s5_b_re[0], (0, 2, 1)).astype(f32)
    bt_im = jnp.transpose(s5_b_im[0], (0, 2, 1)).astype(f32)
    abr, abi, apr, api, bbr, bbi = _s5_prep(s5_a_re[0].astype(f32), s5_a_im[0].astype(f32),
                                            s5_log_dt[0][:, None].astype(f32), bt_re, bt_im, n_sq)
    K8 = (N_SSM_BLOCKS, GROUPS_PER_BLOCK)
    bblk = jnp.concatenate([_block_diag(bbr.reshape(K8 + (SSM_GROUP, P))),
                            _block_diag(bbi.reshape(K8 + (SSM_GROUP, P)))], axis=2).astype(bf16)
    ct_re = jnp.transpose(s5_c_re[0], (0, 2, 1)).astype(f32)
    ct_im = jnp.transpose(s5_c_im[0], (0, 2, 1)).astype(f32)
    cblk = jnp.concatenate([_block_diag(ct_re.reshape(K8 + (P, SSM_GROUP))),
                            _block_diag(-ct_im.reshape(K8 + (P, SSM_GROUP)))], axis=1).astype(bf16)
    lane_k = lambda a: a.reshape(N_SSM_BLOCKS, 1, -1)
    ar, ai, aprk, apik = lane_k(abr), lane_k(abi), lane_k(apr), lane_k(api)
    dblk = lane_k(s5_d[0].astype(f32))

    assert tm == ts
    n_small = -(-(bs + 1) // SUBLANES) * SUBLANES
    u_small = jnp.concatenate([u_s, u_m, jnp.zeros((1, (n_small - bs - 1) * ts, D_SSM), f32)], axis=1)
    pad_state = lambda a: jnp.pad(a[0].reshape(bs, G * P).astype(f32), ((0, n_small - bs), (0, 0)))[None]
    ys_small, sr_small, si_small = _s5(u_small, pad_state(state_s5_re), pad_state(state_s5_im), bblk, cblk,
                                       ar, ai, ar, ai, dblk, n_small, False, "s5_short")
    x0r = jnp.broadcast_to(sr_small[:, bs:bs + 1, :], (bp, 1, G * P))
    x0i = jnp.broadcast_to(si_small[:, bs:bs + 1, :], (bp, 1, G * P))
    ys_p, sr_p, si_p = _s5(u_p, x0r, x0i, bblk, cblk, ar, ai, aprk, apik, dblk, n_seg_p, True, "s5_prompt")

    wg = w_glu[0].astype(bf16)
    bg = b_glu[0][None, :].astype(f32)
    wo = w_out[0].astype(bf16)
    y_prompt = _outproj(x_prompt, att_p, ga_p, ys_p, gs_p, wg, bg, wo, tile_p, "outproj_prompt")
    y_sample = _outproj(x_sample.reshape(1, ls, d), att_s.reshape(1, ls, D_ATT), ga_s, ys_small[:, :ls], gs_s,
                        wg, bg, wo, tile_s, "outproj_sample").reshape(bs, ts, d)

    def heads_with_meta(m, x):
        m = jnp.broadcast_to(m.reshape(1, n_meta, N_HEADS, HEAD_DIM), (bp, n_meta, N_HEADS, HEAD_DIM))
        return jnp.concatenate([m, x.reshape(bp, lp, N_HEADS, HEAD_DIM)], axis=1)[None]

    new_k_prompt = heads_with_meta(kf_m, kf_p)
    new_v_prompt = heads_with_meta(vf_m, vf_p)
    lft_all = jnp.concatenate([jnp.broadcast_to(lft_m, (bp, N_HEADS, n_meta)), lft_p], axis=2)
    new_logf_prompt = jnp.transpose(lft_all, (0, 2, 1))[None]
    new_s5_re_prompt = sr_p.reshape(1, bp, G, P)
    new_s5_im_prompt = si_p.reshape(1, bp, G, P)
    new_k_sample = kf_s.reshape(1, bs, ts, N_HEADS, HEAD_DIM)
    new_v_sample = vf_s.reshape(1, bs, ts, N_HEADS, HEAD_DIM)
    new_logf_sample = jnp.transpose(lft_new, (0, 2, 1))[None]
    new_s5_re_sample = sr_small[:, :bs].reshape(1, bs, G, P)
    new_s5_im_sample = si_small[:, :bs].reshape(1, bs, G, P)
    return (y_prompt, y_sample, new_k_prompt, new_v_prompt, new_logf_prompt, new_s5_re_prompt, new_s5_im_prompt,
            new_k_sample, new_v_sample, new_logf_sample, new_s5_re_sample, new_s5_im_sample)
```

```python
import functools

import jax
import jax.numpy as jnp
from jax import lax
from jax.experimental import pallas as pl
from jax.experimental.pallas import tpu as pltpu

N_HEADS = 8
HEAD_DIM = 64
D_ATT = N_HEADS * HEAD_DIM
SSM_GROUP = 16
SSM_STATE = 64
D_SSM = 512
N_GROUPS = D_SSM // SSM_GROUP
EPS = 1e-6
NEG_INF = -1e30

LANES = 128
SUBLANES = 8
AUG = LANES // N_HEADS
GROUPS_PER_BLOCK = LANES // SSM_GROUP
STATE_BLOCK = GROUPS_PER_BLOCK * SSM_STATE
N_SSM_BLOCKS = D_SSM // LANES
VMEM_LIMIT = 56 * 1024 * 1024
EXP_ZERO = 105.0

ROW_TILE = 512
ATT_TILE = 512
ATT_GROUP = 8
ATT_UNROLL = 4
LOG2E = 1.4426950408889634
PROMPT_SEGMENTS = 256
S5_UNROLL = 8

bf16 = jnp.bfloat16
f32 = jnp.float32


def _cparams(*sem):
    return pltpu.CompilerParams(dimension_semantics=sem, vmem_limit_bytes=VMEM_LIMIT)


def _split3(x):
    hi = x.astype(bf16).astype(f32)
    r1 = x - hi
    mid = r1.astype(bf16).astype(f32)
    lo = (r1 - mid).astype(bf16).astype(f32)
    return hi, mid, lo


def _dot(a, b):
    return jnp.dot(a, b, preferred_element_type=f32)


def _dot_nt(a, b):
    return lax.dot_general(a, b, (((1,), (1,)), ((), ())), preferred_element_type=f32)


def _inproj_kernel(chain, n_blk, x_ref, c0_ref, g_ref, wm_ref, wf_ref, bf_ref, qg_ref, kg_ref, eh_ref,
                   tri_ref, negm_ref, sel_ref, q_ref, kb_ref, vb_ref, kf_ref, vf_ref, lft_ref, edge_ref,
                   qcat_ref, kcat_ref, ga_ref, u_ref, gs_ref, carry_ref):
    x = x_ref[0]
    ms = jnp.mean(x * x, axis=-1, keepdims=True)
    xn = (x * lax.rsqrt(ms + EPS) * g_ref[...]).astype(bf16)
    z = _dot(xn, wm_ref[...])
    q = z[:, 0 * D_ATT:1 * D_ATT]
    k = z[:, 1 * D_ATT:2 * D_ATT]
    v = z[:, 2 * D_ATT:3 * D_ATT]
    eh = eh_ref[...]
    qn = q * lax.rsqrt(_dot((q * q).astype(bf16), eh) + EPS) * qg_ref[...]
    kn = k * lax.rsqrt(_dot((k * k).astype(bf16), eh) + EPS) * kg_ref[...]
    q_ref[0] = qn.astype(bf16)
    kf_ref[0] = kn
    kb_ref[0] = kn.astype(bf16)
    vf_ref[0] = v
    vb_ref[0] = v.astype(bf16)
    ga_ref[0] = z[:, 3 * D_ATT:4 * D_ATT].astype(bf16)
    u_ref[0] = z[:, 4 * D_ATT:4 * D_ATT + D_SSM]
    gs_ref[0] = z[:, 4 * D_ATT + D_SSM:].astype(bf16)

    zf = _dot(xn, wf_ref[...]) + bf_ref[...]
    lf = jnp.minimum(zf, 0.0) - jnp.log1p(jnp.exp(-jnp.abs(zf)))
    sel = sel_ref[...]
    lft_ref[0] = sum(_dot_nt(sel, part.astype(bf16)) for part in _split3(lf))

    if chain:
        @pl.when(pl.program_id(1) == 0)
        def _():
            carry_ref[...] = c0_ref[0]
        carry = carry_ref[...]
    else:
        carry = jnp.zeros((1, LANES), f32)
    tri = tri_ref[...]
    tb = x.shape[0] // n_blk
    cums = []
    for r in range(n_blk):
        hi, mid, lo = _split3(lf[r * tb:(r + 1) * tb])
        c = _dot(tri, hi.astype(bf16)) + _dot(tri, mid.astype(bf16)) + _dot(tri, lo.astype(bf16)) + carry
        cums.append(c)
        carry = c[tb - 1:tb, :]
        edge_ref[0, r] = jnp.concatenate([c[0:1, :], carry, jnp.zeros((SUBLANES - 2, LANES), f32)], axis=0)
    cum = cums[0] if n_blk == 1 else jnp.concatenate(cums, axis=0)
    if chain:
        carry_ref[...] = carry

    slot = lax.broadcasted_iota(jnp.int32, (1, LANES), 1) % AUG
    hi, mid, lo = _split3(cum * LOG2E)
    qc = jnp.where(slot == 0, hi, jnp.where(slot == 1, mid, jnp.where(slot == 2, lo,
                   jnp.where(slot <= 6, 1.0, 0.0))))
    kc = jnp.where(slot <= 2, 1.0, jnp.where(slot == 3, -hi, jnp.where(slot == 4, -mid,
                   jnp.where(slot == 5, -lo, jnp.where(slot == 6, negm_ref[...], 0.0)))))
    qcat_ref[0] = qc.astype(bf16)
    kcat_ref[0] = kc.astype(bf16)


def _inproj(x, c0, params, tile, tri, chain, name):
    bx, lx, d = x.shape
    n_tiles = lx // tile
    n_blk = tile // tri.shape[0]
    full = lambda a: pl.BlockSpec(a.shape, lambda b, j: (0,) * a.ndim)
    row = lambda w: pl.BlockSpec((1, tile, w), lambda b, j: (b, j, 0))
    sd = jax.ShapeDtypeStruct
    rows = lambda w, t: (row(w), sd((bx, lx, w), t))
    outs = [rows(D_ATT, bf16), rows(D_ATT, bf16), rows(D_ATT, bf16), rows(D_ATT, f32), rows(D_ATT, f32),
            (pl.BlockSpec((1, N_HEADS, tile), lambda b, j: (b, 0, j)), sd((bx, N_HEADS, lx), f32)),
            (pl.BlockSpec((1, n_blk, SUBLANES, LANES), lambda b, j: (b, j, 0, 0)),
             sd((bx, n_tiles * n_blk, SUBLANES, LANES), f32)),
            rows(LANES, bf16), rows(LANES, bf16), rows(D_ATT, bf16), rows(D_SSM, f32), rows(D_SSM, bf16)]
    g_in, wm, wf, bfr, qg, kg, eh, negm, sel = params
    operands = (g_in, wm, wf, bfr, qg, kg, eh, tri, negm, sel)
    return pl.pallas_call(
        functools.partial(_inproj_kernel, chain, n_blk),
        grid=(bx, n_tiles),
        in_specs=[row(d), pl.BlockSpec((1, 1, LANES), lambda b, j: (b, 0, 0))] + [full(a) for a in operands],
        out_specs=[o[0] for o in outs],
        out_shape=[o[1] for o in outs],
        scratch_shapes=[pltpu.VMEM((1, LANES), f32)],
        compiler_params=_cparams("parallel", "arbitrary"),
        name=name,
    )(x, c0, *operands)


def _attn_kernel(tile, nq, gq, n_meta, jmin_ref, mflag_ref, q_ref, qc_ref, kb_ref, vb_ref, kc_ref,
                 kbm_ref, vbm_ref, kcm_ref, o_ref, acc_ref, qop_ref, bnd_ref):
    b = pl.program_id(0)
    p = pl.program_id(1)
    g = pl.program_id(2)
    lane = lax.broadcasted_iota(jnp.int32, (1, LANES), 1)
    mcol = lax.broadcasted_iota(jnp.int32, (tile, LANES), 1)
    col = lax.broadcasted_iota(jnp.int32, (tile, tile), 1)
    row = lax.broadcasted_iota(jnp.int32, (tile, tile), 0)
    @pl.when(g == 0)
    def _():
        bnd_ref[0] = jnp.where(col <= row, -NEG_INF, NEG_INF)
        bnd_ref[1] = jnp.full((tile, tile), -NEG_INF, f32)
        bnd_ref[2] = jnp.full((tile, tile), NEG_INF, f32)

    def rows_of(j):
        return pl.ds(pl.multiple_of(j * tile, tile), tile)

    heads = range(2)
    base = [(b * N_HEADS + 2 * p + hh) * nq + g * gq for hh in heads]
    m_q = [(lane // HEAD_DIM == hh).astype(bf16) for hh in heads]
    m_o = [(lane // HEAD_DIM != hh).astype(bf16) for hh in heads]
    m_c = [(lane // AUG == 2 * p + hh).astype(bf16) for hh in heads]

    def stage_q(r, carry):
        for hh in heads:
            qop_ref[hh, r] = jnp.concatenate([q_ref[0, rows_of(r), :] * m_q[hh], qc_ref[0, rows_of(r), :] * m_c[hh]],
                                             axis=1)
            acc_ref[hh, r] = jnp.zeros(acc_ref.shape[2:], f32)
        return carry

    lax.fori_loop(0, gq, stage_q, 0)

    def scores(hh, rc, ks, kc):
        return _dot_nt(qop_ref[hh, rc], jnp.concatenate([ks, kc], axis=1))

    def pv(hh, pe, vs):
        return _dot(pe, vs * m_q[hh] + m_o[hh])

    def item_probs(hh, r, j):
        rc = jnp.minimum(r, gq - 1)
        kind = jnp.where(r < gq, jnp.where(j == g * gq + rc, 0, 1), 2)
        s = scores(hh, rc, kb_ref[0, rows_of(j), :], kc_ref[0, rows_of(j), :])
        return jnp.exp2(jnp.minimum(s, bnd_ref[kind])).astype(bf16), rc

    def item_pv(hh, pe, rc, j):
        acc_ref[hh, rc] += pv(hh, pe, vb_ref[0, rows_of(j), :])

    def advance(hh, r, j):
        last = jnp.logical_and(r < gq, j == g * gq + r)
        r2 = jnp.where(last, r + 1, r)
        j2 = jnp.where(last, jmin_ref[base[hh] + jnp.minimum(r + 1, gq - 1)], jnp.where(r < gq, j + 1, j))
        return r2.astype(jnp.int32), j2.astype(jnp.int32)

    def first_item(hh):
        j0 = jmin_ref[base[hh]]
        pe0, rc0 = item_probs(hh, jnp.int32(0), j0)
        return (pe0, rc0, j0) + advance(hh, jnp.int32(0), j0)

    def pipeline(hh, carry):
        def body(t, carry):
            pe, rp, jp, r, j = carry
            for _ in range(ATT_UNROLL):
                pn, rn = item_probs(hh, r, j)
                item_pv(hh, pe, rp, jp)
                pe, rp, jp = pn, rn, j
                r, j = advance(hh, r, j)
            return pe, rp, jp, r, j

        n_items = sum(g * gq + r - jmin_ref[base[hh] + r] + 1 for r in range(gq))
        trips = (n_items + ATT_UNROLL - 2) // ATT_UNROLL
        return lax.fori_loop(0, trips, body, carry)[:3]

    tail0 = pipeline(0, first_item(0))
    head1 = first_item(1)
    item_pv(0, *tail0)
    item_pv(1, *pipeline(1, head1))

    def meta_tile(r, carry):
        for hh in heads:
            @pl.when(mflag_ref[base[hh] + r] > 0)
            def _(hh=hh):
                s = jnp.where(mcol < n_meta, scores(hh, r, kbm_ref[...], kcm_ref[...]), NEG_INF)
                acc_ref[hh, r] += pv(hh, jnp.exp2(s).astype(bf16), vbm_ref[...])
        return carry

    lax.fori_loop(0, gq, meta_tile, 0)

    for r in range(gq):
        outs = []
        for hh in heads:
            a = acc_ref[hh, r]
            outs.append(a / pltpu.roll(a, HEAD_DIM, axis=1))
        o_ref[0, r * tile:(r + 1) * tile, :] = jnp.where(lane < HEAD_DIM, outs[0], outs[1]).astype(bf16)


def _prompt_attention(q, qcat, kb, vb, kcat, kbm, vbm, kcm, jmin, mflag, n_meta):
    b, l, _ = q.shape
    tile = min(ATT_TILE, l)
    nq = l // tile
    gq = min(ATT_GROUP, nq)
    rows = gq * tile
    grid_spec = pltpu.PrefetchScalarGridSpec(
        num_scalar_prefetch=2,
        grid=(b, N_HEADS // 2, nq // gq),
        in_specs=[
            pl.BlockSpec((1, rows, LANES), lambda b, p, i, *_: (b, i, p)),
            pl.BlockSpec((1, rows, LANES), lambda b, p, i, *_: (b, i, 0)),
            pl.BlockSpec((1, l, LANES), lambda b, p, i, *_: (b, 0, p)),
            pl.BlockSpec((1, l, LANES), lambda b, p, i, *_: (b, 0, p)),
            pl.BlockSpec((1, l, LANES), lambda b, p, i, *_: (b, 0, 0)),
            pl.BlockSpec((LANES, LANES), lambda b, p, i, *_: (0, p)),
            pl.BlockSpec((LANES, LANES), lambda b, p, i, *_: (0, p)),
            pl.BlockSpec((LANES, LANES), lambda b, p, i, *_: (0, 0)),
        ],
        out_specs=pl.BlockSpec((1, rows, LANES), lambda b, p, i, *_: (b, i, p)),
        scratch_shapes=[pltpu.VMEM((2, gq, tile, LANES), f32), pltpu.VMEM((2, gq, tile, 2 * LANES), bf16),
                        pltpu.VMEM((3, tile, tile), f32)],
    )
    return pl.pallas_call(
        functools.partial(_attn_kernel, tile, nq, gq, n_meta),
        grid_spec=grid_spec,
        out_shape=jax.ShapeDtypeStruct((b, l, D_ATT), bf16),
        compiler_params=_cparams("parallel", "parallel", "arbitrary"),
        name="prompt_attention",
    )(jmin, mflag, q, qcat, kb, vb, kcat, kbm, vbm, kcm)


def _sattn_kernel(past, tn, q_ref, kn_ref, vn_ref, ck_ref, cv_ref, lft_ref, o_ref):
    s_len = lft_ref.shape[-1]
    rows = N_HEADS * tn
    c = lft_ref[0]
    lane_h = lax.broadcasted_iota(jnp.int32, c.shape, 1)
    sh = 1
    while sh < s_len:
        c = c + jnp.where(lane_h >= sh, pltpu.roll(c, sh, axis=1), 0.0)
        sh *= 2
    cexp = jnp.concatenate([jnp.broadcast_to(c[h:h + 1, :], (tn, s_len)) for h in range(N_HEADS)], axis=0)
    rix = lax.broadcasted_iota(jnp.int32, (rows, s_len), 0)
    lix = lax.broadcasted_iota(jnp.int32, (rows, s_len), 1)
    qpos = past + rix % tn
    cq = jnp.sum(jnp.where(lix == qpos, cexp, 0.0), axis=1, keepdims=True)
    pad = jnp.zeros((s_len - past - tn, D_ATT), bf16)
    k_all = jnp.concatenate([ck_ref[0].astype(bf16), kn_ref[0], pad], axis=0)
    v_all = jnp.concatenate([cv_ref[0].astype(bf16), vn_ref[0], pad], axis=0)
    qrep = jnp.concatenate([q_ref[0].astype(f32)] * N_HEADS, axis=0)
    r5 = lax.broadcasted_iota(jnp.int32, (rows, D_ATT), 0)
    l5 = lax.broadcasted_iota(jnp.int32, (rows, D_ATT), 1)
    qrows = jnp.where(l5 // HEAD_DIM == r5 // tn, qrep, 0.0).astype(bf16)
    s = _dot_nt(qrows, k_all) + (cq - cexp) * LOG2E
    s = jnp.where(lix <= qpos, s, NEG_INF)
    m = jnp.max(s, axis=1, keepdims=True)
    pe = jnp.exp2(s - m)
    den = jnp.sum(pe, axis=1, keepdims=True)
    o = _dot(pe.astype(bf16), v_all) / den
    lo = lax.broadcasted_iota(jnp.int32, (tn, D_ATT), 1)
    out = jnp.zeros((tn, D_ATT), f32)
    for h in range(N_HEADS):
        out = out + jnp.where(lo // HEAD_DIM == h, o[h * tn:(h + 1) * tn, :], 0.0)
    o_ref[0] = out.astype(bf16)


def _sample_attention(q, kn, vn, ck, cv, lft):
    bs, tn, _ = q.shape
    past = ck.shape[1]
    s_len = lft.shape[-1]
    blk = lambda n, w: pl.BlockSpec((1, n, w), lambda b: (b, 0, 0))
    return pl.pallas_call(
        functools.partial(_sattn_kernel, past, tn),
        grid=(bs,),
        in_specs=[blk(tn, D_ATT), blk(tn, D_ATT), blk(tn, D_ATT), blk(past, D_ATT), blk(past, D_ATT),
                  blk(N_HEADS, s_len)],
        out_specs=blk(tn, D_ATT),
        out_shape=jax.ShapeDtypeStruct((bs, tn, D_ATT), bf16),
        compiler_params=_cparams("parallel"),
        name="sample_attention",
    )(q, kn, vn, ck, cv, lft)


def _s5prep_kernel(n_sq, are_ref, aim_ref, ldt_ref, btr_ref, bti_ref,
                   abr_ref, abi_ref, apr_ref, api_ref, bbr_ref, bbi_ref):
    are = are_ref[...]
    aim = aim_ref[...]
    dt = jnp.exp(ldt_ref[...])
    mag = jnp.exp(are * dt)
    ph = aim * dt
    abr = mag * jnp.cos(ph)
    abi = mag * jnp.sin(ph)
    nr = abr - 1.0
    den = are * are + aim * aim
    cr = (nr * are + abi * aim) / den
    ci = (abi * are - nr * aim) / den
    abr_ref[...] = abr
    abi_ref[...] = abi
    for g in range(are.shape[0]):
        btr = btr_ref[g]
        bti = bti_ref[g]
        bbr_ref[g] = cr[g:g + 1, :] * btr - ci[g:g + 1, :] * bti
        bbi_ref[g] = cr[g:g + 1, :] * bti + ci[g:g + 1, :] * btr
    pr, pi = abr, abi
    for _ in range(n_sq):
        pr, pi = pr * pr - pi * pi, 2.0 * pr * pi
    apr_ref[...] = pr
    api_ref[...] = pi


def _s5_prep(a_re, a_im, log_dt, bt_re, bt_im, n_sq):
    g, p = a_re.shape
    sd = jax.ShapeDtypeStruct
    return pl.pallas_call(
        functools.partial(_s5prep_kernel, n_sq),
        out_shape=[sd((g, p), f32)] * 4 + [sd(bt_re.shape, f32)] * 2,
        name="s5_prep",
    )(a_re, a_im, log_dt, bt_re, bt_im)


def _s5_pitch(seg_len):
    return seg_len + SUBLANES if (seg_len // SUBLANES) % 2 == 0 else seg_len


def _s5_kernel(chain, n_seg, seg_len, u_ref, bblk_ref, cblk_ref, ar_ref, ai_ref, apr_ref, api_ref, d_ref,
               x0r_ref, x0i_ref, y_ref, sr_ref, si_ref, ub_ref, yb_ref, xr_ref, xi_ref, cr_ref, ci_ref):
    pitch = _s5_pitch(seg_len)
    ar = ar_ref[0]
    ai = ai_ref[0]
    bblk = bblk_ref[0]
    cblk = cblk_ref[0]
    dd = d_ref[0]

    def seg_rows(c, n):
        return pl.ds(pl.multiple_of(c * n, SUBLANES), seg_len)

    def stage_in(c, carry):
        ub_ref[seg_rows(c, pitch), :] = u_ref[0, seg_rows(c, seg_len), :]
        return carry

    def stage_out(c, carry):
        y_ref[0, seg_rows(c, seg_len), :] = yb_ref[seg_rows(c, pitch), :]
        return carry

    def step(with_y, s):
        rows = pl.ds(s, n_seg, stride=pitch)
        us = ub_ref[rows, :]
        bu = _dot(us.astype(bf16), bblk)
        xr = xr_ref[...]
        xi = xi_ref[...]
        nxr = ar * xr - ai * xi + bu[:, :STATE_BLOCK]
        nxi = ar * xi + ai * xr + bu[:, STATE_BLOCK:]
        xr_ref[...] = nxr
        xi_ref[...] = nxi
        if with_y:
            xc = jnp.concatenate([nxr, nxi], axis=1).astype(bf16)
            yb_ref[rows, :] = _dot(xc, cblk) + dd * us

    unroll = min(S5_UNROLL, seg_len)

    def steps(with_y, g, carry):
        for sg in range(unroll):
            step(with_y, g * unroll + sg)
        return carry

    lax.fori_loop(0, n_seg, stage_in, 0)
    if chain:
        xr_ref[...] = jnp.zeros(xr_ref.shape, f32)
        xi_ref[...] = jnp.zeros(xi_ref.shape, f32)
        lax.fori_loop(0, seg_len // unroll, functools.partial(steps, False), 0)
        apr = apr_ref[0]
        api = api_ref[0]

        def scan(c, carry):
            kr, ki = carry
            cr_ref[pl.ds(c, 1), :] = kr
            ci_ref[pl.ds(c, 1), :] = ki
            er = xr_ref[pl.ds(c, 1), :]
            ei = xi_ref[pl.ds(c, 1), :]
            return apr * kr - api * ki + er, apr * ki + api * kr + ei

        kr, ki = lax.fori_loop(0, n_seg, scan, (x0r_ref[0], x0i_ref[0]))
        sr_ref[0] = kr
        si_ref[0] = ki
        xr_ref[...] = cr_ref[...]
        xi_ref[...] = ci_ref[...]
        lax.fori_loop(0, seg_len // unroll, functools.partial(steps, True), 0)
    else:
        xr_ref[...] = x0r_ref[0]
        xi_ref[...] = x0i_ref[0]
        lax.fori_loop(0, seg_len // unroll, functools.partial(steps, True), 0)
        sr_ref[0] = xr_ref[...]
        si_ref[0] = xi_ref[...]
    lax.fori_loop(0, n_seg, stage_out, 0)


def _s5(u, x0r, x0i, bblk, cblk, ar, ai, apr, api, dblk, n_seg, chain, name):
    bq, l, _ = u.shape
    seg_len = l // n_seg
    r0 = x0r.shape[1]
    per_k = lambda a: pl.BlockSpec((1,) + a.shape[1:], lambda b, k: (k, 0, 0))
    st = pl.BlockSpec((1, r0, STATE_BLOCK), lambda b, k: (b, 0, k))
    tok = lambda **kw: pl.BlockSpec((1, l, LANES), lambda b, k: (b, 0, k), **kw)
    sd = jax.ShapeDtypeStruct
    stage = pltpu.VMEM((n_seg * _s5_pitch(seg_len), LANES), f32)
    return pl.pallas_call(
        functools.partial(_s5_kernel, chain, n_seg, seg_len),
        grid=(bq, N_SSM_BLOCKS),
        in_specs=[tok(pipeline_mode=pl.Buffered(1)), per_k(bblk), per_k(cblk), per_k(ar), per_k(ai), per_k(apr),
                  per_k(api), per_k(dblk), st, st],
        out_specs=[tok(), st, st],
        out_shape=[sd((bq, l, D_SSM), f32), sd(x0r.shape, f32), sd(x0i.shape, f32)],
        scratch_shapes=[stage, stage] + [pltpu.VMEM((n_seg, STATE_BLOCK), f32)] * 4,
        compiler_params=_cparams("parallel", "parallel"),
        name=name,
    )(u, bblk, cblk, ar, ai, apr, api, dblk, x0r, x0i)


def _out_kernel(x_ref, att_ref, ga_ref, ys_ref, gs_ref, wg_ref, bg_ref, wo_ref, y_ref):
    ga = ga_ref[0].astype(f32)
    a = att_ref[0].astype(f32) * (ga * jax.nn.sigmoid(ga))
    ys = ys_ref[0]
    z = 0.5 * ys * (1.0 + jnp.tanh(0.7978845608028654 * (ys + 0.044715 * (ys * ys * ys))))
    gate = jax.nn.sigmoid(_dot(z.astype(bf16), wg_ref[...]) + bg_ref[...])
    gs = gs_ref[0].astype(f32)
    s5 = z * gate * (gs * jax.nn.sigmoid(gs))
    cat = jnp.concatenate([a, s5], axis=1).astype(bf16)
    y_ref[0] = x_ref[0] + _dot(cat, wo_ref[...])


def _outproj(x, att, ga, ys, gs, wg, bg, wo, tile, name):
    bx, lx, d = x.shape
    full = lambda a: pl.BlockSpec(a.shape, lambda b, j: (0,) * a.ndim)
    row = lambda w: pl.BlockSpec((1, tile, w), lambda b, j: (b, j, 0))
    return pl.pallas_call(
        _out_kernel,
        grid=(bx, lx // tile),
        in_specs=[row(d), row(D_ATT), row(D_ATT), row(D_SSM), row(D_SSM), full(wg), full(bg), full(wo)],
        out_specs=row(d),
        out_shape=jax.ShapeDtypeStruct(x.shape, f32),
        compiler_params=_cparams("parallel", "parallel"),
        name=name,
    )(x, att, ga, ys, gs, wg, bg, wo)


def _lower_tri(n):
    r = jnp.arange(n)
    return (r[None, :] <= r[:, None]).astype(bf16)


def _block_diag(blocks):
    k, g8, r, c = blocks.shape
    eye = jnp.eye(g8, dtype=blocks.dtype)
    return jnp.einsum("kgrc,gh->kgrhc", blocks, eye).reshape(k, g8 * r, g8 * c)


def kernel(x_prompt, x_sample, cache_k, cache_v, cache_logf, state_s5_re, state_s5_im, meta_tokens, norm_g,
           w_in, b_f, q_norm_g, k_norm_g, s5_a_re, s5_a_im, s5_log_dt, s5_b_re, s5_b_im, s5_c_re, s5_c_im,
           s5_d, w_glu, b_glu, w_out):
    assert norm_g.shape[0] == 1, "single-layer step"
    bp, lp, d = x_prompt.shape
    bs, ts, _ = x_sample.shape
    n_meta = meta_tokens.shape[0]
    past = cache_k.shape[2]
    G, P = N_GROUPS, SSM_STATE

    w = w_in[0]
    o3 = 3 * D_ATT
    o4 = o3 + N_HEADS
    wm = jnp.concatenate([w[:, :o3], w[:, o4:]], axis=1).astype(bf16)
    wf = jnp.repeat(w[:, o3:o4], AUG, axis=1).astype(bf16)
    bfr = jnp.repeat(b_f[0], AUG)[None, :].astype(f32)
    g_in = norm_g[0][None, :].astype(f32)
    qg = (jnp.tile(q_norm_g[0], N_HEADS) * (HEAD_DIM ** -0.5 * LOG2E))[None, :].astype(f32)
    kg = jnp.tile(k_norm_g[0], N_HEADS)[None, :].astype(f32)
    hid = jnp.arange(D_ATT) // HEAD_DIM
    eh = ((hid[:, None] == hid[None, :]).astype(f32) / HEAD_DIM).astype(bf16)
    sel = (jnp.arange(LANES)[None, :] == AUG * jnp.arange(N_HEADS)[:, None]).astype(bf16)
    m_bound = 8.0 * jnp.max(jnp.abs(q_norm_g[0])) * jnp.max(jnp.abs(k_norm_g[0])) * 1.02
    negm = jnp.full((1, LANES), -LOG2E, f32) * m_bound
    params = (g_in, wm, wf, bfr, qg, kg, eh, negm, sel)

    tm = n_meta
    meta = _inproj(meta_tokens[None].astype(f32), jnp.zeros((1, 1, LANES), f32), params, tm, _lower_tri(tm), True,
                   "inproj_meta")
    (_, kb_m, vb_m, kf_m, vf_m, lft_m, edge_m, _, kcat_m, _, u_m, _) = meta
    cum_meta_end = edge_m[:, -1, 1:2, :]
    c0 = jnp.broadcast_to(cum_meta_end, (bp, 1, LANES))
    tile_p = min(ROW_TILE, lp)
    tb_p = min(LANES, tile_p)
    (q_p, kb_p, vb_p, kf_p, vf_p, lft_p, edge_p, qcat_p, kcat_p, ga_p, u_p, gs_p) = _inproj(
        x_prompt, c0, params, tile_p, _lower_tri(tb_p), True, "inproj_prompt")
    ls = bs * ts
    tile_s = min(ROW_TILE, ls)
    (q_s, kb_s, vb_s, kf_s, vf_s, lft_s, _, _, _, ga_s, u_s, gs_s) = _inproj(
        x_sample.reshape(1, ls, d), jnp.zeros((1, 1, LANES), f32), params, tile_s,
        _lower_tri(min(LANES, tile_s)), False, "inproj_sample")

    tile_a = min(ATT_TILE, lp)
    nq = lp // tile_a
    per_tile = tile_a // tb_p
    edges = edge_p[:, :, :2, ::AUG]
    q_start = edges[:, 0::per_tile, 0, :]
    k_end = edges[:, per_tile - 1::per_tile, 1, :]
    dmax = q_start[:, :, None, :] - k_end[:, None, :, :]
    jmin = jnp.sum((dmax < -EXP_ZERO).astype(jnp.int32), axis=2)
    jmin = jnp.minimum(jmin, jnp.arange(nq, dtype=jnp.int32)[None, :, None])
    jmin = jnp.transpose(jmin, (0, 2, 1)).reshape(-1)
    mflag = (q_start - cum_meta_end[:, :, ::AUG] >= -EXP_ZERO).astype(jnp.int32)
    mflag = jnp.transpose(mflag, (0, 2, 1)).reshape(-1)
    padm = lambda a: jnp.pad(a[0], ((0, LANES - tm), (0, 0)))
    att_p = _prompt_attention(q_p, qcat_p, kb_p, vb_p, kcat_p, padm(kb_m), padm(vb_m), padm(kcat_m),
                              jmin, mflag, n_meta)

    lft_new = jnp.transpose(lft_s.reshape(N_HEADS, bs, ts), (1, 0, 2))
    s_len = -(-(past + ts) // LANES) * LANES
    lft = jnp.concatenate([jnp.transpose(cache_logf[0].astype(f32), (0, 2, 1)), lft_new,
                           jnp.zeros((bs, N_HEADS, s_len - past - ts), f32)], axis=2)
    att_s = _sample_attention(q_s.reshape(bs, ts, D_ATT), kb_s.reshape(bs, ts, D_ATT), vb_s.reshape(bs, ts, D_ATT),
                              cache_k[0].reshape(bs, past, D_ATT), cache_v[0].reshape(bs, past, D_ATT), lft)

    n_seg_p = min(PROMPT_SEGMENTS, lp)
    seg_len_p = lp // n_seg_p
    n_sq = seg_len_p.bit_length() - 1
    assert seg_len_p == 1 << n_sq
    bt_re = jnp.transpose(s5_b_re[0], (0, 2, 1)).astype(f32)
    bt_im = jnp.transpose(s5_b_im[0], (0, 2, 1)).astype(f32)
    abr, abi, apr, api, bbr, bbi = _s5_prep(s5_a_re[0].astype(f32), s5_a_im[0].astype(f32),
                                            s5_log_dt[0][:, None].astype(f32), bt_re, bt_im, n_sq)
    K8 = (N_SSM_BLOCKS, GROUPS_PER_BLOCK)
    bblk = jnp.concatenate([_block_diag(bbr.reshape(K8 + (SSM_GROUP, P))),
                            _block_diag(bbi.reshape(K8 + (SSM_GROUP, P)))], axis=2).astype(bf16)
    ct_re = jnp.transpose(s5_c_re[0], (0, 2, 1)).astype(f32)
    ct_im = jnp.transpose(s5_c_im[0], (0, 2, 1)).astype(f32)
    cblk = jnp.concatenate([_block_diag(ct_re.reshape(K8 + (P, SSM_GROUP))),
                            _block_diag(-ct_im.reshape(K8 + (P, SSM_GROUP)))], axis=1).astype(bf16)
    lane_k = lambda a: a.reshape(N_SSM_BLOCKS, 1, -1)
    ar, ai, aprk, apik = lane_k(abr), lane_k(abi), lane_k(apr), lane_k(api)
    dblk = lane_k(s5_d[0].astype(f32))

    assert tm == ts
    n_small = -(-(bs + 1) // SUBLANES) * SUBLANES
    u_small = jnp.concatenate([u_s, u_m, jnp.zeros((1, (n_small - bs - 1) * ts, D_SSM), f32)], axis=1)
    pad_state = lambda a: jnp.pad(a[0].reshape(bs, G * P).astype(f32), ((0, n_small - bs), (0, 0)))[None]
    ys_small, sr_small, si_small = _s5(u_small, pad_state(state_s5_re), pad_state(state_s5_im), bblk, cblk,
                                       ar, ai, ar, ai, dblk, n_small, False, "s5_short")
    x0r = jnp.broadcast_to(sr_small[:, bs:bs + 1, :], (bp, 1, G * P))
    x0i = jnp.broadcast_to(si_small[:, bs:bs + 1, :], (bp, 1, G * P))
    ys_p, sr_p, si_p = _s5(u_p, x0r, x0i, bblk, cblk, ar, ai, aprk, apik, dblk, n_seg_p, True, "s5_prompt")

    wg = w_glu[0].astype(bf16)
    bg = b_glu[0][None, :].astype(f32)
    wo = w_out[0].astype(bf16)
    y_prompt = _outproj(x_prompt, att_p, ga_p, ys_p, gs_p, wg, bg, wo, tile_p, "outproj_prompt")
    y_sample = _outproj(x_sample.reshape(1, ls, d), att_s.reshape(1, ls, D_ATT), ga_s, ys_small[:, :ls], gs_s,
                        wg, bg, wo, tile_s, "outproj_sample").reshape(bs, ts, d)

    def heads_with_meta(m, x):
        m = jnp.broadcast_to(m.reshape(1, n_meta, N_HEADS, HEAD_DIM), (bp, n_meta, N_HEADS, HEAD_DIM))
        x = jnp.pad(x.reshape(bp, lp, N_HEADS, HEAD_DIM), ((0, 0), (n_meta, 0), (0, 0), (0, 0)))
        return lax.dynamic_update_slice(x, m, (0, 0, 0, 0))[None]

    new_k_prompt = heads_with_meta(kf_m, kf_p)
    new_v_prompt = heads_with_meta(vf_m, vf_p)
    lft_all = jnp.concatenate([jnp.broadcast_to(lft_m, (bp, N_HEADS, n_meta)), lft_p], axis=2)
    new_logf_prompt = jnp.transpose(lft_all, (0, 2, 1))[None]
    new_s5_re_prompt = sr_p.reshape(1, bp, G, P)
    new_s5_im_prompt = si_p.reshape(1, bp, G, P)
    new_k_sample = kf_s.reshape(1, bs, ts, N_HEADS, HEAD_DIM)
    new_v_sample = vf_s.reshape(1, bs, ts, N_HEADS, HEAD_DIM)
    new_logf_sample = jnp.transpose(lft_new, (0, 2, 1))[None]
    new_s5_re_sample = sr_small[:, :bs].reshape(1, bs, G, P)
    new_s5_im_sample = si_small[:, :bs].reshape(1, bs, G, P)
    return (y_prompt, y_sample, new_k_prompt, new_v_prompt, new_logf_prompt, new_s5_re_prompt, new_s5_im_prompt,
            new_k_sample, new_v_sample, new_logf_sample, new_s5_re_sample, new_s5_im_sample)
```

```python
import functools

import jax
import jax.numpy as jnp
from jax import lax
from jax.experimental import pallas as pl
from jax.experimental.pallas import tpu as pltpu

N_HEADS = 8
HEAD_DIM = 64
D_ATT = N_HEADS * HEAD_DIM
SSM_GROUP = 16
SSM_STATE = 64
D_SSM = 512
N_GROUPS = D_SSM // SSM_GROUP
EPS = 1e-6
NEG_INF = -1e30

LANES = 128
SUBLANES = 8
AUG = LANES // N_HEADS
GROUPS_PER_BLOCK = LANES // SSM_GROUP
STATE_BLOCK = GROUPS_PER_BLOCK * SSM_STATE
N_SSM_BLOCKS = D_SSM // LANES
VMEM_LIMIT = 56 * 1024 * 1024
EXP_ZERO = 105.0

ROW_TILE = 512
ATT_TILE = 512
ATT_GROUP = 8
ATT_UNROLL = 4
LOG2E = 1.4426950408889634
PROMPT_SEGMENTS = 512
S5_Q = 8

bf16 = jnp.bfloat16
f32 = jnp.float32


def _cparams(*sem):
    return pltpu.CompilerParams(dimension_semantics=sem, vmem_limit_bytes=VMEM_LIMIT)


def _split3(x):
    hi = x.astype(bf16).astype(f32)
    r1 = x - hi
    mid = r1.astype(bf16).astype(f32)
    lo = (r1 - mid).astype(bf16).astype(f32)
    return hi, mid, lo


def _dot(a, b):
    return jnp.dot(a, b, preferred_element_type=f32)


def _dot_nt(a, b):
    return lax.dot_general(a, b, (((1,), (1,)), ((), ())), preferred_element_type=f32)


def _s5_tokens_shape(b, l):
    return (b, N_SSM_BLOCKS, S5_Q, l // S5_Q, LANES)


def _s5_tokens_spec(rows, index_map, **kw):
    return pl.BlockSpec((1, N_SSM_BLOCKS, S5_Q, rows // S5_Q, LANES), index_map, **kw)


def _inproj_kernel(chain, n_blk, x_ref, c0_ref, g_ref, wm_ref, wf_ref, bf_ref, qg_ref, kg_ref, eh_ref,
                   tri_ref, negm_ref, sel_ref, q_ref, kb_ref, vb_ref, kf_ref, vf_ref, lft_ref, edge_ref,
                   qcat_ref, kcat_ref, ga_ref, u_ref, gs_ref, carry_ref, us_ref):
    x = x_ref[0]
    ms = jnp.mean(x * x, axis=-1, keepdims=True)
    xn = (x * lax.rsqrt(ms + EPS) * g_ref[...]).astype(bf16)
    z = _dot(xn, wm_ref[...])
    q = z[:, 0 * D_ATT:1 * D_ATT]
    k = z[:, 1 * D_ATT:2 * D_ATT]
    v = z[:, 2 * D_ATT:3 * D_ATT]
    eh = eh_ref[...]
    qn = q * lax.rsqrt(_dot((q * q).astype(bf16), eh) + EPS) * qg_ref[...]
    kn = k * lax.rsqrt(_dot((k * k).astype(bf16), eh) + EPS) * kg_ref[...]
    q_ref[0] = qn.astype(bf16)
    kf_ref[0] = kn
    kb_ref[0] = kn.astype(bf16)
    vf_ref[0] = v
    vb_ref[0] = v.astype(bf16)
    ga_ref[0] = z[:, 3 * D_ATT:4 * D_ATT].astype(bf16)
    gs_ref[0] = z[:, 4 * D_ATT + D_SSM:].astype(bf16)
    for kb in range(N_SSM_BLOCKS):
        us_ref[kb] = z[:, 4 * D_ATT + kb * LANES:4 * D_ATT + (kb + 1) * LANES]
    for kb in range(N_SSM_BLOCKS):
        for qq in range(S5_Q):
            u_ref[0, kb, qq] = us_ref[kb, pl.ds(qq, x.shape[0] // S5_Q, stride=S5_Q), :]

    zf = _dot(xn, wf_ref[...]) + bf_ref[...]
    lf = jnp.minimum(zf, 0.0) - jnp.log1p(jnp.exp(-jnp.abs(zf)))
    sel = sel_ref[...]
    lft_ref[0] = sum(_dot_nt(sel, part.astype(bf16)) for part in _split3(lf))

    if chain:
        @pl.when(pl.program_id(1) == 0)
        def _():
            carry_ref[...] = c0_ref[0]
        carry = carry_ref[...]
    else:
        carry = jnp.zeros((1, LANES), f32)
    tri = tri_ref[...]
    tb = x.shape[0] // n_blk
    cums = []
    for r in range(n_blk):
        hi, mid, lo = _split3(lf[r * tb:(r + 1) * tb])
        c = _dot(tri, hi.astype(bf16)) + _dot(tri, mid.astype(bf16)) + _dot(tri, lo.astype(bf16)) + carry
        cums.append(c)
        carry = c[tb - 1:tb, :]
        edge_ref[0, r] = jnp.concatenate([c[0:1, :], carry, jnp.zeros((SUBLANES - 2, LANES), f32)], axis=0)
    cum = cums[0] if n_blk == 1 else jnp.concatenate(cums, axis=0)
    if chain:
        carry_ref[...] = carry

    slot = lax.broadcasted_iota(jnp.int32, (1, LANES), 1) % AUG
    hi, mid, lo = _split3(cum * LOG2E)
    qc = jnp.where(slot == 0, hi, jnp.where(slot == 1, mid, jnp.where(slot == 2, lo,
                   jnp.where(slot <= 6, 1.0, 0.0))))
    kc = jnp.where(slot <= 2, 1.0, jnp.where(slot == 3, -hi, jnp.where(slot == 4, -mid,
                   jnp.where(slot == 5, -lo, jnp.where(slot == 6, negm_ref[...], 0.0)))))
    qcat_ref[0] = qc.astype(bf16)
    kcat_ref[0] = kc.astype(bf16)


def _inproj(x, c0, params, tile, tri, chain, name):
    bx, lx, d = x.shape
    n_tiles = lx // tile
    n_blk = tile // tri.shape[0]
    full = lambda a: pl.BlockSpec(a.shape, lambda b, j: (0,) * a.ndim)
    row = lambda w: pl.BlockSpec((1, tile, w), lambda b, j: (b, j, 0))
    sd = jax.ShapeDtypeStruct
    rows = lambda w, t: (row(w), sd((bx, lx, w), t))
    outs = [rows(D_ATT, bf16), rows(D_ATT, bf16), rows(D_ATT, bf16), rows(D_ATT, f32), rows(D_ATT, f32),
            (pl.BlockSpec((1, N_HEADS, tile), lambda b, j: (b, 0, j)), sd((bx, N_HEADS, lx), f32)),
            (pl.BlockSpec((1, n_blk, SUBLANES, LANES), lambda b, j: (b, j, 0, 0)),
             sd((bx, n_tiles * n_blk, SUBLANES, LANES), f32)),
            rows(LANES, bf16), rows(LANES, bf16), rows(D_ATT, bf16),
            (_s5_tokens_spec(tile, lambda b, j: (b, 0, 0, j, 0)), sd(_s5_tokens_shape(bx, lx), f32)),
            rows(D_SSM, bf16)]
    g_in, wm, wf, bfr, qg, kg, eh, negm, sel = params
    operands = (g_in, wm, wf, bfr, qg, kg, eh, tri, negm, sel)
    return pl.pallas_call(
        functools.partial(_inproj_kernel, chain, n_blk),
        grid=(bx, n_tiles),
        in_specs=[row(d), pl.BlockSpec((1, 1, LANES), lambda b, j: (b, 0, 0))] + [full(a) for a in operands],
        out_specs=[o[0] for o in outs],
        out_shape=[o[1] for o in outs],
        scratch_shapes=[pltpu.VMEM((1, LANES), f32), pltpu.VMEM((N_SSM_BLOCKS, tile, LANES), f32)],
        compiler_params=_cparams("parallel", "arbitrary"),
        name=name,
    )(x, c0, *operands)


def _attn_kernel(tile, nq, gq, n_meta, jmin_ref, mflag_ref, q_ref, qc_ref, kb_ref, vb_ref, kc_ref,
                 kbm_ref, vbm_ref, kcm_ref, o_ref, acc_ref, qop_ref, bnd_ref):
    b = pl.program_id(0)
    p = pl.program_id(1)
    g = pl.program_id(2)
    lane = lax.broadcasted_iota(jnp.int32, (1, LANES), 1)
    mcol = lax.broadcasted_iota(jnp.int32, (tile, LANES), 1)
    col = lax.broadcasted_iota(jnp.int32, (tile, tile), 1)
    row = lax.broadcasted_iota(jnp.int32, (tile, tile), 0)
    @pl.when(g == 0)
    def _():
        bnd_ref[0] = jnp.where(col <= row, -NEG_INF, NEG_INF)
        bnd_ref[1] = jnp.full((tile, tile), -NEG_INF, f32)
        bnd_ref[2] = jnp.full((tile, tile), NEG_INF, f32)

    def rows_of(j):
        return pl.ds(pl.multiple_of(j * tile, tile), tile)

    heads = range(2)
    base = [(b * N_HEADS + 2 * p + hh) * nq + g * gq for hh in heads]
    m_q = [(lane // HEAD_DIM == hh).astype(bf16) for hh in heads]
    m_o = [(lane // HEAD_DIM != hh).astype(bf16) for hh in heads]
    m_c = [(lane // AUG == 2 * p + hh).astype(bf16) for hh in heads]

    def stage_q(r, carry):
        for hh in heads:
            qop_ref[hh, r] = jnp.concatenate([q_ref[0, rows_of(r), :] * m_q[hh], qc_ref[0, rows_of(r), :] * m_c[hh]],
                                             axis=1)
            acc_ref[hh, r] = jnp.zeros(acc_ref.shape[2:], f32)
        return carry

    lax.fori_loop(0, gq, stage_q, 0)

    def scores(hh, rc, ks, kc):
        return _dot_nt(qop_ref[hh, rc], jnp.concatenate([ks, kc], axis=1))

    def pv(hh, pe, vs):
        return _dot(pe, vs * m_q[hh] + m_o[hh])

    def item_probs(hh, r, j):
        rc = jnp.minimum(r, gq - 1)
        kind = jnp.where(r < gq, jnp.where(j == g * gq + rc, 0, 1), 2)
        s = scores(hh, rc, kb_ref[0, rows_of(j), :], kc_ref[0, rows_of(j), :])
        return jnp.exp2(jnp.minimum(s, bnd_ref[kind])).astype(bf16), rc

    def item_pv(hh, pe, rc, j):
        acc_ref[hh, rc] += pv(hh, pe, vb_ref[0, rows_of(j), :])

    def advance(hh, r, j):
        last = jnp.logical_and(r < gq, j == g * gq + r)
        r2 = jnp.where(last, r + 1, r)
        j2 = jnp.where(last, jmin_ref[base[hh] + jnp.minimum(r + 1, gq - 1)], jnp.where(r < gq, j + 1, j))
        return r2.astype(jnp.int32), j2.astype(jnp.int32)

    def first_item(hh):
        j0 = jmin_ref[base[hh]]
        pe0, rc0 = item_probs(hh, jnp.int32(0), j0)
        return (pe0, rc0, j0) + advance(hh, jnp.int32(0), j0)

    def pipeline(hh, carry):
        def body(t, carry):
            pe, rp, jp, r, j = carry
            for _ in range(ATT_UNROLL):
                pn, rn = item_probs(hh, r, j)
                item_pv(hh, pe, rp, jp)
                pe, rp, jp = pn, rn, j
                r, j = advance(hh, r, j)
            return pe, rp, jp, r, j

        n_items = sum(g * gq + r - jmin_ref[base[hh] + r] + 1 for r in range(gq))
        trips = (n_items + ATT_UNROLL - 2) // ATT_UNROLL
        return lax.fori_loop(0, trips, body, carry)[:3]

    tail0 = pipeline(0, first_item(0))
    head1 = first_item(1)
    item_pv(0, *tail0)
    item_pv(1, *pipeline(1, head1))

    def meta_tile(r, carry):
        for hh in heads:
            @pl.when(mflag_ref[base[hh] + r] > 0)
            def _(hh=hh):
                s = jnp.where(mcol < n_meta, scores(hh, r, kbm_ref[...], kcm_ref[...]), NEG_INF)
                acc_ref[hh, r] += pv(hh, jnp.exp2(s).astype(bf16), vbm_ref[...])
        return carry

    lax.fori_loop(0, gq, meta_tile, 0)

    for r in range(gq):
        outs = []
        for hh in heads:
            a = acc_ref[hh, r]
            outs.append(a / pltpu.roll(a, HEAD_DIM, axis=1))
        o_ref[0, r * tile:(r + 1) * tile, :] = jnp.where(lane < HEAD_DIM, outs[0], outs[1]).astype(bf16)


def _prompt_attention(q, qcat, kb, vb, kcat, kbm, vbm, kcm, jmin, mflag, n_meta):
    b, l, _ = q.shape
    tile = min(ATT_TILE, l)
    nq = l // tile
    gq = min(ATT_GROUP, nq)
    rows = gq * tile
    grid_spec = pltpu.PrefetchScalarGridSpec(
        num_scalar_prefetch=2,
        grid=(b, N_HEADS // 2, nq // gq),
        in_specs=[
            pl.BlockSpec((1, rows, LANES), lambda b, p, i, *_: (b, i, p)),
            pl.BlockSpec((1, rows, LANES), lambda b, p, i, *_: (b, i, 0)),
            pl.BlockSpec((1, l, LANES), lambda b, p, i, *_: (b, 0, p)),
            pl.BlockSpec((1, l, LANES), lambda b, p, i, *_: (b, 0, p)),
            pl.BlockSpec((1, l, LANES), lambda b, p, i, *_: (b, 0, 0)),
            pl.BlockSpec((LANES, LANES), lambda b, p, i, *_: (0, p)),
            pl.BlockSpec((LANES, LANES), lambda b, p, i, *_: (0, p)),
            pl.BlockSpec((LANES, LANES), lambda b, p, i, *_: (0, 0)),
        ],
        out_specs=pl.BlockSpec((1, rows, LANES), lambda b, p, i, *_: (b, i, p)),
        scratch_shapes=[pltpu.VMEM((2, gq, tile, LANES), f32), pltpu.VMEM((2, gq, tile, 2 * LANES), bf16),
                        pltpu.VMEM((3, tile, tile), f32)],
    )
    return pl.pallas_call(
        functools.partial(_attn_kernel, tile, nq, gq, n_meta),
        grid_spec=grid_spec,
        out_shape=jax.ShapeDtypeStruct((b, l, D_ATT), bf16),
        compiler_params=_cparams("parallel", "parallel", "arbitrary"),
        name="prompt_attention",
    )(jmin, mflag, q, qcat, kb, vb, kcat, kbm, vbm, kcm)


def _sattn_kernel(past, tn, q_ref, kn_ref, vn_ref, ckt_ref, cvt_ref, lft_ref, o_ref):
    s_len = lft_ref.shape[-1]
    rows = N_HEADS * tn
    c = lft_ref[0]
    lane_h = lax.broadcasted_iota(jnp.int32, c.shape, 1)
    sh = 1
    while sh < s_len:
        c = c + jnp.where(lane_h >= sh, pltpu.roll(c, sh, axis=1), 0.0)
        sh *= 2
    cexp = jnp.concatenate([jnp.broadcast_to(c[h:h + 1, :], (tn, s_len)) for h in range(N_HEADS)], axis=0)
    rix = lax.broadcasted_iota(jnp.int32, (rows, s_len), 0)
    lix = lax.broadcasted_iota(jnp.int32, (rows, s_len), 1)
    qpos = past + rix % tn
    cq = jnp.sum(jnp.where(lix == qpos, cexp, 0.0), axis=1, keepdims=True)
    pad = jnp.zeros((s_len - past - tn, D_ATT), bf16)
    k_new = jnp.concatenate([kn_ref[0], pad], axis=0)
    v_new = jnp.concatenate([vn_ref[0], pad], axis=0)
    qrep = jnp.concatenate([q_ref[0].astype(f32)] * N_HEADS, axis=0)
    r5 = lax.broadcasted_iota(jnp.int32, (rows, D_ATT), 0)
    l5 = lax.broadcasted_iota(jnp.int32, (rows, D_ATT), 1)
    qrows = jnp.where(l5 // HEAD_DIM == r5 // tn, qrep, 0.0).astype(bf16)
    s = jnp.concatenate([_dot(qrows, ckt_ref[0].astype(bf16)), _dot_nt(qrows, k_new)], axis=1)
    s = s + (cq - cexp) * LOG2E
    s = jnp.where(lix <= qpos, s, NEG_INF)
    m = jnp.max(s, axis=1, keepdims=True)
    pe = jnp.exp2(s - m)
    den = jnp.sum(pe, axis=1, keepdims=True)
    pb = pe.astype(bf16)
    o = (_dot_nt(pb[:, :past], cvt_ref[0].astype(bf16)) + _dot(pb[:, past:], v_new)) / den
    lo = lax.broadcasted_iota(jnp.int32, (tn, D_ATT), 1)
    out = jnp.zeros((tn, D_ATT), f32)
    for h in range(N_HEADS):
        out = out + jnp.where(lo // HEAD_DIM == h, o[h * tn:(h + 1) * tn, :], 0.0)
    o_ref[0] = out.astype(bf16)


def _sample_attention(q, kn, vn, ckt, cvt, lft):
    bs, tn, _ = q.shape
    past = ckt.shape[2]
    s_len = lft.shape[-1]
    blk = lambda n, w: pl.BlockSpec((1, n, w), lambda b: (b, 0, 0))
    return pl.pallas_call(
        functools.partial(_sattn_kernel, past, tn),
        grid=(bs,),
        in_specs=[blk(tn, D_ATT), blk(tn, D_ATT), blk(tn, D_ATT), blk(D_ATT, past), blk(D_ATT, past),
                  blk(N_HEADS, s_len)],
        out_specs=blk(tn, D_ATT),
        out_shape=jax.ShapeDtypeStruct((bs, tn, D_ATT), bf16),
        compiler_params=_cparams("parallel"),
        name="sample_attention",
    )(q, kn, vn, ckt, cvt, lft)


def _s5prep_kernel(n_sq, are_ref, aim_ref, ldt_ref, btr_ref, bti_ref, cre_ref, cim_ref,
                   aqr_ref, aqi_ref, apr_ref, api_ref, wr_ref, wi_ref, cr_ref, ci_ref, kt_ref):
    are = are_ref[...]
    aim = aim_ref[...]
    dt = jnp.exp(ldt_ref[...])
    mag = jnp.exp(are * dt)
    ph = aim * dt
    abr = mag * jnp.cos(ph)
    abi = mag * jnp.sin(ph)
    nr = abr - 1.0
    den = are * are + aim * aim
    fr = (nr * are + abi * aim) / den
    fi = (abi * are - nr * aim) / den
    pw = [(jnp.ones_like(abr), jnp.zeros_like(abr))]
    for _ in range(S5_Q):
        pr, pi = pw[-1]
        pw.append((pr * abr - pi * abi, pr * abi + pi * abr))
    aqr_ref[...], aqi_ref[...] = pw[S5_Q]
    pr, pi = pw[S5_Q]
    for _ in range(n_sq):
        pr, pi = pr * pr - pi * pi, 2.0 * pr * pi
    apr_ref[...] = pr
    api_ref[...] = pi
    for g in range(are.shape[0]):
        row = slice(g, g + 1)
        btr = btr_ref[g]
        bti = bti_ref[g]
        bbr = fr[row] * btr - fi[row] * bti
        bbi = fr[row] * bti + fi[row] * btr
        cre = cre_ref[g]
        cim = cim_ref[g]
        car_all, cai_all = [cre], [cim]
        for q in range(S5_Q):
            er, ei = pw[S5_Q - 1 - q]
            wr_ref[q, g] = er[row] * bbr - ei[row] * bbi
            wi_ref[q, g] = er[row] * bbi + ei[row] * bbr
            gr, gi = pw[q + 1]
            car = cre * gr[row] - cim * gi[row]
            cai = cre * gi[row] + cim * gr[row]
            cr_ref[q, g] = car
            ci_ref[q, g] = cai
            car_all.append(car)
            cai_all.append(cai)
        car_all = jnp.concatenate(car_all[:S5_Q], axis=0)
        cai_all = jnp.concatenate(cai_all[:S5_Q], axis=0)
        hi = functools.partial(lax.dot_general, dimension_numbers=(((1,), (1,)), ((), ())),
                               precision=lax.Precision.HIGHEST, preferred_element_type=f32)
        kt_ref[g] = hi(car_all, bbr) - hi(cai_all, bbi)


def _s5_prep(a_re, a_im, log_dt, bt_re, bt_im, c_re, c_im, n_sq):
    g, p = a_re.shape
    sd = jax.ShapeDtypeStruct
    per_q = sd((S5_Q,) + bt_re.shape, f32)
    return pl.pallas_call(
        functools.partial(_s5prep_kernel, n_sq),
        out_shape=[sd((g, p), f32)] * 4 + [per_q] * 4 + [sd((g, S5_Q * SSM_GROUP, SSM_GROUP), f32)],
        name="s5_prep",
    )(a_re, a_im, log_dt, bt_re, bt_im, c_re, c_im)


def _s5_kernel(chain, n_seg, n_sb, u_ref, w8_ref, toep_ref, cpow_ref, aqr_ref, aqi_ref, apr_ref, api_ref, d8_ref,
               x0r_ref, x0i_ref, y_ref, sr_ref, si_ref, xr_ref, xi_ref, cr_ref, ci_ref):
    aqr = aqr_ref[0]
    aqi = aqi_ref[0]
    w8 = w8_ref[0]
    toep = toep_ref[0]
    cpow = cpow_ref[0]
    d8 = d8_ref[0]

    def step(with_y, sb):
        rows = pl.ds(sb, n_seg, stride=n_sb) if n_sb > 1 else pl.ds(0, n_seg)
        uf = jnp.concatenate([u_ref[0, 0, q, rows, :] for q in range(S5_Q)], axis=1)
        ub = uf.astype(bf16)
        xr = xr_ref[...]
        xi = xi_ref[...]
        if with_y:
            xc = jnp.concatenate([xr, xi], axis=1).astype(bf16)
            y = _dot(ub, toep) + _dot(xc, cpow) + d8 * uf
            for q in range(S5_Q):
                y_ref[0, 0, q, rows, :] = y[:, q * LANES:(q + 1) * LANES]
        pj = _dot(ub, w8)
        xr_ref[...] = aqr * xr - aqi * xi + pj[:, :STATE_BLOCK]
        xi_ref[...] = aqr * xi + aqi * xr + pj[:, STATE_BLOCK:]

    if chain:
        xr_ref[...] = jnp.zeros(xr_ref.shape, f32)
        xi_ref[...] = jnp.zeros(xi_ref.shape, f32)
        for sb in range(n_sb):
            step(False, sb)
        apr = apr_ref[0]
        api = api_ref[0]

        def scan(c, carry):
            kr, ki = carry
            cr_ref[pl.ds(c, 1), :] = kr
            ci_ref[pl.ds(c, 1), :] = ki
            er = xr_ref[pl.ds(c, 1), :]
            ei = xi_ref[pl.ds(c, 1), :]
            return apr * kr - api * ki + er, apr * ki + api * kr + ei

        kr, ki = lax.fori_loop(0, n_seg, scan, (x0r_ref[0], x0i_ref[0]))
        sr_ref[0] = kr
        si_ref[0] = ki
        xr_ref[...] = cr_ref[...]
        xi_ref[...] = ci_ref[...]
        for sb in range(n_sb):
            step(True, sb)
    else:
        xr_ref[...] = x0r_ref[0]
        xi_ref[...] = x0i_ref[0]
        for sb in range(n_sb):
            step(True, sb)
        sr_ref[0] = xr_ref[...]
        si_ref[0] = xi_ref[...]


def _s5(u, x0r, x0i, weights, n_seg, chain, name):
    bq, _, _, m, _ = u.shape
    n_sb = m // n_seg
    r0 = x0r.shape[1]
    per_k = lambda a: pl.BlockSpec((1,) + a.shape[1:], lambda b, k: (k, 0, 0))
    st = pl.BlockSpec((1, r0, STATE_BLOCK), lambda b, k: (b, 0, k))
    tok = lambda **kw: pl.BlockSpec((1, 1, S5_Q, m, LANES), lambda b, k: (b, k, 0, 0, 0), **kw)
    sd = jax.ShapeDtypeStruct
    return pl.pallas_call(
        functools.partial(_s5_kernel, chain, n_seg, n_sb),
        grid=(bq, N_SSM_BLOCKS),
        in_specs=[tok(pipeline_mode=pl.Buffered(1))] + [per_k(a) for a in weights] + [st, st],
        out_specs=[tok(), st, st],
        out_shape=[sd(u.shape, f32), sd(x0r.shape, f32), sd(x0i.shape, f32)],
        scratch_shapes=[pltpu.VMEM((n_seg, STATE_BLOCK), f32)] * 4,
        compiler_params=_cparams("parallel", "parallel"),
        name=name,
    )(u, *weights, x0r, x0i)


def _out_kernel(x_ref, att_ref, ga_ref, ys_ref, gs_ref, wg_ref, bg_ref, wo_ref, y_ref, ys_scr):
    ga = ga_ref[0].astype(f32)
    a = att_ref[0].astype(f32) * (ga * jax.nn.sigmoid(ga))
    rows = x_ref.shape[1]
    for kb in range(N_SSM_BLOCKS):
        for qq in range(S5_Q):
            ys_scr[kb, pl.ds(qq, rows // S5_Q, stride=S5_Q), :] = ys_ref[0, kb, qq]
    ys = jnp.concatenate([ys_scr[kb] for kb in range(N_SSM_BLOCKS)], axis=1)
    z = 0.5 * ys * (1.0 + jnp.tanh(0.7978845608028654 * (ys + 0.044715 * (ys * ys * ys))))
    gate = jax.nn.sigmoid(_dot(z.astype(bf16), wg_ref[...]) + bg_ref[...])
    gs = gs_ref[0].astype(f32)
    s5 = z * gate * (gs * jax.nn.sigmoid(gs))
    cat = jnp.concatenate([a, s5], axis=1).astype(bf16)
    y_ref[0] = x_ref[0] + _dot(cat, wo_ref[...])


def _outproj(x, att, ga, ys, gs, wg, bg, wo, tile, name):
    bx, lx, d = x.shape
    full = lambda a: pl.BlockSpec(a.shape, lambda b, j: (0,) * a.ndim)
    row = lambda w: pl.BlockSpec((1, tile, w), lambda b, j: (b, j, 0))
    return pl.pallas_call(
        _out_kernel,
        grid=(bx, lx // tile),
        in_specs=[row(d), row(D_ATT), row(D_ATT), _s5_tokens_spec(tile, lambda b, j: (b, 0, 0, j, 0)), row(D_SSM),
                  full(wg), full(bg), full(wo)],
        out_specs=row(d),
        out_shape=jax.ShapeDtypeStruct(x.shape, f32),
        scratch_shapes=[pltpu.VMEM((N_SSM_BLOCKS, tile, LANES), f32)],
        compiler_params=_cparams("parallel", "parallel"),
        name=name,
    )(x, att, ga, ys, gs, wg, bg, wo)


def _lower_tri(n):
    r = jnp.arange(n)
    return (r[None, :] <= r[:, None]).astype(bf16)


def _block_diag(blocks):
    k, g8, r, c = blocks.shape
    eye = jnp.eye(g8, dtype=blocks.dtype)
    return jnp.einsum("kgrc,gh->kgrhc", blocks, eye).reshape(k, g8 * r, g8 * c)


def kernel(x_prompt, x_sample, cache_k, cache_v, cache_logf, state_s5_re, state_s5_im, meta_tokens, norm_g,
           w_in, b_f, q_norm_g, k_norm_g, s5_a_re, s5_a_im, s5_log_dt, s5_b_re, s5_b_im, s5_c_re, s5_c_im,
           s5_d, w_glu, b_glu, w_out):
    assert norm_g.shape[0] == 1, "single-layer step"
    bp, lp, d = x_prompt.shape
    bs, ts, _ = x_sample.shape
    n_meta = meta_tokens.shape[0]
    past = cache_k.shape[2]
    G, P = N_GROUPS, SSM_STATE

    w = w_in[0]
    o3 = 3 * D_ATT
    o4 = o3 + N_HEADS
    wm = jnp.concatenate([w[:, :o3], w[:, o4:]], axis=1).astype(bf16)
    wf = jnp.repeat(w[:, o3:o4], AUG, axis=1).astype(bf16)
    bfr = jnp.repeat(b_f[0], AUG)[None, :].astype(f32)
    g_in = norm_g[0][None, :].astype(f32)
    qg = (jnp.tile(q_norm_g[0], N_HEADS) * (HEAD_DIM ** -0.5 * LOG2E))[None, :].astype(f32)
    kg = jnp.tile(k_norm_g[0], N_HEADS)[None, :].astype(f32)
    hid = jnp.arange(D_ATT) // HEAD_DIM
    eh = ((hid[:, None] == hid[None, :]).astype(f32) / HEAD_DIM).astype(bf16)
    sel = (jnp.arange(LANES)[None, :] == AUG * jnp.arange(N_HEADS)[:, None]).astype(bf16)
    m_bound = 8.0 * jnp.max(jnp.abs(q_norm_g[0])) * jnp.max(jnp.abs(k_norm_g[0])) * 1.02
    negm = jnp.full((1, LANES), -LOG2E, f32) * m_bound
    params = (g_in, wm, wf, bfr, qg, kg, eh, negm, sel)

    tm = n_meta
    meta = _inproj(meta_tokens[None].astype(f32), jnp.zeros((1, 1, LANES), f32), params, tm, _lower_tri(tm), True,
                   "inproj_meta")
    (_, kb_m, vb_m, kf_m, vf_m, lft_m, edge_m, _, kcat_m, _, u_m, _) = meta
    cum_meta_end = edge_m[:, -1, 1:2, :]
    c0 = jnp.broadcast_to(cum_meta_end, (bp, 1, LANES))
    tile_p = min(ROW_TILE, lp)
    tb_p = min(LANES, tile_p)
    (q_p, kb_p, vb_p, kf_p, vf_p, lft_p, edge_p, qcat_p, kcat_p, ga_p, u_p, gs_p) = _inproj(
        x_prompt, c0, params, tile_p, _lower_tri(tb_p), True, "inproj_prompt")
    ls = bs * ts
    tile_s = min(ROW_TILE, ls)
    (q_s, kb_s, vb_s, kf_s, vf_s, lft_s, _, _, _, ga_s, u_s, gs_s) = _inproj(
        x_sample.reshape(1, ls, d), jnp.zeros((1, 1, LANES), f32), params, tile_s,
        _lower_tri(min(LANES, tile_s)), False, "inproj_sample")

    tile_a = min(ATT_TILE, lp)
    nq = lp // tile_a
    per_tile = tile_a // tb_p
    edges = edge_p[:, :, :2, ::AUG]
    q_start = edges[:, 0::per_tile, 0, :]
    k_end = edges[:, per_tile - 1::per_tile, 1, :]
    dmax = q_start[:, :, None, :] - k_end[:, None, :, :]
    jmin = jnp.sum((dmax < -EXP_ZERO).astype(jnp.int32), axis=2)
    jmin = jnp.minimum(jmin, jnp.arange(nq, dtype=jnp.int32)[None, :, None])
    jmin = jnp.transpose(jmin, (0, 2, 1)).reshape(-1)
    mflag = (q_start - cum_meta_end[:, :, ::AUG] >= -EXP_ZERO).astype(jnp.int32)
    mflag = jnp.transpose(mflag, (0, 2, 1)).reshape(-1)
    padm = lambda a: jnp.pad(a[0], ((0, LANES - tm), (0, 0)))
    att_p = _prompt_attention(q_p, qcat_p, kb_p, vb_p, kcat_p, padm(kb_m), padm(vb_m), padm(kcat_m),
                              jmin, mflag, n_meta)

    lft_new = jnp.transpose(lft_s.reshape(N_HEADS, bs, ts), (1, 0, 2))
    s_len = -(-(past + ts) // LANES) * LANES
    lft = jnp.concatenate([jnp.transpose(cache_logf[0].astype(f32), (0, 2, 1)), lft_new,
                           jnp.zeros((bs, N_HEADS, s_len - past - ts), f32)], axis=2)
    cache_t = lambda c: jnp.transpose(c[0], (0, 2, 3, 1)).reshape(bs, D_ATT, past)
    att_s = _sample_attention(q_s.reshape(bs, ts, D_ATT), kb_s.reshape(bs, ts, D_ATT), vb_s.reshape(bs, ts, D_ATT),
                              cache_t(cache_k), cache_t(cache_v), lft)

    n_seg_p = min(PROMPT_SEGMENTS, lp // S5_Q)
    n_sb_p = lp // (S5_Q * n_seg_p)
    n_sq = n_sb_p.bit_length() - 1
    assert n_sb_p == 1 << n_sq and ts % S5_Q == 0
    tr = lambda a: jnp.transpose(a[0], (0, 2, 1)).astype(f32)
    aqr, aqi, apr, api, wr, wi, cqr, cqi, kt = _s5_prep(
        s5_a_re[0].astype(f32), s5_a_im[0].astype(f32), s5_log_dt[0][:, None].astype(f32),
        tr(s5_b_re), tr(s5_b_im), s5_c_re[0].astype(f32), s5_c_im[0].astype(f32), n_sq)
    NB, G8, Q = N_SSM_BLOCKS, GROUPS_PER_BLOCK, S5_Q
    eye = jnp.eye(G8, dtype=f32)
    split_g = lambda a: a.reshape(Q, NB, G8, SSM_GROUP, P)
    w_part = lambda a: jnp.einsum("qkghp,gG->kqghGp", split_g(a), eye).reshape(NB, Q * LANES, STATE_BLOCK)
    w8 = jnp.concatenate([w_part(wr), w_part(wi)], axis=2).astype(bf16)
    c_part = lambda a: jnp.einsum("qkghp,gG->kgpqGh", split_g(a), eye).reshape(NB, STATE_BLOCK, Q * LANES)
    cpow = jnp.concatenate([c_part(cqr), c_part(-cqi)], axis=1).astype(bf16)
    lag = jnp.arange(Q)[None, :] - jnp.arange(Q)[:, None]
    ktq = kt.reshape(NB, G8, Q, SSM_GROUP, SSM_GROUP)[:, :, jnp.clip(lag, 0, Q - 1)]
    ktq = jnp.where((lag >= 0)[None, None, :, :, None, None], ktq, 0.0)
    toep = jnp.einsum("kgiohj,gG->kigjoGh", ktq, eye).reshape(NB, Q * LANES, Q * LANES).astype(bf16)
    lane_k = lambda a: a.reshape(NB, 1, -1)
    d8 = jnp.tile(lane_k(s5_d[0].astype(f32)), (1, 1, Q))
    weights = lambda pr_, pi_: (w8, toep, cpow, lane_k(aqr), lane_k(aqi), lane_k(pr_), lane_k(pi_), d8)

    assert tm == ts
    n_small = -(-(bs + 1) // SUBLANES) * SUBLANES
    u_small = jnp.concatenate([u_s, u_m, jnp.zeros(_s5_tokens_shape(1, (n_small - bs - 1) * ts), f32)], axis=3)
    pad_state = lambda a: jnp.pad(a[0].reshape(bs, G * P).astype(f32), ((0, n_small - bs), (0, 0)))[None]
    ys_small, sr_small, si_small = _s5(u_small, pad_state(state_s5_re), pad_state(state_s5_im),
                                       weights(aqr, aqi), n_small, False, "s5_short")
    x0r = jnp.broadcast_to(sr_small[:, bs:bs + 1, :], (bp, 1, G * P))
    x0i = jnp.broadcast_to(si_small[:, bs:bs + 1, :], (bp, 1, G * P))
    ys_p, sr_p, si_p = _s5(u_p, x0r, x0i, weights(apr, api), n_seg_p, True, "s5_prompt")

    wg = w_glu[0].astype(bf16)
    bg = b_glu[0][None, :].astype(f32)
    wo = w_out[0].astype(bf16)
    y_prompt = _outproj(x_prompt, att_p, ga_p, ys_p, gs_p, wg, bg, wo, tile_p, "outproj_prompt")
    y_sample = _outproj(x_sample.reshape(1, ls, d), att_s.reshape(1, ls, D_ATT), ga_s, ys_small[:, :, :, :ls // S5_Q], gs_s,
                        wg, bg, wo, tile_s, "outproj_sample").reshape(bs, ts, d)

    def heads_with_meta(m, x):
        m = jnp.broadcast_to(m.reshape(1, n_meta, N_HEADS, HEAD_DIM), (bp, n_meta, N_HEADS, HEAD_DIM))
        x = jnp.pad(x.reshape(bp, lp, N_HEADS, HEAD_DIM), ((0, 0), (n_meta, 0), (0, 0), (0, 0)))
        return lax.dynamic_update_slice(x, m, (0, 0, 0, 0))[None]

    new_k_prompt = heads_with_meta(kf_m, kf_p)
    new_v_prompt = heads_with_meta(vf_m, vf_p)
    lft_all = jnp.concatenate([jnp.broadcast_to(lft_m, (bp, N_HEADS, n_meta)), lft_p], axis=2)
    new_logf_prompt = jnp.transpose(lft_all, (0, 2, 1))[None]
    new_s5_re_prompt = sr_p.reshape(1, bp, G, P)
    new_s5_im_prompt = si_p.reshape(1, bp, G, P)
    new_k_sample = kf_s.reshape(1, bs, ts, N_HEADS, HEAD_DIM)
    new_v_sample = vf_s.reshape(1, bs, ts, N_HEADS, HEAD_DIM)
    new_logf_sample = jnp.transpose(lft_new, (0, 2, 1))[None]
    new_s5_re_sample = sr_small[:, :bs].reshape(1, bs, G, P)
    new_s5_im_sample = si_small[:, :bs].reshape(1, bs, G, P)
    return (y_prompt, y_sample, new_k_prompt, new_v_prompt, new_logf_prompt, new_s5_re_prompt, new_s5_im_prompt,
            new_k_sample, new_v_sample, new_logf_sample, new_s5_re_sample, new_s5_im_sample)
```

```python
import functools

import jax
import jax.numpy as jnp
from jax import lax
from jax.experimental import pallas as pl
from jax.experimental.pallas import tpu as pltpu

N_HEADS = 8
HEAD_DIM = 64
D_ATT = N_HEADS * HEAD_DIM
SSM_GROUP = 16
SSM_STATE = 64
D_SSM = 512
N_GROUPS = D_SSM // SSM_GROUP
EPS = 1e-6
NEG_INF = -1e30

LANES = 128
SUBLANES = 8
AUG = LANES // N_HEADS
GROUPS_PER_BLOCK = LANES // SSM_GROUP
STATE_BLOCK = GROUPS_PER_BLOCK * SSM_STATE
N_SSM_BLOCKS = D_SSM // LANES
VMEM_LIMIT = 56 * 1024 * 1024
EXP_ZERO = 105.0

ROW_TILE = 512
ATT_TILE = 512
ATT_GROUP = 8
ATT_UNROLL = 4
LOG2E = 1.4426950408889634
PROMPT_SEGMENTS = 512
S5_Q = 8

bf16 = jnp.bfloat16
f32 = jnp.float32


def _cparams(*sem):
    return pltpu.CompilerParams(dimension_semantics=sem, vmem_limit_bytes=VMEM_LIMIT)


def _split3(x):
    hi = x.astype(bf16).astype(f32)
    r1 = x - hi
    mid = r1.astype(bf16).astype(f32)
    lo = (r1 - mid).astype(bf16).astype(f32)
    return hi, mid, lo


def _dot(a, b):
    return jnp.dot(a, b, preferred_element_type=f32)


def _dot_nt(a, b):
    return lax.dot_general(a, b, (((1,), (1,)), ((), ())), preferred_element_type=f32)


def _s5_tokens_shape(b, l):
    return (b, N_SSM_BLOCKS, S5_Q, l // S5_Q, LANES)


def _s5_tokens_spec(rows, index_map, **kw):
    return pl.BlockSpec((1, N_SSM_BLOCKS, S5_Q, rows // S5_Q, LANES), index_map, **kw)


def _inproj_kernel(chain, n_blk, x_ref, c0_ref, g_ref, wm_ref, wf_ref, bf_ref, qg_ref, kg_ref, eh_ref,
                   tri_ref, negm_ref, sel_ref, q_ref, kb_ref, vb_ref, kf_ref, vf_ref, lft_ref, edge_ref,
                   qcat_ref, kcat_ref, ga_ref, u_ref, gs_ref, carry_ref, us_ref):
    x = x_ref[0]
    ms = jnp.mean(x * x, axis=-1, keepdims=True)
    xn = (x * lax.rsqrt(ms + EPS) * g_ref[...]).astype(bf16)
    z = _dot(xn, wm_ref[...])
    q = z[:, 0 * D_ATT:1 * D_ATT]
    k = z[:, 1 * D_ATT:2 * D_ATT]
    v = z[:, 2 * D_ATT:3 * D_ATT]
    eh = eh_ref[...]
    qn = q * lax.rsqrt(_dot((q * q).astype(bf16), eh) + EPS) * qg_ref[...]
    kn = k * lax.rsqrt(_dot((k * k).astype(bf16), eh) + EPS) * kg_ref[...]
    q_ref[0] = qn.astype(bf16)
    kf_ref[0] = kn
    kb_ref[0] = kn.astype(bf16)
    vf_ref[0] = v
    vb_ref[0] = v.astype(bf16)
    ga_ref[0] = z[:, 3 * D_ATT:4 * D_ATT].astype(bf16)
    gs_ref[0] = z[:, 4 * D_ATT + D_SSM:].astype(bf16)
    for kb in range(N_SSM_BLOCKS):
        us_ref[kb] = z[:, 4 * D_ATT + kb * LANES:4 * D_ATT + (kb + 1) * LANES]
    for kb in range(N_SSM_BLOCKS):
        for qq in range(S5_Q):
            u_ref[0, kb, qq] = us_ref[kb, pl.ds(qq, x.shape[0] // S5_Q, stride=S5_Q), :]

    zf = _dot(xn, wf_ref[...]) + bf_ref[...]
    lf = jnp.minimum(zf, 0.0) - jnp.log1p(jnp.exp(-jnp.abs(zf)))
    sel = sel_ref[...]
    lft_ref[0] = sum(_dot_nt(sel, part.astype(bf16)) for part in _split3(lf))

    if chain:
        @pl.when(pl.program_id(1) == 0)
        def _():
            carry_ref[...] = c0_ref[0]
        carry = carry_ref[...]
    else:
        carry = jnp.zeros((1, LANES), f32)
    tri = tri_ref[...]
    tb = x.shape[0] // n_blk
    cums = []
    for r in range(n_blk):
        hi, mid, lo = _split3(lf[r * tb:(r + 1) * tb])
        c = _dot(tri, hi.astype(bf16)) + _dot(tri, mid.astype(bf16)) + _dot(tri, lo.astype(bf16)) + carry
        cums.append(c)
        carry = c[tb - 1:tb, :]
        edge_ref[0, r] = jnp.concatenate([c[0:1, :], carry, jnp.zeros((SUBLANES - 2, LANES), f32)], axis=0)
    cum = cums[0] if n_blk == 1 else jnp.concatenate(cums, axis=0)
    if chain:
        carry_ref[...] = carry

    slot = lax.broadcasted_iota(jnp.int32, (1, LANES), 1) % AUG
    hi, mid, lo = _split3(cum * LOG2E)
    qc = jnp.where(slot == 0, hi, jnp.where(slot == 1, mid, jnp.where(slot == 2, lo,
                   jnp.where(slot <= 6, 1.0, 0.0))))
    kc = jnp.where(slot <= 2, 1.0, jnp.where(slot == 3, -hi, jnp.where(slot == 4, -mid,
                   jnp.where(slot == 5, -lo, jnp.where(slot == 6, negm_ref[...], 0.0)))))
    qcat_ref[0] = qc.astype(bf16)
    kcat_ref[0] = kc.astype(bf16)


def _inproj(x, c0, params, tile, tri, chain, name):
    bx, lx, d = x.shape
    n_tiles = lx // tile
    n_blk = tile // tri.shape[0]
    full = lambda a: pl.BlockSpec(a.shape, lambda b, j: (0,) * a.ndim)
    row = lambda w: pl.BlockSpec((1, tile, w), lambda b, j: (b, j, 0))
    sd = jax.ShapeDtypeStruct
    rows = lambda w, t: (row(w), sd((bx, lx, w), t))
    outs = [rows(D_ATT, bf16), rows(D_ATT, bf16), rows(D_ATT, bf16), rows(D_ATT, f32), rows(D_ATT, f32),
            (pl.BlockSpec((1, N_HEADS, tile), lambda b, j: (b, 0, j)), sd((bx, N_HEADS, lx), f32)),
            (pl.BlockSpec((1, n_blk, SUBLANES, LANES), lambda b, j: (b, j, 0, 0)),
             sd((bx, n_tiles * n_blk, SUBLANES, LANES), f32)),
            rows(LANES, bf16), rows(LANES, bf16), rows(D_ATT, bf16),
            (_s5_tokens_spec(tile, lambda b, j: (b, 0, 0, j, 0)), sd(_s5_tokens_shape(bx, lx), f32)),
            rows(D_SSM, bf16)]
    g_in, wm, wf, bfr, qg, kg, eh, negm, sel = params
    operands = (g_in, wm, wf, bfr, qg, kg, eh, tri, negm, sel)
    return pl.pallas_call(
        functools.partial(_inproj_kernel, chain, n_blk),
        grid=(bx, n_tiles),
        in_specs=[row(d), pl.BlockSpec((1, 1, LANES), lambda b, j: (b, 0, 0))] + [full(a) for a in operands],
        out_specs=[o[0] for o in outs],
        out_shape=[o[1] for o in outs],
        scratch_shapes=[pltpu.VMEM((1, LANES), f32), pltpu.VMEM((N_SSM_BLOCKS, tile, LANES), f32)],
        compiler_params=_cparams("parallel", "arbitrary"),
        name=name,
    )(x, c0, *operands)


def _attn_kernel(tile, nq, gq, n_meta, jmin_ref, mflag_ref, q_ref, qc_ref, kb_ref, vb_ref, kc_ref,
                 kbm_ref, vbm_ref, kcm_ref, o_ref, acc_ref, qop_ref, bnd_ref):
    b = pl.program_id(0)
    p = pl.program_id(1)
    g = pl.program_id(2)
    lane = lax.broadcasted_iota(jnp.int32, (1, LANES), 1)
    mcol = lax.broadcasted_iota(jnp.int32, (tile, LANES), 1)
    col = lax.broadcasted_iota(jnp.int32, (tile, tile), 1)
    row = lax.broadcasted_iota(jnp.int32, (tile, tile), 0)
    @pl.when(g == 0)
    def _():
        bnd_ref[0] = jnp.where(col <= row, -NEG_INF, NEG_INF)
        bnd_ref[1] = jnp.full((tile, tile), -NEG_INF, f32)
        bnd_ref[2] = jnp.full((tile, tile), NEG_INF, f32)

    def rows_of(j):
        return pl.ds(pl.multiple_of(j * tile, tile), tile)

    heads = range(2)
    base = [(b * N_HEADS + 2 * p + hh) * nq + g * gq for hh in heads]
    m_q = [(lane // HEAD_DIM == hh).astype(bf16) for hh in heads]
    m_o = [(lane // HEAD_DIM != hh).astype(bf16) for hh in heads]
    m_c = [(lane // AUG == 2 * p + hh).astype(bf16) for hh in heads]

    def stage_q(r, carry):
        for hh in heads:
            qop_ref[hh, r] = jnp.concatenate([q_ref[0, rows_of(r), :] * m_q[hh], qc_ref[0, rows_of(r), :] * m_c[hh]],
                                             axis=1)
            acc_ref[hh, r] = jnp.zeros(acc_ref.shape[2:], f32)
        return carry

    lax.fori_loop(0, gq, stage_q, 0)

    def scores(hh, rc, ks, kc):
        return _dot_nt(qop_ref[hh, rc], jnp.concatenate([ks, kc], axis=1))

    def pv(hh, pe, vs):
        return _dot(pe, vs * m_q[hh] + m_o[hh])

    def item_probs(hh, r, j):
        rc = jnp.minimum(r, gq - 1)
        kind = jnp.where(r < gq, jnp.where(j == g * gq + rc, 0, 1), 2)
        s = scores(hh, rc, kb_ref[0, rows_of(j), :], kc_ref[0, rows_of(j), :])
        return jnp.exp2(jnp.minimum(s, bnd_ref[kind])).astype(bf16), rc

    def item_pv(hh, pe, rc, j):
        acc_ref[hh, rc] += pv(hh, pe, vb_ref[0, rows_of(j), :])

    def advance(hh, r, j):
        last = jnp.logical_and(r < gq, j == g * gq + r)
        r2 = jnp.where(last, r + 1, r)
        j2 = jnp.where(last, jmin_ref[base[hh] + jnp.minimum(r + 1, gq - 1)], jnp.where(r < gq, j + 1, j))
        return r2.astype(jnp.int32), j2.astype(jnp.int32)

    def first_item(hh):
        j0 = jmin_ref[base[hh]]
        pe0, rc0 = item_probs(hh, jnp.int32(0), j0)
        return (pe0, rc0, j0) + advance(hh, jnp.int32(0), j0)

    def pipeline(hh, carry):
        def body(t, carry):
            pe, rp, jp, r, j = carry
            for _ in range(ATT_UNROLL):
                pn, rn = item_probs(hh, r, j)
                item_pv(hh, pe, rp, jp)
                pe, rp, jp = pn, rn, j
                r, j = advance(hh, r, j)
            return pe, rp, jp, r, j

        n_items = sum(g * gq + r - jmin_ref[base[hh] + r] + 1 for r in range(gq))
        trips = (n_items + ATT_UNROLL - 2) // ATT_UNROLL
        return lax.fori_loop(0, trips, body, carry)[:3]

    tail0 = pipeline(0, first_item(0))
    head1 = first_item(1)
    item_pv(0, *tail0)
    item_pv(1, *pipeline(1, head1))

    def meta_tile(r, carry):
        for hh in heads:
            @pl.when(mflag_ref[base[hh] + r] > 0)
            def _(hh=hh):
                s = jnp.where(mcol < n_meta, scores(hh, r, kbm_ref[...], kcm_ref[...]), NEG_INF)
                acc_ref[hh, r] += pv(hh, jnp.exp2(s).astype(bf16), vbm_ref[...])
        return carry

    lax.fori_loop(0, gq, meta_tile, 0)

    for r in range(gq):
        outs = []
        for hh in heads:
            a = acc_ref[hh, r]
            outs.append(a / pltpu.roll(a, HEAD_DIM, axis=1))
        o_ref[0, r * tile:(r + 1) * tile, :] = jnp.where(lane < HEAD_DIM, outs[0], outs[1]).astype(bf16)


def _prompt_attention(q, qcat, kb, vb, kcat, kbm, vbm, kcm, jmin, mflag, n_meta):
    b, l, _ = q.shape
    tile = min(ATT_TILE, l)
    nq = l // tile
    gq = min(ATT_GROUP, nq)
    rows = gq * tile
    grid_spec = pltpu.PrefetchScalarGridSpec(
        num_scalar_prefetch=2,
        grid=(b, N_HEADS // 2, nq // gq),
        in_specs=[
            pl.BlockSpec((1, rows, LANES), lambda b, p, i, *_: (b, i, p)),
            pl.BlockSpec((1, rows, LANES), lambda b, p, i, *_: (b, i, 0)),
            pl.BlockSpec((1, l, LANES), lambda b, p, i, *_: (b, 0, p)),
            pl.BlockSpec((1, l, LANES), lambda b, p, i, *_: (b, 0, p)),
            pl.BlockSpec((1, l, LANES), lambda b, p, i, *_: (b, 0, 0)),
            pl.BlockSpec((LANES, LANES), lambda b, p, i, *_: (0, p)),
            pl.BlockSpec((LANES, LANES), lambda b, p, i, *_: (0, p)),
            pl.BlockSpec((LANES, LANES), lambda b, p, i, *_: (0, 0)),
        ],
        out_specs=pl.BlockSpec((1, rows, LANES), lambda b, p, i, *_: (b, i, p)),
        scratch_shapes=[pltpu.VMEM((2, gq, tile, LANES), f32), pltpu.VMEM((2, gq, tile, 2 * LANES), bf16),
                        pltpu.VMEM((3, tile, tile), f32)],
    )
    return pl.pallas_call(
        functools.partial(_attn_kernel, tile, nq, gq, n_meta),
        grid_spec=grid_spec,
        out_shape=jax.ShapeDtypeStruct((b, l, D_ATT), bf16),
        compiler_params=_cparams("parallel", "parallel", "arbitrary"),
        name="prompt_attention",
    )(jmin, mflag, q, qcat, kb, vb, kcat, kbm, vbm, kcm)


def _sattn_kernel(past, tn, q_ref, kn_ref, vn_ref, ckt_ref, cvt_ref, lft_ref, o_ref):
    s_len = lft_ref.shape[-1]
    rows = N_HEADS * tn
    c = lft_ref[0]
    lane_h = lax.broadcasted_iota(jnp.int32, c.shape, 1)
    sh = 1
    while sh < s_len:
        c = c + jnp.where(lane_h >= sh, pltpu.roll(c, sh, axis=1), 0.0)
        sh *= 2
    cexp = jnp.concatenate([jnp.broadcast_to(c[h:h + 1, :], (tn, s_len)) for h in range(N_HEADS)], axis=0)
    rix = lax.broadcasted_iota(jnp.int32, (rows, s_len), 0)
    lix = lax.broadcasted_iota(jnp.int32, (rows, s_len), 1)
    qpos = past + rix % tn
    cq = jnp.sum(jnp.where(lix == qpos, cexp, 0.0), axis=1, keepdims=True)
    pad = jnp.zeros((s_len - past - tn, D_ATT), bf16)
    k_new = jnp.concatenate([kn_ref[0], pad], axis=0)
    v_new = jnp.concatenate([vn_ref[0], pad], axis=0)
    qrep = jnp.concatenate([q_ref[0].astype(f32)] * N_HEADS, axis=0)
    r5 = lax.broadcasted_iota(jnp.int32, (rows, D_ATT), 0)
    l5 = lax.broadcasted_iota(jnp.int32, (rows, D_ATT), 1)
    qrows = jnp.where(l5 // HEAD_DIM == r5 // tn, qrep, 0.0).astype(bf16)
    s = jnp.concatenate([_dot(qrows, ckt_ref[0].astype(bf16)), _dot_nt(qrows, k_new)], axis=1)
    s = s + (cq - cexp) * LOG2E
    s = jnp.where(lix <= qpos, s, NEG_INF)
    m = jnp.max(s, axis=1, keepdims=True)
    pe = jnp.exp2(s - m)
    den = jnp.sum(pe, axis=1, keepdims=True)
    pb = pe.astype(bf16)
    o = (_dot_nt(pb[:, :past], cvt_ref[0].astype(bf16)) + _dot(pb[:, past:], v_new)) / den
    lo = lax.broadcasted_iota(jnp.int32, (tn, D_ATT), 1)
    out = jnp.zeros((tn, D_ATT), f32)
    for h in range(N_HEADS):
        out = out + jnp.where(lo // HEAD_DIM == h, o[h * tn:(h + 1) * tn, :], 0.0)
    o_ref[0] = out.astype(bf16)


def _sample_attention(q, kn, vn, ckt, cvt, lft):
    bs, tn, _ = q.shape
    past = ckt.shape[2]
    s_len = lft.shape[-1]
    blk = lambda n, w: pl.BlockSpec((1, n, w), lambda b: (b, 0, 0))
    return pl.pallas_call(
        functools.partial(_sattn_kernel, past, tn),
        grid=(bs,),
        in_specs=[blk(tn, D_ATT), blk(tn, D_ATT), blk(tn, D_ATT), blk(D_ATT, past), blk(D_ATT, past),
                  blk(N_HEADS, s_len)],
        out_specs=blk(tn, D_ATT),
        out_shape=jax.ShapeDtypeStruct((bs, tn, D_ATT), bf16),
        compiler_params=_cparams("parallel"),
        name="sample_attention",
    )(q, kn, vn, ckt, cvt, lft)


def _s5prep_kernel(n_sq, are_ref, aim_ref, ldt_ref, btr_ref, bti_ref, cre_ref, cim_ref,
                   aqr_ref, aqi_ref, apr_ref, api_ref, wr_ref, wi_ref, ctr_ref, cti_ref, kk_ref):
    are = are_ref[...]
    aim = aim_ref[...]
    dt = jnp.exp(ldt_ref[...])
    mag = jnp.exp(are * dt)
    ph = aim * dt
    abr = mag * jnp.cos(ph)
    abi = mag * jnp.sin(ph)
    nr = abr - 1.0
    den = are * are + aim * aim
    fr = (nr * are + abi * aim) / den
    fi = (abi * are - nr * aim) / den
    pw = [(jnp.ones_like(abr), jnp.zeros_like(abr))]
    for _ in range(S5_Q):
        pr, pi = pw[-1]
        pw.append((pr * abr - pi * abi, pr * abi + pi * abr))
    aqr_ref[...], aqi_ref[...] = pw[S5_Q]
    pr, pi = pw[S5_Q]
    for _ in range(n_sq):
        pr, pi = pr * pr - pi * pi, 2.0 * pr * pi
    apr_ref[...] = pr
    api_ref[...] = pi
    p_states = are.shape[1]
    ident = (lax.broadcasted_iota(jnp.int32, (p_states, p_states), 0)
             == lax.broadcasted_iota(jnp.int32, (p_states, p_states), 1)).astype(bf16)
    lane = lax.broadcasted_iota(jnp.int32, (SSM_GROUP, S5_Q * SSM_GROUP), 1)
    hi = functools.partial(lax.dot_general, dimension_numbers=(((1,), (1,)), ((), ())),
                           precision=lax.Precision.HIGHEST, preferred_element_type=f32)
    for g in range(are.shape[0]):
        row = slice(g, g + 1)
        btr = btr_ref[g]
        bti = bti_ref[g]
        bbr = fr[row] * btr - fi[row] * bti
        bbi = fr[row] * bti + fi[row] * btr
        cre = cre_ref[g]
        cim = cim_ref[g]
        car_all, cai_all = [cre], [cim]
        for q in range(S5_Q):
            er, ei = pw[S5_Q - 1 - q]
            wr_ref[q, g] = er[row] * bbr - ei[row] * bbi
            wi_ref[q, g] = er[row] * bbi + ei[row] * bbr
            gr, gi = pw[q + 1]
            car_all.append(cre * gr[row] - cim * gi[row])
            cai_all.append(cre * gi[row] + cim * gr[row])
        ctr_ref[g] = _dot_nt(ident, jnp.concatenate(car_all[1:], axis=0).astype(bf16))
        cti_ref[g] = _dot_nt(ident, jnp.concatenate(cai_all[1:], axis=0).astype(bf16))
        kt = (hi(bbr, jnp.concatenate(car_all[:S5_Q], axis=0))
              - hi(bbi, jnp.concatenate(cai_all[:S5_Q], axis=0)))
        kk_ref[g] = jnp.concatenate(
            [kt if qi == 0 else jnp.where(lane >= qi * SSM_GROUP, pltpu.roll(kt, qi * SSM_GROUP, axis=1), 0.0)
             for qi in range(S5_Q)], axis=0)


def _s5_prep(a_re, a_im, log_dt, bt_re, bt_im, c_re, c_im, n_sq):
    g, p = a_re.shape
    sd = jax.ShapeDtypeStruct
    qh = S5_Q * SSM_GROUP
    return pl.pallas_call(
        functools.partial(_s5prep_kernel, n_sq),
        out_shape=[sd((g, p), f32)] * 4 + [sd((S5_Q,) + bt_re.shape, f32)] * 2 + [sd((g, p, qh), f32)] * 2
        + [sd((g, qh, qh), f32)],
        name="s5_prep",
    )(a_re, a_im, log_dt, bt_re, bt_im, c_re, c_im)


def _s5_kernel(chain, n_seg, n_sb, u_ref, w8_ref, toep_ref, cpow_ref, aqr_ref, aqi_ref, apr_ref, api_ref, d8_ref,
               x0r_ref, x0i_ref, y_ref, sr_ref, si_ref, xr_ref, xi_ref, cr_ref, ci_ref):
    aqr = aqr_ref[0]
    aqi = aqi_ref[0]
    w8 = w8_ref[0]
    toep = toep_ref[0]
    cpow = cpow_ref[0]
    d8 = d8_ref[0]

    def step(with_y, sb):
        rows = pl.ds(sb, n_seg, stride=n_sb) if n_sb > 1 else pl.ds(0, n_seg)
        uf = jnp.concatenate([u_ref[0, 0, q, rows, :] for q in range(S5_Q)], axis=1)
        ub = uf.astype(bf16)
        xr = xr_ref[...]
        xi = xi_ref[...]
        if with_y:
            xc = jnp.concatenate([xr, xi], axis=1).astype(bf16)
            y = _dot(ub, toep) + _dot(xc, cpow) + d8 * uf
            for q in range(S5_Q):
                y_ref[0, 0, q, rows, :] = y[:, q * LANES:(q + 1) * LANES]
        pj = _dot(ub, w8)
        xr_ref[...] = aqr * xr - aqi * xi + pj[:, :STATE_BLOCK]
        xi_ref[...] = aqr * xi + aqi * xr + pj[:, STATE_BLOCK:]

    if chain:
        xr_ref[...] = jnp.zeros(xr_ref.shape, f32)
        xi_ref[...] = jnp.zeros(xi_ref.shape, f32)
        for sb in range(n_sb):
            step(False, sb)
        apr = apr_ref[0]
        api = api_ref[0]

        def scan(c, carry):
            kr, ki = carry
            cr_ref[pl.ds(c, 1), :] = kr
            ci_ref[pl.ds(c, 1), :] = ki
            er = xr_ref[pl.ds(c, 1), :]
            ei = xi_ref[pl.ds(c, 1), :]
            return apr * kr - api * ki + er, apr * ki + api * kr + ei

        kr, ki = lax.fori_loop(0, n_seg, scan, (x0r_ref[0], x0i_ref[0]))
        sr_ref[0] = kr
        si_ref[0] = ki
        xr_ref[...] = cr_ref[...]
        xi_ref[...] = ci_ref[...]
        for sb in range(n_sb):
            step(True, sb)
    else:
        xr_ref[...] = x0r_ref[0]
        xi_ref[...] = x0i_ref[0]
        for sb in range(n_sb):
            step(True, sb)
        sr_ref[0] = xr_ref[...]
        si_ref[0] = xi_ref[...]


def _s5(u, x0r, x0i, weights, n_seg, chain, name):
    bq, _, _, m, _ = u.shape
    n_sb = m // n_seg
    r0 = x0r.shape[1]
    per_k = lambda a: pl.BlockSpec((1,) + a.shape[1:], lambda b, k: (k, 0, 0))
    st = pl.BlockSpec((1, r0, STATE_BLOCK), lambda b, k: (b, 0, k))
    tok = lambda **kw: pl.BlockSpec((1, 1, S5_Q, m, LANES), lambda b, k: (b, k, 0, 0, 0), **kw)
    sd = jax.ShapeDtypeStruct
    return pl.pallas_call(
        functools.partial(_s5_kernel, chain, n_seg, n_sb),
        grid=(bq, N_SSM_BLOCKS),
        in_specs=[tok(pipeline_mode=pl.Buffered(1))] + [per_k(a) for a in weights] + [st, st],
        out_specs=[tok(), st, st],
        out_shape=[sd(u.shape, f32), sd(x0r.shape, f32), sd(x0i.shape, f32)],
        scratch_shapes=[pltpu.VMEM((n_seg, STATE_BLOCK), f32)] * 4,
        compiler_params=_cparams("parallel", "parallel"),
        name=name,
    )(u, *weights, x0r, x0i)


def _out_kernel(x_ref, att_ref, ga_ref, ys_ref, gs_ref, wg_ref, bg_ref, wo_ref, y_ref, ys_scr):
    ga = ga_ref[0].astype(f32)
    a = att_ref[0].astype(f32) * (ga * jax.nn.sigmoid(ga))
    rows = x_ref.shape[1]
    for kb in range(N_SSM_BLOCKS):
        for qq in range(S5_Q):
            ys_scr[kb, pl.ds(qq, rows // S5_Q, stride=S5_Q), :] = ys_ref[0, kb, qq]
    ys = jnp.concatenate([ys_scr[kb] for kb in range(N_SSM_BLOCKS)], axis=1)
    z = 0.5 * ys * (1.0 + jnp.tanh(0.7978845608028654 * (ys + 0.044715 * (ys * ys * ys))))
    gate = jax.nn.sigmoid(_dot(z.astype(bf16), wg_ref[...]) + bg_ref[...])
    gs = gs_ref[0].astype(f32)
    s5 = z * gate * (gs * jax.nn.sigmoid(gs))
    cat = jnp.concatenate([a, s5], axis=1).astype(bf16)
    y_ref[0] = x_ref[0] + _dot(cat, wo_ref[...])


def _outproj(x, att, ga, ys, gs, wg, bg, wo, tile, name):
    bx, lx, d = x.shape
    full = lambda a: pl.BlockSpec(a.shape, lambda b, j: (0,) * a.ndim)
    row = lambda w: pl.BlockSpec((1, tile, w), lambda b, j: (b, j, 0))
    return pl.pallas_call(
        _out_kernel,
        grid=(bx, lx // tile),
        in_specs=[row(d), row(D_ATT), row(D_ATT), _s5_tokens_spec(tile, lambda b, j: (b, 0, 0, j, 0)), row(D_SSM),
                  full(wg), full(bg), full(wo)],
        out_specs=row(d),
        out_shape=jax.ShapeDtypeStruct(x.shape, f32),
        scratch_shapes=[pltpu.VMEM((N_SSM_BLOCKS, tile, LANES), f32)],
        compiler_params=_cparams("parallel", "parallel"),
        name=name,
    )(x, att, ga, ys, gs, wg, bg, wo)


def _lower_tri(n):
    r = jnp.arange(n)
    return (r[None, :] <= r[:, None]).astype(bf16)


def kernel(x_prompt, x_sample, cache_k, cache_v, cache_logf, state_s5_re, state_s5_im, meta_tokens, norm_g,
           w_in, b_f, q_norm_g, k_norm_g, s5_a_re, s5_a_im, s5_log_dt, s5_b_re, s5_b_im, s5_c_re, s5_c_im,
           s5_d, w_glu, b_glu, w_out):
    assert norm_g.shape[0] == 1, "single-layer step"
    bp, lp, d = x_prompt.shape
    bs, ts, _ = x_sample.shape
    n_meta = meta_tokens.shape[0]
    past = cache_k.shape[2]
    G, P = N_GROUPS, SSM_STATE

    w = w_in[0]
    o3 = 3 * D_ATT
    o4 = o3 + N_HEADS
    wm = jnp.concatenate([w[:, :o3], w[:, o4:]], axis=1).astype(bf16)
    wf = jnp.repeat(w[:, o3:o4], AUG, axis=1).astype(bf16)
    bfr = jnp.repeat(b_f[0], AUG)[None, :].astype(f32)
    g_in = norm_g[0][None, :].astype(f32)
    qg = (jnp.tile(q_norm_g[0], N_HEADS) * (HEAD_DIM ** -0.5 * LOG2E))[None, :].astype(f32)
    kg = jnp.tile(k_norm_g[0], N_HEADS)[None, :].astype(f32)
    hid = jnp.arange(D_ATT) // HEAD_DIM
    eh = ((hid[:, None] == hid[None, :]).astype(f32) / HEAD_DIM).astype(bf16)
    sel = (jnp.arange(LANES)[None, :] == AUG * jnp.arange(N_HEADS)[:, None]).astype(bf16)
    m_bound = 8.0 * jnp.max(jnp.abs(q_norm_g[0])) * jnp.max(jnp.abs(k_norm_g[0])) * 1.02
    negm = jnp.full((1, LANES), -LOG2E, f32) * m_bound
    params = (g_in, wm, wf, bfr, qg, kg, eh, negm, sel)

    tm = n_meta
    meta = _inproj(meta_tokens[None].astype(f32), jnp.zeros((1, 1, LANES), f32), params, tm, _lower_tri(tm), True,
                   "inproj_meta")
    (_, kb_m, vb_m, kf_m, vf_m, lft_m, edge_m, _, kcat_m, _, u_m, _) = meta
    cum_meta_end = edge_m[:, -1, 1:2, :]
    c0 = jnp.broadcast_to(cum_meta_end, (bp, 1, LANES))
    tile_p = min(ROW_TILE, lp)
    tb_p = min(LANES, tile_p)
    (q_p, kb_p, vb_p, kf_p, vf_p, lft_p, edge_p, qcat_p, kcat_p, ga_p, u_p, gs_p) = _inproj(
        x_prompt, c0, params, tile_p, _lower_tri(tb_p), True, "inproj_prompt")
    ls = bs * ts
    tile_s = min(ROW_TILE, ls)
    (q_s, kb_s, vb_s, kf_s, vf_s, lft_s, _, _, _, ga_s, u_s, gs_s) = _inproj(
        x_sample.reshape(1, ls, d), jnp.zeros((1, 1, LANES), f32), params, tile_s,
        _lower_tri(min(LANES, tile_s)), False, "inproj_sample")

    tile_a = min(ATT_TILE, lp)
    nq = lp // tile_a
    per_tile = tile_a // tb_p
    edges = edge_p[:, :, :2, ::AUG]
    q_start = edges[:, 0::per_tile, 0, :]
    k_end = edges[:, per_tile - 1::per_tile, 1, :]
    dmax = q_start[:, :, None, :] - k_end[:, None, :, :]
    jmin = jnp.sum((dmax < -EXP_ZERO).astype(jnp.int32), axis=2)
    jmin = jnp.minimum(jmin, jnp.arange(nq, dtype=jnp.int32)[None, :, None])
    jmin = jnp.transpose(jmin, (0, 2, 1)).reshape(-1)
    mflag = (q_start - cum_meta_end[:, :, ::AUG] >= -EXP_ZERO).astype(jnp.int32)
    mflag = jnp.transpose(mflag, (0, 2, 1)).reshape(-1)
    padm = lambda a: jnp.pad(a[0], ((0, LANES - tm), (0, 0)))
    att_p = _prompt_attention(q_p, qcat_p, kb_p, vb_p, kcat_p, padm(kb_m), padm(vb_m), padm(kcat_m),
                              jmin, mflag, n_meta)

    lft_new = jnp.transpose(lft_s.reshape(N_HEADS, bs, ts), (1, 0, 2))
    s_len = -(-(past + ts) // LANES) * LANES
    lft = jnp.concatenate([jnp.transpose(cache_logf[0].astype(f32), (0, 2, 1)), lft_new,
                           jnp.zeros((bs, N_HEADS, s_len - past - ts), f32)], axis=2)
    cache_t = lambda c: jnp.transpose(c[0], (0, 2, 3, 1)).reshape(bs, D_ATT, past)
    att_s = _sample_attention(q_s.reshape(bs, ts, D_ATT), kb_s.reshape(bs, ts, D_ATT), vb_s.reshape(bs, ts, D_ATT),
                              cache_t(cache_k), cache_t(cache_v), lft)

    n_seg_p = min(PROMPT_SEGMENTS, lp // S5_Q)
    n_sb_p = lp // (S5_Q * n_seg_p)
    n_sq = n_sb_p.bit_length() - 1
    assert n_sb_p == 1 << n_sq and ts % S5_Q == 0
    tr = lambda a: jnp.transpose(a[0], (0, 2, 1)).astype(f32)
    aqr, aqi, apr, api, wr, wi, ctr, cti, kk = _s5_prep(
        s5_a_re[0].astype(f32), s5_a_im[0].astype(f32), s5_log_dt[0][:, None].astype(f32),
        tr(s5_b_re), tr(s5_b_im), s5_c_re[0].astype(f32), s5_c_im[0].astype(f32), n_sq)
    NB, G8, Q = N_SSM_BLOCKS, GROUPS_PER_BLOCK, S5_Q
    qgh = Q * LANES

    def spread(x, col_of_src, row_group, col_group):
        n_src = x.shape[-1]
        n_rows = x.shape[1]
        n_cols = col_of_src.shape[0]
        sel = (col_of_src[None, :] == jnp.arange(n_src)[:, None]).astype(bf16)
        keep = row_group(jnp.arange(n_rows))[:, None] == col_group(jnp.arange(n_cols))[None, :]
        y = jnp.einsum("krs,sc->krc", x.astype(bf16), sel, preferred_element_type=f32)
        return jnp.where(keep[None], y, 0.0).astype(bf16)

    cols_state = jnp.arange(STATE_BLOCK) % P
    cols_qgh = (jnp.arange(qgh) // LANES) * SSM_GROUP + jnp.arange(qgh) % SSM_GROUP
    group_of_qgh = lambda i: (i // SSM_GROUP) % G8
    by_block = lambda a: jnp.transpose(a.reshape(Q, NB, G8 * SSM_GROUP, P), (1, 0, 2, 3)).reshape(NB, qgh, P)
    w_part = lambda a: spread(by_block(a), cols_state, group_of_qgh, lambda i: i // P)
    w8 = jnp.concatenate([w_part(wr), w_part(wi)], axis=2)
    c_part = lambda a: spread(a.reshape(NB, STATE_BLOCK, Q * SSM_GROUP), cols_qgh, lambda i: i // P, group_of_qgh)
    cpow = jnp.concatenate([c_part(ctr), c_part(-cti)], axis=1)
    kk_rows = jnp.transpose(kk.reshape(NB, G8, Q, SSM_GROUP, Q * SSM_GROUP), (0, 2, 1, 3, 4)).reshape(NB, qgh, -1)
    toep = spread(kk_rows, cols_qgh, group_of_qgh, group_of_qgh)
    lane_k = lambda a: a.reshape(NB, 1, -1)
    d8 = jnp.tile(lane_k(s5_d[0].astype(f32)), (1, 1, Q))
    weights = lambda pr_, pi_: (w8, toep, cpow, lane_k(aqr), lane_k(aqi), lane_k(pr_), lane_k(pi_), d8)

    assert tm == ts
    n_small = -(-(bs + 1) // SUBLANES) * SUBLANES
    u_small = jnp.concatenate([u_s, u_m, jnp.zeros(_s5_tokens_shape(1, (n_small - bs - 1) * ts), f32)], axis=3)
    pad_state = lambda a: jnp.pad(a[0].reshape(bs, G * P).astype(f32), ((0, n_small - bs), (0, 0)))[None]
    ys_small, sr_small, si_small = _s5(u_small, pad_state(state_s5_re), pad_state(state_s5_im),
                                       weights(aqr, aqi), n_small, False, "s5_short")
    x0r = jnp.broadcast_to(sr_small[:, bs:bs + 1, :], (bp, 1, G * P))
    x0i = jnp.broadcast_to(si_small[:, bs:bs + 1, :], (bp, 1, G * P))
    ys_p, sr_p, si_p = _s5(u_p, x0r, x0i, weights(apr, api), n_seg_p, True, "s5_prompt")

    wg = w_glu[0].astype(bf16)
    bg = b_glu[0][None, :].astype(f32)
    wo = w_out[0].astype(bf16)
    y_prompt = _outproj(x_prompt, att_p, ga_p, ys_p, gs_p, wg, bg, wo, tile_p, "outproj_prompt")
    y_sample = _outproj(x_sample.reshape(1, ls, d), att_s.reshape(1, ls, D_ATT), ga_s, ys_small[:, :, :, :ls // S5_Q], gs_s,
                        wg, bg, wo, tile_s, "outproj_sample").reshape(bs, ts, d)

    def heads_with_meta(m, x):
        m = jnp.broadcast_to(m.reshape(1, n_meta, N_HEADS, HEAD_DIM), (bp, n_meta, N_HEADS, HEAD_DIM))
        x = jnp.pad(x.reshape(bp, lp, N_HEADS, HEAD_DIM), ((0, 0), (n_meta, 0), (0, 0), (0, 0)))
        return lax.dynamic_update_slice(x, m, (0, 0, 0, 0))[None]

    new_k_prompt = heads_with_meta(kf_m, kf_p)
    new_v_prompt = heads_with_meta(vf_m, vf_p)
    lft_all = jnp.concatenate([jnp.broadcast_to(lft_m, (bp, N_HEADS, n_meta)), lft_p], axis=2)
    new_logf_prompt = jnp.transpose(lft_all, (0, 2, 1))[None]
    new_s5_re_prompt = sr_p.reshape(1, bp, G, P)
    new_s5_im_prompt = si_p.reshape(1, bp, G, P)
    new_k_sample = kf_s.reshape(1, bs, ts, N_HEADS, HEAD_DIM)
    new_v_sample = vf_s.reshape(1, bs, ts, N_HEADS, HEAD_DIM)
    new_logf_sample = jnp.transpose(lft_new, (0, 2, 1))[None]
    new_s5_re_sample = sr_small[:, :bs].reshape(1, bs, G, P)
    new_s5_im_sample = si_small[:, :bs].reshape(1, bs, G, P)
    return (y_prompt, y_sample, new_k_prompt, new_v_prompt, new_logf_prompt, new_s5_re_prompt, new_s5_im_prompt,
            new_k_sample, new_v_sample, new_logf_sample, new_s5_re_sample, new_s5_im_sample)
```

```python
import functools

import jax
import jax.numpy as jnp
from jax import lax
from jax.experimental import pallas as pl
from jax.experimental.pallas import tpu as pltpu

N_HEADS = 8
HEAD_DIM = 64
D_ATT = N_HEADS * HEAD_DIM
SSM_GROUP = 16
SSM_STATE = 64
D_SSM = 512
N_GROUPS = D_SSM // SSM_GROUP
EPS = 1e-6
NEG_INF = -1e30

LANES = 128
SUBLANES = 8
AUG = LANES // N_HEADS
GROUPS_PER_BLOCK = LANES // SSM_GROUP
STATE_BLOCK = GROUPS_PER_BLOCK * SSM_STATE
N_SSM_BLOCKS = D_SSM // LANES
VMEM_LIMIT = 56 * 1024 * 1024
EXP_ZERO = 105.0

ROW_TILE = 512
ATT_TILE = 512
ATT_GROUP = 16
ATT_UNROLL = 8
LOG2E = 1.4426950408889634
PROMPT_SEGMENTS = 512
S5_Q = 8

bf16 = jnp.bfloat16
f32 = jnp.float32


def _cparams(*sem):
    return pltpu.CompilerParams(dimension_semantics=sem, vmem_limit_bytes=VMEM_LIMIT)


def _split3(x):
    hi = x.astype(bf16).astype(f32)
    r1 = x - hi
    mid = r1.astype(bf16).astype(f32)
    lo = (r1 - mid).astype(bf16).astype(f32)
    return hi, mid, lo


def _dot(a, b):
    return jnp.dot(a, b, preferred_element_type=f32)


def _dot_nt(a, b):
    return lax.dot_general(a, b, (((1,), (1,)), ((), ())), preferred_element_type=f32)


def _s5_tokens_shape(b, l):
    return (b, N_SSM_BLOCKS, S5_Q, l // S5_Q, LANES)


def _s5_tokens_spec(rows, index_map, **kw):
    return pl.BlockSpec((1, N_SSM_BLOCKS, S5_Q, rows // S5_Q, LANES), index_map, **kw)


def _inproj_kernel(chain, n_blk, join, *refs):
    n_in = 12
    if not join:
        _inproj_tile(chain, n_blk, False, *refs)
        return
    kmt_ref, vmt_ref = refs[n_in:n_in + 2]
    refs = refs[:n_in] + refs[n_in + 2:]
    kf_ref, vf_ref = refs[n_in + 3], refs[n_in + 4]
    prev_ref, tails_ref = refs[-2:]
    j = pl.program_id(1)
    n_tiles = pl.num_programs(1) - 2
    lane_t = lax.broadcasted_iota(jnp.int32, (D_ATT, LANES), 1)

    def emit():
        for idx, t_ref in enumerate((kf_ref, vf_ref)):
            rolled = pltpu.roll(prev_ref[idx].T, join, axis=1)
            first = jnp.where(lane_t < join, tails_ref[idx], rolled[:, :LANES])
            t_ref[0] = first if rolled.shape[1] == LANES else jnp.concatenate([first, rolled[:, LANES:]], axis=1)
            tails_ref[idx] = jnp.where(j > 0, rolled[:, :LANES], tails_ref[idx])

    @pl.when(j == 0)
    def _():
        tails_ref[0] = kmt_ref[0]
        tails_ref[1] = vmt_ref[0]
        prev_ref[...] = jnp.zeros(prev_ref.shape, f32)

    @pl.when(j < n_tiles)
    def _():
        emit()
        _inproj_tile(chain, n_blk, True, *refs)

    pl.when(j == n_tiles)(emit)

    @pl.when(j == n_tiles + 1)
    def _():
        for idx, t_ref in enumerate((kf_ref, vf_ref)):
            t_ref[0] = jnp.zeros(t_ref.shape[1:], f32)
            t_ref[0, :, :LANES] = tails_ref[idx]


def _inproj_tile(chain, n_blk, join, x_ref, c0_ref, g_ref, wm_ref, wf_ref, bf_ref, qg_ref, kg_ref, eh_ref,
                 tri_ref, negm_ref, sel_ref, q_ref, kb_ref, vb_ref, kf_ref, vf_ref, lft_ref, edge_ref,
                 qcat_ref, kcat_ref, ga_ref, u_ref, gs_ref, carry_ref, us_ref, prev_ref=None, tails_ref=None):
    x = x_ref[0]
    ms = jnp.mean(x * x, axis=-1, keepdims=True)
    xn = (x * lax.rsqrt(ms + EPS) * g_ref[...]).astype(bf16)
    z = _dot(xn, wm_ref[...])
    q = z[:, 0 * D_ATT:1 * D_ATT]
    k = z[:, 1 * D_ATT:2 * D_ATT]
    v = z[:, 2 * D_ATT:3 * D_ATT]
    eh = eh_ref[...]
    qn = q * lax.rsqrt(_dot((q * q).astype(bf16), eh) + EPS) * qg_ref[...]
    kn = k * lax.rsqrt(_dot((k * k).astype(bf16), eh) + EPS) * kg_ref[...]
    q_ref[0] = qn.astype(bf16)
    kb_ref[0] = kn.astype(bf16)
    vb_ref[0] = v.astype(bf16)
    if join:
        prev_ref[0] = kn
        prev_ref[1] = v
    else:
        kf_ref[0] = kn
        vf_ref[0] = v
    ga_ref[0] = z[:, 3 * D_ATT:4 * D_ATT].astype(bf16)
    gs_ref[0] = z[:, 4 * D_ATT + D_SSM:].astype(bf16)
    for kb in range(N_SSM_BLOCKS):
        us_ref[kb] = z[:, 4 * D_ATT + kb * LANES:4 * D_ATT + (kb + 1) * LANES]
    for kb in range(N_SSM_BLOCKS):
        for qq in range(S5_Q):
            u_ref[0, kb, qq] = us_ref[kb, pl.ds(qq, x.shape[0] // S5_Q, stride=S5_Q), :]

    zf = _dot(xn, wf_ref[...]) + bf_ref[...]
    lf = jnp.minimum(zf, 0.0) - jnp.log1p(jnp.exp(-jnp.abs(zf)))
    sel = sel_ref[...]
    lft_ref[0] = sum(_dot_nt(sel, part.astype(bf16)) for part in _split3(lf))

    if chain:
        @pl.when(pl.program_id(1) == 0)
        def _():
            carry_ref[...] = c0_ref[0]
        carry = carry_ref[...]
    else:
        carry = jnp.zeros((1, LANES), f32)
    tri = tri_ref[...]
    tb = x.shape[0] // n_blk
    cums = []
    for r in range(n_blk):
        hi, mid, lo = _split3(lf[r * tb:(r + 1) * tb])
        c = _dot(tri, hi.astype(bf16)) + _dot(tri, mid.astype(bf16)) + _dot(tri, lo.astype(bf16)) + carry
        cums.append(c)
        carry = c[tb - 1:tb, :]
        edge_ref[0, r] = jnp.concatenate([c[0:1, :], carry, jnp.zeros((SUBLANES - 2, LANES), f32)], axis=0)
    cum = cums[0] if n_blk == 1 else jnp.concatenate(cums, axis=0)
    if chain:
        carry_ref[...] = carry

    slot = lax.broadcasted_iota(jnp.int32, (1, LANES), 1) % AUG
    hi, mid, lo = _split3(cum * LOG2E)
    qc = jnp.where(slot == 0, hi, jnp.where(slot == 1, mid, jnp.where(slot == 2, lo,
                   jnp.where(slot <= 6, 1.0, 0.0))))
    kc = jnp.where(slot <= 2, 1.0, jnp.where(slot == 3, -hi, jnp.where(slot == 4, -mid,
                   jnp.where(slot == 5, -lo, jnp.where(slot == 6, negm_ref[...], 0.0)))))
    qcat_ref[0] = qc.astype(bf16)
    kcat_ref[0] = kc.astype(bf16)


def _inproj(x, c0, params, tile, tri, chain, name, meta_t=None):
    bx, lx, d = x.shape
    n_tiles = lx // tile
    n_blk = tile // tri.shape[0]
    join = 0 if meta_t is None else meta_t[2]
    last = n_tiles - 1
    tj = (lambda j: jnp.minimum(j, last)) if join else (lambda j: j)
    tprev = lambda j: jnp.maximum(j - 1, 0)
    full = lambda a: pl.BlockSpec(a.shape, lambda b, j: (0,) * a.ndim)
    row = lambda w: pl.BlockSpec((1, tile, w), lambda b, j: (b, tj(j), 0))
    sd = jax.ShapeDtypeStruct
    rows = lambda w, t: (row(w), sd((bx, lx, w), t))
    kv = (rows(D_ATT, f32) if not join else
          (pl.BlockSpec((1, D_ATT, tile), lambda b, j: (b, 0, tprev(j))), sd((bx, D_ATT, join + lx), f32)))
    outs = [rows(D_ATT, bf16), rows(D_ATT, bf16), rows(D_ATT, bf16), kv, kv,
            (pl.BlockSpec((1, N_HEADS, tile), lambda b, j: (b, 0, tj(j))), sd((bx, N_HEADS, lx), f32)),
            (pl.BlockSpec((1, n_blk, SUBLANES, LANES), lambda b, j: (b, tj(j), 0, 0)),
             sd((bx, n_tiles * n_blk, SUBLANES, LANES), f32)),
            rows(LANES, bf16), rows(LANES, bf16), rows(D_ATT, bf16),
            (_s5_tokens_spec(tile, lambda b, j: (b, 0, 0, tj(j), 0)), sd(_s5_tokens_shape(bx, lx), f32)),
            rows(D_SSM, bf16)]
    g_in, wm, wf, bfr, qg, kg, eh, negm, sel = params
    operands = (g_in, wm, wf, bfr, qg, kg, eh, tri, negm, sel) + (tuple(meta_t[:2]) if join else ())
    scratch = [pltpu.VMEM((1, LANES), f32), pltpu.VMEM((N_SSM_BLOCKS, tile, LANES), f32)]
    if join:
        scratch += [pltpu.VMEM((2, tile, D_ATT), f32), pltpu.VMEM((2, D_ATT, LANES), f32)]
    return pl.pallas_call(
        functools.partial(_inproj_kernel, chain, n_blk, join),
        grid=(bx, n_tiles + (2 if join else 0)),
        in_specs=[row(d), pl.BlockSpec((1, 1, LANES), lambda b, j: (b, 0, 0))] + [full(a) for a in operands],
        out_specs=[o[0] for o in outs],
        out_shape=[o[1] for o in outs],
        scratch_shapes=scratch,
        compiler_params=_cparams("parallel", "arbitrary"),
        name=name,
    )(x, c0, *operands)


def _attn_kernel(tile, nq, gq, n_meta, jmin_ref, mflag_ref, q_ref, qc_ref, kb_ref, vb_ref, kc_ref,
                 kbm_ref, vbm_ref, kcm_ref, o_ref, acc_ref, bnd_ref):
    b = pl.program_id(0)
    p = pl.program_id(1)
    g = pl.program_id(2)
    lane = lax.broadcasted_iota(jnp.int32, (1, LANES), 1)
    mcol = lax.broadcasted_iota(jnp.int32, (tile, LANES), 1)
    col = lax.broadcasted_iota(jnp.int32, (tile, tile), 1)
    row = lax.broadcasted_iota(jnp.int32, (tile, tile), 0)

    @pl.when(g == 0)
    def _():
        bnd_ref[0] = jnp.where(col <= row, -NEG_INF, NEG_INF)
        bnd_ref[1] = jnp.full((tile, tile), -NEG_INF, f32)
        bnd_ref[2] = jnp.full((tile, tile), NEG_INF, f32)

    def rows_of(j):
        return pl.ds(pl.multiple_of(j * tile, tile), tile)

    heads = range(2)
    base = [(b * N_HEADS + 2 * p + hh) * nq + g * gq for hh in heads]
    m_q = [(lane // HEAD_DIM == hh).astype(bf16) for hh in heads]
    m_o = [(lane // HEAD_DIM != hh).astype(bf16) for hh in heads]
    m_c = [(lane // AUG == 2 * p + hh).astype(bf16) for hh in heads]

    def clear(r, carry):
        for hh in heads:
            acc_ref[hh, r] = jnp.zeros(acc_ref.shape[2:], f32)
        return carry

    lax.fori_loop(0, gq, clear, 0)

    def scores(hh, rc, ks, kc):
        q_op = jnp.concatenate([q_ref[0, rows_of(rc), :] * m_q[hh], qc_ref[0, rows_of(rc), :] * m_c[hh]], axis=1)
        return _dot_nt(q_op, jnp.concatenate([ks, kc], axis=1))

    def pv(hh, pe, vs):
        return _dot(pe, vs * m_q[hh] + m_o[hh])

    def item_probs(hh, r, j):
        rc = jnp.minimum(r, gq - 1)
        kind = jnp.where(r < gq, jnp.where(j == g * gq + rc, 0, 1), 2)
        s = scores(hh, rc, kb_ref[0, rows_of(j), :], kc_ref[0, rows_of(j), :])
        return jnp.exp2(jnp.minimum(s, bnd_ref[kind])).astype(bf16), rc

    def item_pv(hh, pe, rc, j):
        acc_ref[hh, rc] += pv(hh, pe, vb_ref[0, rows_of(j), :])

    def advance(hh, r, j):
        last = jnp.logical_and(r < gq, j == g * gq + r)
        r2 = jnp.where(last, r + 1, r)
        j2 = jnp.where(last, jmin_ref[base[hh] + jnp.minimum(r + 1, gq - 1)], jnp.where(r < gq, j + 1, j))
        return r2.astype(jnp.int32), j2.astype(jnp.int32)

    def first_item(hh):
        j0 = jmin_ref[base[hh]]
        pe0, rc0 = item_probs(hh, jnp.int32(0), j0)
        return (pe0, rc0, j0) + advance(hh, jnp.int32(0), j0)

    def pipeline(hh, carry):
        def body(t, carry):
            pe, rp, jp, r, j = carry
            for _ in range(ATT_UNROLL):
                pn, rn = item_probs(hh, r, j)
                item_pv(hh, pe, rp, jp)
                pe, rp, jp = pn, rn, j
                r, j = advance(hh, r, j)
            return pe, rp, jp, r, j

        n_items = sum(g * gq + r - jmin_ref[base[hh] + r] + 1 for r in range(gq))
        trips = (n_items + ATT_UNROLL - 2) // ATT_UNROLL
        return lax.fori_loop(0, trips, body, carry)[:3]

    tail0 = pipeline(0, first_item(0))
    head1 = first_item(1)
    item_pv(0, *tail0)
    item_pv(1, *pipeline(1, head1))

    def meta_tile(r, carry):
        for hh in heads:
            @pl.when(mflag_ref[base[hh] + r] > 0)
            def _(hh=hh):
                s = jnp.where(mcol < n_meta, scores(hh, r, kbm_ref[...], kcm_ref[...]), NEG_INF)
                acc_ref[hh, r] += pv(hh, jnp.exp2(s).astype(bf16), vbm_ref[...])
        return carry

    lax.fori_loop(0, gq, meta_tile, 0)

    for r in range(gq):
        outs = []
        for hh in heads:
            a = acc_ref[hh, r]
            outs.append(a / pltpu.roll(a, HEAD_DIM, axis=1))
        o_ref[0, r * tile:(r + 1) * tile, :] = jnp.where(lane < HEAD_DIM, outs[0], outs[1]).astype(bf16)


def _prompt_attention(q, qcat, kb, vb, kcat, kbm, vbm, kcm, jmin, mflag, n_meta):
    b, l, _ = q.shape
    tile = min(ATT_TILE, l)
    nq = l // tile
    gq = min(ATT_GROUP, nq)
    rows = gq * tile
    grid_spec = pltpu.PrefetchScalarGridSpec(
        num_scalar_prefetch=2,
        grid=(b, N_HEADS // 2, nq // gq),
        in_specs=[
            pl.BlockSpec((1, rows, LANES), lambda b, p, i, *_: (b, i, p)),
            pl.BlockSpec((1, rows, LANES), lambda b, p, i, *_: (b, i, 0)),
            pl.BlockSpec((1, l, LANES), lambda b, p, i, *_: (b, 0, p)),
            pl.BlockSpec((1, l, LANES), lambda b, p, i, *_: (b, 0, p)),
            pl.BlockSpec((1, l, LANES), lambda b, p, i, *_: (b, 0, 0)),
            pl.BlockSpec((LANES, LANES), lambda b, p, i, *_: (0, p)),
            pl.BlockSpec((LANES, LANES), lambda b, p, i, *_: (0, p)),
            pl.BlockSpec((LANES, LANES), lambda b, p, i, *_: (0, 0)),
        ],
        out_specs=pl.BlockSpec((1, rows, LANES), lambda b, p, i, *_: (b, i, p)),
        scratch_shapes=[pltpu.VMEM((2, gq, tile, LANES), f32), pltpu.VMEM((3, tile, tile), f32)],
    )
    return pl.pallas_call(
        functools.partial(_attn_kernel, tile, nq, gq, n_meta),
        grid_spec=grid_spec,
        out_shape=jax.ShapeDtypeStruct((b, l, D_ATT), bf16),
        compiler_params=_cparams("parallel", "parallel", "arbitrary"),
        name="prompt_attention",
    )(jmin, mflag, q, qcat, kb, vb, kcat, kbm, vbm, kcm)


def _sattn_kernel(past, tn, q_ref, kn_ref, vn_ref, ckt_ref, cvt_ref, lft_ref, o_ref):
    s_len = lft_ref.shape[-1]
    rows = N_HEADS * tn
    c = lft_ref[0]
    lane_h = lax.broadcasted_iota(jnp.int32, c.shape, 1)
    sh = 1
    while sh < s_len:
        c = c + jnp.where(lane_h >= sh, pltpu.roll(c, sh, axis=1), 0.0)
        sh *= 2
    cexp = jnp.concatenate([jnp.broadcast_to(c[h:h + 1, :], (tn, s_len)) for h in range(N_HEADS)], axis=0)
    rix = lax.broadcasted_iota(jnp.int32, (rows, s_len), 0)
    lix = lax.broadcasted_iota(jnp.int32, (rows, s_len), 1)
    qpos = past + rix % tn
    cq = jnp.sum(jnp.where(lix == qpos, cexp, 0.0), axis=1, keepdims=True)
    pad = jnp.zeros((s_len - past - tn, D_ATT), bf16)
    k_new = jnp.concatenate([kn_ref[0], pad], axis=0)
    v_new = jnp.concatenate([vn_ref[0], pad], axis=0)
    qrep = jnp.concatenate([q_ref[0].astype(f32)] * N_HEADS, axis=0)
    r5 = lax.broadcasted_iota(jnp.int32, (rows, D_ATT), 0)
    l5 = lax.broadcasted_iota(jnp.int32, (rows, D_ATT), 1)
    qrows = jnp.where(l5 // HEAD_DIM == r5 // tn, qrep, 0.0).astype(bf16)
    s = jnp.concatenate([_dot(qrows, ckt_ref[0].astype(bf16)), _dot_nt(qrows, k_new)], axis=1)
    s = s + (cq - cexp) * LOG2E
    s = jnp.where(lix <= qpos, s, NEG_INF)
    m = jnp.max(s, axis=1, keepdims=True)
    pe = jnp.exp2(s - m)
    den = jnp.sum(pe, axis=1, keepdims=True)
    pb = pe.astype(bf16)
    o = (_dot_nt(pb[:, :past], cvt_ref[0].astype(bf16)) + _dot(pb[:, past:], v_new)) / den
    lo = lax.broadcasted_iota(jnp.int32, (tn, D_ATT), 1)
    out = jnp.zeros((tn, D_ATT), f32)
    for h in range(N_HEADS):
        out = out + jnp.where(lo // HEAD_DIM == h, o[h * tn:(h + 1) * tn, :], 0.0)
    o_ref[0] = out.astype(bf16)


def _sample_attention(q, kn, vn, ckt, cvt, lft):
    bs, tn, _ = q.shape
    past = ckt.shape[2]
    s_len = lft.shape[-1]
    blk = lambda n, w: pl.BlockSpec((1, n, w), lambda b: (b, 0, 0))
    return pl.pallas_call(
        functools.partial(_sattn_kernel, past, tn),
        grid=(bs,),
        in_specs=[blk(tn, D_ATT), blk(tn, D_ATT), blk(tn, D_ATT), blk(D_ATT, past), blk(D_ATT, past),
                  blk(N_HEADS, s_len)],
        out_specs=blk(tn, D_ATT),
        out_shape=jax.ShapeDtypeStruct((bs, tn, D_ATT), bf16),
        compiler_params=_cparams("parallel"),
        name="sample_attention",
    )(q, kn, vn, ckt, cvt, lft)


def _s5prep_kernel(n_sq, are_ref, aim_ref, ldt_ref, btr_ref, bti_ref, cre_ref, cim_ref,
                   aqr_ref, aqi_ref, apr_ref, api_ref, wr_ref, wi_ref, ctr_ref, cti_ref, kk_ref):
    are = are_ref[...]
    aim = aim_ref[...]
    dt = jnp.exp(ldt_ref[...])
    mag = jnp.exp(are * dt)
    ph = aim * dt
    abr = mag * jnp.cos(ph)
    abi = mag * jnp.sin(ph)
    nr = abr - 1.0
    den = are * are + aim * aim
    fr = (nr * are + abi * aim) / den
    fi = (abi * are - nr * aim) / den
    pw = [(jnp.ones_like(abr), jnp.zeros_like(abr))]
    for _ in range(S5_Q):
        pr, pi = pw[-1]
        pw.append((pr * abr - pi * abi, pr * abi + pi * abr))
    aqr_ref[...], aqi_ref[...] = pw[S5_Q]
    pr, pi = pw[S5_Q]
    for _ in range(n_sq):
        pr, pi = pr * pr - pi * pi, 2.0 * pr * pi
    apr_ref[...] = pr
    api_ref[...] = pi
    p_states = are.shape[1]
    ident = (lax.broadcasted_iota(jnp.int32, (p_states, p_states), 0)
             == lax.broadcasted_iota(jnp.int32, (p_states, p_states), 1)).astype(bf16)
    lane = lax.broadcasted_iota(jnp.int32, (SSM_GROUP, S5_Q * SSM_GROUP), 1)
    hi = functools.partial(lax.dot_general, dimension_numbers=(((1,), (1,)), ((), ())),
                           precision=lax.Precision.HIGHEST, preferred_element_type=f32)
    for g in range(are.shape[0]):
        row = slice(g, g + 1)
        btr = btr_ref[g]
        bti = bti_ref[g]
        bbr = fr[row] * btr - fi[row] * bti
        bbi = fr[row] * bti + fi[row] * btr
        cre = cre_ref[g]
        cim = cim_ref[g]
        car_all, cai_all = [cre], [cim]
        for q in range(S5_Q):
            er, ei = pw[S5_Q - 1 - q]
            wr_ref[q, g] = er[row] * bbr - ei[row] * bbi
            wi_ref[q, g] = er[row] * bbi + ei[row] * bbr
            gr, gi = pw[q + 1]
            car_all.append(cre * gr[row] - cim * gi[row])
            cai_all.append(cre * gi[row] + cim * gr[row])
        ctr_ref[g] = _dot_nt(ident, jnp.concatenate(car_all[1:], axis=0).astype(bf16))
        cti_ref[g] = _dot_nt(ident, jnp.concatenate(cai_all[1:], axis=0).astype(bf16))
        kt = (hi(bbr, jnp.concatenate(car_all[:S5_Q], axis=0))
              - hi(bbi, jnp.concatenate(cai_all[:S5_Q], axis=0)))
        kk_ref[g] = jnp.concatenate(
            [kt if qi == 0 else jnp.where(lane >= qi * SSM_GROUP, pltpu.roll(kt, qi * SSM_GROUP, axis=1), 0.0)
             for qi in range(S5_Q)], axis=0)


def _s5_prep(a_re, a_im, log_dt, bt_re, bt_im, c_re, c_im, n_sq):
    g, p = a_re.shape
    sd = jax.ShapeDtypeStruct
    qh = S5_Q * SSM_GROUP
    return pl.pallas_call(
        functools.partial(_s5prep_kernel, n_sq),
        out_shape=[sd((g, p), f32)] * 4 + [sd((S5_Q,) + bt_re.shape, f32)] * 2 + [sd((g, p, qh), f32)] * 2
        + [sd((g, qh, qh), f32)],
        name="s5_prep",
    )(a_re, a_im, log_dt, bt_re, bt_im, c_re, c_im)


def _s5_kernel(chain, n_seg, n_sb, u_ref, w8_ref, toep_ref, cpow_ref, aqr_ref, aqi_ref, apr_ref, api_ref, d8_ref,
               x0r_ref, x0i_ref, y_ref, sr_ref, si_ref, xr_ref, xi_ref, cr_ref, ci_ref):
    aqr = aqr_ref[0]
    aqi = aqi_ref[0]
    w8 = w8_ref[0]
    toep = toep_ref[0]
    cpow = cpow_ref[0]
    d8 = d8_ref[0]

    def step(with_y, sb):
        rows = pl.ds(sb, n_seg, stride=n_sb) if n_sb > 1 else pl.ds(0, n_seg)
        uf = jnp.concatenate([u_ref[0, 0, q, rows, :] for q in range(S5_Q)], axis=1)
        ub = uf.astype(bf16)
        xr = xr_ref[...]
        xi = xi_ref[...]
        if with_y:
            xc = jnp.concatenate([xr, xi], axis=1).astype(bf16)
            y = _dot(ub, toep) + _dot(xc, cpow) + d8 * uf
            for q in range(S5_Q):
                y_ref[0, 0, q, rows, :] = y[:, q * LANES:(q + 1) * LANES]
        pj = _dot(ub, w8)
        xr_ref[...] = aqr * xr - aqi * xi + pj[:, :STATE_BLOCK]
        xi_ref[...] = aqr * xi + aqi * xr + pj[:, STATE_BLOCK:]

    if chain:
        xr_ref[...] = jnp.zeros(xr_ref.shape, f32)
        xi_ref[...] = jnp.zeros(xi_ref.shape, f32)
        for sb in range(n_sb):
            step(False, sb)
        apr = apr_ref[0]
        api = api_ref[0]

        def scan(c, carry):
            kr, ki = carry
            cr_ref[pl.ds(c, 1), :] = kr
            ci_ref[pl.ds(c, 1), :] = ki
            er = xr_ref[pl.ds(c, 1), :]
            ei = xi_ref[pl.ds(c, 1), :]
            return apr * kr - api * ki + er, apr * ki + api * kr + ei

        kr, ki = lax.fori_loop(0, n_seg, scan, (x0r_ref[0], x0i_ref[0]))
        sr_ref[0] = kr
        si_ref[0] = ki
        xr_ref[...] = cr_ref[...]
        xi_ref[...] = ci_ref[...]
        for sb in range(n_sb):
            step(True, sb)
    else:
        xr_ref[...] = x0r_ref[0]
        xi_ref[...] = x0i_ref[0]
        for sb in range(n_sb):
            step(True, sb)
        sr_ref[0] = xr_ref[...]
        si_ref[0] = xi_ref[...]


def _s5(u, x0r, x0i, weights, n_seg, chain, name):
    bq, _, _, m, _ = u.shape
    n_sb = m // n_seg
    r0 = x0r.shape[1]
    per_k = lambda a: pl.BlockSpec((1,) + a.shape[1:], lambda b, k: (k, 0, 0))
    st = pl.BlockSpec((1, r0, STATE_BLOCK), lambda b, k: (b, 0, k))
    tok = lambda **kw: pl.BlockSpec((1, 1, S5_Q, m, LANES), lambda b, k: (b, k, 0, 0, 0), **kw)
    sd = jax.ShapeDtypeStruct
    return pl.pallas_call(
        functools.partial(_s5_kernel, chain, n_seg, n_sb),
        grid=(bq, N_SSM_BLOCKS),
        in_specs=[tok(pipeline_mode=pl.Buffered(1))] + [per_k(a) for a in weights] + [st, st],
        out_specs=[tok(), st, st],
        out_shape=[sd(u.shape, f32), sd(x0r.shape, f32), sd(x0i.shape, f32)],
        scratch_shapes=[pltpu.VMEM((n_seg, STATE_BLOCK), f32)] * 4,
        compiler_params=_cparams("parallel", "parallel"),
        name=name,
    )(u, *weights, x0r, x0i)


def _out_kernel(x_ref, att_ref, ga_ref, ys_ref, gs_ref, wg_ref, bg_ref, wo_ref, y_ref, ys_scr):
    ga = ga_ref[0].astype(f32)
    a = att_ref[0].astype(f32) * (ga * jax.nn.sigmoid(ga))
    rows = x_ref.shape[1]
    for kb in range(N_SSM_BLOCKS):
        for qq in range(S5_Q):
            ys_scr[kb, pl.ds(qq, rows // S5_Q, stride=S5_Q), :] = ys_ref[0, kb, qq]
    ys = jnp.concatenate([ys_scr[kb] for kb in range(N_SSM_BLOCKS)], axis=1)
    z = 0.5 * ys * (1.0 + jnp.tanh(0.7978845608028654 * (ys + 0.044715 * (ys * ys * ys))))
    gate = jax.nn.sigmoid(_dot(z.astype(bf16), wg_ref[...]) + bg_ref[...])
    gs = gs_ref[0].astype(f32)
    s5 = z * gate * (gs * jax.nn.sigmoid(gs))
    cat = jnp.concatenate([a, s5], axis=1).astype(bf16)
    y_ref[0] = x_ref[0] + _dot(cat, wo_ref[...])


def _outproj(x, att, ga, ys, gs, wg, bg, wo, tile, name):
    bx, lx, d = x.shape
    full = lambda a: pl.BlockSpec(a.shape, lambda b, j: (0,) * a.ndim)
    row = lambda w: pl.BlockSpec((1, tile, w), lambda b, j: (b, j, 0))
    return pl.pallas_call(
        _out_kernel,
        grid=(bx, lx // tile),
        in_specs=[row(d), row(D_ATT), row(D_ATT), _s5_tokens_spec(tile, lambda b, j: (b, 0, 0, j, 0)), row(D_SSM),
                  full(wg), full(bg), full(wo)],
        out_specs=row(d),
        out_shape=jax.ShapeDtypeStruct(x.shape, f32),
        scratch_shapes=[pltpu.VMEM((N_SSM_BLOCKS, tile, LANES), f32)],
        compiler_params=_cparams("parallel", "parallel"),
        name=name,
    )(x, att, ga, ys, gs, wg, bg, wo)


def _lower_tri(n):
    r = jnp.arange(n)
    return (r[None, :] <= r[:, None]).astype(bf16)


def kernel(x_prompt, x_sample, cache_k, cache_v, cache_logf, state_s5_re, state_s5_im, meta_tokens, norm_g,
           w_in, b_f, q_norm_g, k_norm_g, s5_a_re, s5_a_im, s5_log_dt, s5_b_re, s5_b_im, s5_c_re, s5_c_im,
           s5_d, w_glu, b_glu, w_out):
    assert norm_g.shape[0] == 1, "single-layer step"
    bp, lp, d = x_prompt.shape
    bs, ts, _ = x_sample.shape
    n_meta = meta_tokens.shape[0]
    past = cache_k.shape[2]
    G, P = N_GROUPS, SSM_STATE

    w = w_in[0]
    o3 = 3 * D_ATT
    o4 = o3 + N_HEADS
    wm = jnp.concatenate([w[:, :o3], w[:, o4:]], axis=1).astype(bf16)
    wf = jnp.repeat(w[:, o3:o4], AUG, axis=1).astype(bf16)
    bfr = jnp.repeat(b_f[0], AUG)[None, :].astype(f32)
    g_in = norm_g[0][None, :].astype(f32)
    qg = (jnp.tile(q_norm_g[0], N_HEADS) * (HEAD_DIM ** -0.5 * LOG2E))[None, :].astype(f32)
    kg = jnp.tile(k_norm_g[0], N_HEADS)[None, :].astype(f32)
    hid = jnp.arange(D_ATT) // HEAD_DIM
    eh = ((hid[:, None] == hid[None, :]).astype(f32) / HEAD_DIM).astype(bf16)
    sel = (jnp.arange(LANES)[None, :] == AUG * jnp.arange(N_HEADS)[:, None]).astype(bf16)
    m_bound = 8.0 * jnp.max(jnp.abs(q_norm_g[0])) * jnp.max(jnp.abs(k_norm_g[0])) * 1.02
    negm = jnp.full((1, LANES), -LOG2E, f32) * m_bound
    params = (g_in, wm, wf, bfr, qg, kg, eh, negm, sel)

    tm = n_meta
    meta = _inproj(meta_tokens[None].astype(f32), jnp.zeros((1, 1, LANES), f32), params, tm, _lower_tri(tm), True,
                   "inproj_meta")
    (_, kb_m, vb_m, kf_m, vf_m, lft_m, edge_m, _, kcat_m, _, u_m, _) = meta
    cum_meta_end = edge_m[:, -1, 1:2, :]
    c0 = jnp.broadcast_to(cum_meta_end, (bp, 1, LANES))
    tile_p = min(ROW_TILE, lp)
    tb_p = min(LANES, tile_p)
    meta_cols = lambda a: jnp.pad(jnp.swapaxes(a, 1, 2), ((0, 0), (0, 0), (0, LANES - tm)))
    (q_p, kb_p, vb_p, kt_p, vt_p, lft_p, edge_p, qcat_p, kcat_p, ga_p, u_p, gs_p) = _inproj(
        x_prompt, c0, params, tile_p, _lower_tri(tb_p), True, "inproj_prompt",
        meta_t=(meta_cols(kf_m), meta_cols(vf_m), n_meta))
    ls = bs * ts
    tile_s = min(ROW_TILE, ls)
    (q_s, kb_s, vb_s, kf_s, vf_s, lft_s, _, _, _, ga_s, u_s, gs_s) = _inproj(
        x_sample.reshape(1, ls, d), jnp.zeros((1, 1, LANES), f32), params, tile_s,
        _lower_tri(min(LANES, tile_s)), False, "inproj_sample")

    tile_a = min(ATT_TILE, lp)
    nq = lp // tile_a
    per_tile = tile_a // tb_p
    edges = edge_p[:, :, :2, ::AUG]
    q_start = edges[:, 0::per_tile, 0, :]
    k_end = edges[:, per_tile - 1::per_tile, 1, :]
    dmax = q_start[:, :, None, :] - k_end[:, None, :, :]
    jmin = jnp.sum((dmax < -EXP_ZERO).astype(jnp.int32), axis=2)
    jmin = jnp.minimum(jmin, jnp.arange(nq, dtype=jnp.int32)[None, :, None])
    jmin = jnp.transpose(jmin, (0, 2, 1)).reshape(-1)
    mflag = (q_start - cum_meta_end[:, :, ::AUG] >= -EXP_ZERO).astype(jnp.int32)
    mflag = jnp.transpose(mflag, (0, 2, 1)).reshape(-1)
    padm = lambda a: jnp.pad(a[0], ((0, LANES - tm), (0, 0)))
    att_p = _prompt_attention(q_p, qcat_p, kb_p, vb_p, kcat_p, padm(kb_m), padm(vb_m), padm(kcat_m),
                              jmin, mflag, n_meta)

    lft_new = jnp.transpose(lft_s.reshape(N_HEADS, bs, ts), (1, 0, 2))
    s_len = -(-(past + ts) // LANES) * LANES
    lft = jnp.concatenate([jnp.transpose(cache_logf[0].astype(f32), (0, 2, 1)), lft_new,
                           jnp.zeros((bs, N_HEADS, s_len - past - ts), f32)], axis=2)
    cache_t = lambda c: jnp.transpose(c[0], (0, 2, 3, 1)).reshape(bs, D_ATT, past)
    att_s = _sample_attention(q_s.reshape(bs, ts, D_ATT), kb_s.reshape(bs, ts, D_ATT), vb_s.reshape(bs, ts, D_ATT),
                              cache_t(cache_k), cache_t(cache_v), lft)

    n_seg_p = min(PROMPT_SEGMENTS, lp // S5_Q)
    n_sb_p = lp // (S5_Q * n_seg_p)
    n_sq = n_sb_p.bit_length() - 1
    assert n_sb_p == 1 << n_sq and ts % S5_Q == 0
    tr = lambda a: jnp.transpose(a[0], (0, 2, 1)).astype(f32)
    aqr, aqi, apr, api, wr, wi, ctr, cti, kk = _s5_prep(
        s5_a_re[0].astype(f32), s5_a_im[0].astype(f32), s5_log_dt[0][:, None].astype(f32),
        tr(s5_b_re), tr(s5_b_im), s5_c_re[0].astype(f32), s5_c_im[0].astype(f32), n_sq)
    NB, G8, Q = N_SSM_BLOCKS, GROUPS_PER_BLOCK, S5_Q
    qgh = Q * LANES

    def spread(x, col_of_src, row_group, col_group):
        n_src = x.shape[-1]
        n_rows = x.shape[1]
        n_cols = col_of_src.shape[0]
        sel = (col_of_src[None, :] == jnp.arange(n_src)[:, None]).astype(bf16)
        keep = row_group(jnp.arange(n_rows))[:, None] == col_group(jnp.arange(n_cols))[None, :]
        y = jnp.einsum("krs,sc->krc", x.astype(bf16), sel, preferred_element_type=f32)
        return jnp.where(keep[None], y, 0.0).astype(bf16)

    cols_state = jnp.arange(STATE_BLOCK) % P
    cols_qgh = (jnp.arange(qgh) // LANES) * SSM_GROUP + jnp.arange(qgh) % SSM_GROUP
    group_of_qgh = lambda i: (i // SSM_GROUP) % G8
    by_block = lambda a: jnp.transpose(a.reshape(Q, NB, G8 * SSM_GROUP, P), (1, 0, 2, 3)).reshape(NB, qgh, P)
    w_part = lambda a: spread(by_block(a), cols_state, group_of_qgh, lambda i: i // P)
    w8 = jnp.concatenate([w_part(wr), w_part(wi)], axis=2)
    c_part = lambda a: spread(a.reshape(NB, STATE_BLOCK, Q * SSM_GROUP), cols_qgh, lambda i: i // P, group_of_qgh)
    cpow = jnp.concatenate([c_part(ctr), c_part(-cti)], axis=1)
    kk_rows = jnp.transpose(kk.reshape(NB, G8, Q, SSM_GROUP, Q * SSM_GROUP), (0, 2, 1, 3, 4)).reshape(NB, qgh, -1)
    toep = spread(kk_rows, cols_qgh, group_of_qgh, group_of_qgh)
    lane_k = lambda a: a.reshape(NB, 1, -1)
    d8 = jnp.tile(lane_k(s5_d[0].astype(f32)), (1, 1, Q))
    weights = lambda pr_, pi_: (w8, toep, cpow, lane_k(aqr), lane_k(aqi), lane_k(pr_), lane_k(pi_), d8)

    assert tm == ts
    n_small = -(-(bs + 1) // SUBLANES) * SUBLANES
    u_small = jnp.concatenate([u_s, u_m, jnp.zeros(_s5_tokens_shape(1, (n_small - bs - 1) * ts), f32)], axis=3)
    pad_state = lambda a: jnp.pad(a[0].reshape(bs, G * P).astype(f32), ((0, n_small - bs), (0, 0)))[None]
    ys_small, sr_small, si_small = _s5(u_small, pad_state(state_s5_re), pad_state(state_s5_im),
                                       weights(aqr, aqi), n_small, False, "s5_short")
    x0r = jnp.broadcast_to(sr_small[:, bs:bs + 1, :], (bp, 1, G * P))
    x0i = jnp.broadcast_to(si_small[:, bs:bs + 1, :], (bp, 1, G * P))
    ys_p, sr_p, si_p = _s5(u_p, x0r, x0i, weights(apr, api), n_seg_p, True, "s5_prompt")

    wg = w_glu[0].astype(bf16)
    bg = b_glu[0][None, :].astype(f32)
    wo = w_out[0].astype(bf16)
    y_prompt = _outproj(x_prompt, att_p, ga_p, ys_p, gs_p, wg, bg, wo, tile_p, "outproj_prompt")
    y_sample = _outproj(x_sample.reshape(1, ls, d), att_s.reshape(1, ls, D_ATT), ga_s, ys_small[:, :, :, :ls // S5_Q],
                        gs_s, wg, bg, wo, tile_s, "outproj_sample").reshape(bs, ts, d)

    heads_t = lambda a: jnp.transpose(a.reshape(bp, N_HEADS, HEAD_DIM, n_meta + lp), (0, 3, 1, 2))[None]
    new_k_prompt = heads_t(kt_p)
    new_v_prompt = heads_t(vt_p)
    lft_all = jnp.concatenate([jnp.broadcast_to(lft_m, (bp, N_HEADS, n_meta)), lft_p], axis=2)
    new_logf_prompt = jnp.transpose(lft_all, (0, 2, 1))[None]
    new_s5_re_prompt = sr_p.reshape(1, bp, G, P)
    new_s5_im_prompt = si_p.reshape(1, bp, G, P)
    new_k_sample = kf_s.reshape(1, bs, ts, N_HEADS, HEAD_DIM)
    new_v_sample = vf_s.reshape(1, bs, ts, N_HEADS, HEAD_DIM)
    new_logf_sample = jnp.transpose(lft_new, (0, 2, 1))[None]
    new_s5_re_sample = sr_small[:, :bs].reshape(1, bs, G, P)
    new_s5_im_sample = si_small[:, :bs].reshape(1, bs, G, P)
    return (y_prompt, y_sample, new_k_prompt, new_v_prompt, new_logf_prompt, new_s5_re_prompt, new_s5_im_prompt,
            new_k_sample, new_v_sample, new_logf_sample, new_s5_re_sample, new_s5_im_sample)
```

```python
import functools

import jax
import jax.numpy as jnp
from jax import lax
from jax.experimental import pallas as pl
from jax.experimental.pallas import tpu as pltpu

N_HEADS = 8
HEAD_DIM = 64
D_ATT = N_HEADS * HEAD_DIM
SSM_GROUP = 16
SSM_STATE = 64
D_SSM = 512
N_GROUPS = D_SSM // SSM_GROUP
EPS = 1e-6
NEG_INF = -1e30

LANES = 128
SUBLANES = 8
AUG = LANES // N_HEADS
SHIFT_SLOT = 7
GROUPS_PER_BLOCK = LANES // SSM_GROUP
STATE_BLOCK = GROUPS_PER_BLOCK * SSM_STATE
N_SSM_BLOCKS = D_SSM // LANES
VMEM_LIMIT = 56 * 1024 * 1024
EXP_ZERO = 105.0
SHIFT_SAFE = 32.0

ROW_TILE = 512
ATT_TILE = 512
ATT_GROUP = 16
ATT_UNROLL = 8
LOG2E = 1.4426950408889634
PROMPT_SEGMENTS = 512
S5_Q = 8

bf16 = jnp.bfloat16
f32 = jnp.float32


def _cparams(*sem):
    return pltpu.CompilerParams(dimension_semantics=sem, vmem_limit_bytes=VMEM_LIMIT)


def _split3(x):
    hi = x.astype(bf16).astype(f32)
    r1 = x - hi
    mid = r1.astype(bf16).astype(f32)
    lo = (r1 - mid).astype(bf16).astype(f32)
    return hi, mid, lo


def _dot(a, b):
    return jnp.dot(a, b, preferred_element_type=f32)


def _dot_nt(a, b):
    return lax.dot_general(a, b, (((1,), (1,)), ((), ())), preferred_element_type=f32)


def _s5_tokens_shape(b, l):
    return (b, N_SSM_BLOCKS, S5_Q, l // S5_Q, LANES)


def _s5_tokens_spec(rows, index_map, **kw):
    return pl.BlockSpec((1, N_SSM_BLOCKS, S5_Q, rows // S5_Q, LANES), index_map, **kw)


def _inproj_kernel(chain, n_blk, join, *refs):
    n_in = 12
    if not join:
        _inproj_tile(chain, n_blk, False, *refs)
        return
    kmt_ref, vmt_ref = refs[n_in:n_in + 2]
    refs = refs[:n_in] + refs[n_in + 2:]
    kf_ref, vf_ref = refs[n_in + 3], refs[n_in + 4]
    prev_ref, tails_ref = refs[-2:]
    j = pl.program_id(1)
    n_tiles = pl.num_programs(1) - 2
    lane_t = lax.broadcasted_iota(jnp.int32, (D_ATT, LANES), 1)

    def emit():
        for idx, t_ref in enumerate((kf_ref, vf_ref)):
            rolled = pltpu.roll(prev_ref[idx].T, join, axis=1)
            first = jnp.where(lane_t < join, tails_ref[idx], rolled[:, :LANES])
            t_ref[0] = first if rolled.shape[1] == LANES else jnp.concatenate([first, rolled[:, LANES:]], axis=1)
            tails_ref[idx] = jnp.where(j > 0, rolled[:, :LANES], tails_ref[idx])

    @pl.when(j == 0)
    def _():
        tails_ref[0] = kmt_ref[0]
        tails_ref[1] = vmt_ref[0]
        prev_ref[...] = jnp.zeros(prev_ref.shape, f32)

    @pl.when(j < n_tiles)
    def _():
        emit()
        _inproj_tile(chain, n_blk, True, *refs)

    pl.when(j == n_tiles)(emit)

    @pl.when(j == n_tiles + 1)
    def _():
        for idx, t_ref in enumerate((kf_ref, vf_ref)):
            t_ref[0] = jnp.zeros(t_ref.shape[1:], f32)
            t_ref[0, :, :LANES] = tails_ref[idx]


def _inproj_tile(chain, n_blk, join, x_ref, c0_ref, g_ref, wm_ref, wf_ref, bf_ref, qg_ref, kg_ref, eh_ref,
                 tri_ref, negm_ref, sel_ref, q_ref, kb_ref, vb_ref, kf_ref, vf_ref, lft_ref, edge_ref,
                 qcat_ref, kcat_ref, ga_ref, u_ref, gs_ref, carry_ref, us_ref, prev_ref=None, tails_ref=None):
    x = x_ref[0]
    ms = jnp.mean(x * x, axis=-1, keepdims=True)
    xn = (x * lax.rsqrt(ms + EPS) * g_ref[...]).astype(bf16)
    z = _dot(xn, wm_ref[...])
    q = z[:, 0 * D_ATT:1 * D_ATT]
    k = z[:, 1 * D_ATT:2 * D_ATT]
    v = z[:, 2 * D_ATT:3 * D_ATT]
    eh = eh_ref[...]
    qn = q * lax.rsqrt(_dot((q * q).astype(bf16), eh) + EPS) * qg_ref[...]
    kn = k * lax.rsqrt(_dot((k * k).astype(bf16), eh) + EPS) * kg_ref[...]
    q_ref[0] = qn.astype(bf16)
    kb_ref[0] = kn.astype(bf16)
    vb_ref[0] = v.astype(bf16)
    if join:
        prev_ref[0] = kn
        prev_ref[1] = v
    else:
        kf_ref[0] = kn
        vf_ref[0] = v
    ga_ref[0] = z[:, 3 * D_ATT:4 * D_ATT].astype(bf16)
    gs_ref[0] = z[:, 4 * D_ATT + D_SSM:].astype(bf16)
    for kb in range(N_SSM_BLOCKS):
        us_ref[kb] = z[:, 4 * D_ATT + kb * LANES:4 * D_ATT + (kb + 1) * LANES]
    for kb in range(N_SSM_BLOCKS):
        for qq in range(S5_Q):
            u_ref[0, kb, qq] = us_ref[kb, pl.ds(qq, x.shape[0] // S5_Q, stride=S5_Q), :]

    zf = _dot(xn, wf_ref[...]) + bf_ref[...]
    lf = jnp.minimum(zf, 0.0) - jnp.log1p(jnp.exp(-jnp.abs(zf)))
    sel = sel_ref[...]
    lft_ref[0] = sum(_dot_nt(sel, part.astype(bf16)) for part in _split3(lf))

    if chain:
        @pl.when(pl.program_id(1) == 0)
        def _():
            carry_ref[...] = c0_ref[0]
        carry = carry_ref[...]
    else:
        carry = jnp.zeros((1, LANES), f32)
    tri = tri_ref[...]
    tb = x.shape[0] // n_blk
    cums = []
    for r in range(n_blk):
        hi, mid, lo = _split3(lf[r * tb:(r + 1) * tb])
        c = _dot(tri, hi.astype(bf16)) + _dot(tri, mid.astype(bf16)) + _dot(tri, lo.astype(bf16)) + carry
        cums.append(c)
        carry = c[tb - 1:tb, :]
        edge_ref[0, r] = jnp.concatenate([c[0:1, :], carry, jnp.zeros((SUBLANES - 2, LANES), f32)], axis=0)
    cum = cums[0] if n_blk == 1 else jnp.concatenate(cums, axis=0)
    if chain:
        carry_ref[...] = carry

    slot = lax.broadcasted_iota(jnp.int32, (1, LANES), 1) % AUG
    hi, mid, lo = _split3(cum * LOG2E)
    qc = jnp.where(slot == 0, hi, jnp.where(slot == 1, mid, jnp.where(slot == 2, lo,
                   jnp.where(slot <= 6, 1.0, 0.0))))
    kc = jnp.where(slot <= 2, 1.0, jnp.where(slot == 3, -hi, jnp.where(slot == 4, -mid,
                   jnp.where(slot == 5, -lo, jnp.where(slot == 6, negm_ref[...],
                             jnp.where(slot <= SHIFT_SLOT + 2, 1.0, 0.0))))))
    qcat_ref[0] = qc.astype(bf16)
    kcat_ref[0] = kc.astype(bf16)


def _inproj(x, c0, params, tile, tri, chain, name, meta_t=None):
    bx, lx, d = x.shape
    n_tiles = lx // tile
    n_blk = tile // tri.shape[0]
    join = 0 if meta_t is None else meta_t[2]
    last = n_tiles - 1
    tj = (lambda j: jnp.minimum(j, last)) if join else (lambda j: j)
    tprev = lambda j: jnp.maximum(j - 1, 0)
    full = lambda a: pl.BlockSpec(a.shape, lambda b, j: (0,) * a.ndim)
    row = lambda w: pl.BlockSpec((1, tile, w), lambda b, j: (b, tj(j), 0))
    sd = jax.ShapeDtypeStruct
    rows = lambda w, t: (row(w), sd((bx, lx, w), t))
    kv = (rows(D_ATT, f32) if not join else
          (pl.BlockSpec((1, D_ATT, tile), lambda b, j: (b, 0, tprev(j))), sd((bx, D_ATT, join + lx), f32)))
    outs = [rows(D_ATT, bf16), rows(D_ATT, bf16), rows(D_ATT, bf16), kv, kv,
            (pl.BlockSpec((1, N_HEADS, tile), lambda b, j: (b, 0, tj(j))), sd((bx, N_HEADS, lx), f32)),
            (pl.BlockSpec((1, n_blk, SUBLANES, LANES), lambda b, j: (b, tj(j), 0, 0)),
             sd((bx, n_tiles * n_blk, SUBLANES, LANES), f32)),
            rows(LANES, bf16), rows(LANES, bf16), rows(D_ATT, bf16),
            (_s5_tokens_spec(tile, lambda b, j: (b, 0, 0, tj(j), 0)), sd(_s5_tokens_shape(bx, lx), f32)),
            rows(D_SSM, bf16)]
    g_in, wm, wf, bfr, qg, kg, eh, negm, sel = params
    operands = (g_in, wm, wf, bfr, qg, kg, eh, tri, negm, sel) + (tuple(meta_t[:2]) if join else ())
    scratch = [pltpu.VMEM((1, LANES), f32), pltpu.VMEM((N_SSM_BLOCKS, tile, LANES), f32)]
    if join:
        scratch += [pltpu.VMEM((2, tile, D_ATT), f32), pltpu.VMEM((2, D_ATT, LANES), f32)]
    return pl.pallas_call(
        functools.partial(_inproj_kernel, chain, n_blk, join),
        grid=(bx, n_tiles + (2 if join else 0)),
        in_specs=[row(d), pl.BlockSpec((1, 1, LANES), lambda b, j: (b, 0, 0))] + [full(a) for a in operands],
        out_specs=[o[0] for o in outs],
        out_shape=[o[1] for o in outs],
        scratch_shapes=scratch,
        compiler_params=_cparams("parallel", "arbitrary"),
        name=name,
    )(x, c0, *operands)


def _attn_kernel(tile, nq, gq, n_meta, jmin_ref, mflag_ref, q_ref, qc_ref, kb_ref, vb_ref, kc_ref,
                 kbm_ref, vbm_ref, kcm_ref, o_ref, acc_ref, bnd_ref):
    b = pl.program_id(0)
    p = pl.program_id(1)
    g = pl.program_id(2)
    lane = lax.broadcasted_iota(jnp.int32, (1, LANES), 1)
    mcol = lax.broadcasted_iota(jnp.int32, (tile, LANES), 1)
    col = lax.broadcasted_iota(jnp.int32, (tile, tile), 1)
    row = lax.broadcasted_iota(jnp.int32, (tile, tile), 0)

    @pl.when(g == 0)
    def _():
        bnd_ref[0] = jnp.where(col <= row, -NEG_INF, NEG_INF)
        bnd_ref[1] = jnp.full((tile, tile), -NEG_INF, f32)
        bnd_ref[2] = jnp.full((tile, tile), NEG_INF, f32)

    def rows_of(j):
        return pl.ds(pl.multiple_of(j * tile, tile), tile)

    n_slots = 2 * gq
    base = (b * N_HEADS + 2 * p) * nq + g * gq

    def table_index(t):
        return base + (t // gq) * nq + t % gq

    def clear(t, carry):
        acc_ref[t] = jnp.zeros(acc_ref.shape[1:], f32)
        return carry

    lax.fori_loop(0, n_slots, clear, 0)

    def head_masks(hh):
        own = lane // HEAD_DIM == hh
        return own.astype(bf16), (~own).astype(bf16), (lane // AUG == 2 * p + hh).astype(bf16)

    def scores(tc, ks, kc):
        m_q, _, m_c = head_masks(tc // gq)
        rows = rows_of(tc % gq)
        q_op = jnp.concatenate([q_ref[0, rows, :] * m_q, qc_ref[0, rows, :] * m_c], axis=1)
        return _dot_nt(q_op, jnp.concatenate([ks, kc], axis=1))

    def pv(tc, pe, vs):
        m_q, m_o, _ = head_masks(tc // gq)
        return _dot(pe, vs * m_q + m_o)

    def item_probs(t, j):
        tc = jnp.minimum(t, n_slots - 1)
        kind = jnp.where(t < n_slots, jnp.where(j == g * gq + tc % gq, 0, 1), 2)
        s = scores(tc, kb_ref[0, rows_of(j), :], kc_ref[0, rows_of(j), :])
        return jnp.exp2(jnp.minimum(s, bnd_ref[kind])).astype(bf16), tc

    def item_pv(pe, tc, j):
        acc_ref[tc] += pv(tc, pe, vb_ref[0, rows_of(j), :])

    def advance(t, j):
        last = jnp.logical_and(t < n_slots, j == g * gq + t % gq)
        t2 = jnp.where(last, t + 1, t)
        j2 = jnp.where(last, jmin_ref[table_index(jnp.minimum(t + 1, n_slots - 1))],
                       jnp.where(t < n_slots, j + 1, j))
        return t2.astype(jnp.int32), j2.astype(jnp.int32)

    def body(trip, carry):
        pe, tp, jp, t, j = carry
        for _ in range(ATT_UNROLL):
            pn, tn = item_probs(t, j)
            item_pv(pe, tp, jp)
            pe, tp, jp = pn, tn, j
            t, j = advance(t, j)
        return pe, tp, jp, t, j

    n_items = sum(g * gq + t % gq - jmin_ref[table_index(t)] + 1 for t in range(n_slots))
    trips = (n_items + ATT_UNROLL - 2) // ATT_UNROLL
    j0 = jmin_ref[base]
    pe0, t0 = item_probs(jnp.int32(0), j0)
    pe, tp, jp, _, _ = lax.fori_loop(0, trips, body, (pe0, t0, j0) + advance(jnp.int32(0), j0))
    item_pv(pe, tp, jp)

    def meta_tile(t, carry):
        @pl.when(mflag_ref[table_index(t)] > 0)
        def _():
            s = jnp.where(mcol < n_meta, scores(t, kbm_ref[...], kcm_ref[...]), NEG_INF)
            acc_ref[t] += pv(t, jnp.exp2(s).astype(bf16), vbm_ref[...])
        return carry

    lax.fori_loop(0, n_slots, meta_tile, 0)

    for r in range(gq):
        outs = []
        for hh in range(2):
            a = acc_ref[hh * gq + r]
            outs.append(a / pltpu.roll(a, HEAD_DIM, axis=1))
        o_ref[0, r * tile:(r + 1) * tile, :] = jnp.where(lane < HEAD_DIM, outs[0], outs[1]).astype(bf16)


def _prompt_attention(q, qcat, kb, vb, kcat, kbm, vbm, kcm, jmin, mflag, n_meta):
    b, l, _ = q.shape
    tile = min(ATT_TILE, l)
    nq = l // tile
    gq = min(ATT_GROUP, nq)
    rows = gq * tile
    grid_spec = pltpu.PrefetchScalarGridSpec(
        num_scalar_prefetch=2,
        grid=(b, N_HEADS // 2, nq // gq),
        in_specs=[
            pl.BlockSpec((1, rows, LANES), lambda b, p, i, *_: (b, i, p)),
            pl.BlockSpec((1, rows, LANES), lambda b, p, i, *_: (b, i, 0)),
            pl.BlockSpec((1, l, LANES), lambda b, p, i, *_: (b, 0, p)),
            pl.BlockSpec((1, l, LANES), lambda b, p, i, *_: (b, 0, p)),
            pl.BlockSpec((1, l, LANES), lambda b, p, i, *_: (b, 0, 0)),
            pl.BlockSpec((LANES, LANES), lambda b, p, i, *_: (0, p)),
            pl.BlockSpec((LANES, LANES), lambda b, p, i, *_: (0, p)),
            pl.BlockSpec((LANES, LANES), lambda b, p, i, *_: (0, 0)),
        ],
        out_specs=pl.BlockSpec((1, rows, LANES), lambda b, p, i, *_: (b, i, p)),
        scratch_shapes=[pltpu.VMEM((2 * gq, tile, LANES), f32), pltpu.VMEM((3, tile, tile), f32)],
    )
    return pl.pallas_call(
        functools.partial(_attn_kernel, tile, nq, gq, n_meta),
        grid_spec=grid_spec,
        out_shape=jax.ShapeDtypeStruct((b, l, D_ATT), bf16),
        compiler_params=_cparams("parallel", "parallel", "arbitrary"),
        name="prompt_attention",
    )(jmin, mflag, q, qcat, kb, vb, kcat, kbm, vbm, kcm)


def _rowshift_kernel(tile, nq, gq, n_meta, jmin_ref, mflag_ref, q_ref, qc_ref, kb_ref, kc_ref, kbm_ref, kcm_ref,
                     o_ref):
    b = pl.program_id(0)
    p = pl.program_id(1)
    g = pl.program_id(2)
    lane = lax.broadcasted_iota(jnp.int32, (1, LANES), 1)
    slot = lane % AUG
    mcol = lax.broadcasted_iota(jnp.int32, (tile, LANES), 1)
    col = lax.broadcasted_iota(jnp.int32, (tile, tile), 1)
    row = lax.broadcasted_iota(jnp.int32, (tile, tile), 0)

    def rows_of(j):
        return pl.ds(pl.multiple_of(j * tile, tile), tile)

    def row_max(x):
        return jnp.max(x, axis=1, keepdims=True)

    for hh in range(2):
        h = 2 * p + hh
        base = (b * N_HEADS + h) * nq + g * gq
        m_q = (lane // HEAD_DIM == hh).astype(bf16)
        mine = lane // AUG == h
        m_c = mine.astype(bf16)

        def per_tile(r, carry):
            i = g * gq + r
            q_op = jnp.concatenate([q_ref[0, rows_of(r), :] * m_q, qc_ref[0, rows_of(r), :] * m_c], axis=1)

            def scores(ks, kc):
                return _dot_nt(q_op, jnp.concatenate([ks, kc], axis=1))

            m = row_max(jnp.where(col <= row, scores(kb_ref[0, rows_of(i), :], kc_ref[0, rows_of(i), :]), NEG_INF))
            m = lax.fori_loop(jmin_ref[base + r], i, lambda j, m: jnp.maximum(
                m, row_max(scores(kb_ref[0, rows_of(j), :], kc_ref[0, rows_of(j), :]))), m)
            mm = row_max(jnp.where(mcol < n_meta, scores(kbm_ref[...], kcm_ref[...]), NEG_INF))
            m = jnp.where(mflag_ref[base + r] > 0, jnp.maximum(m, mm), m)
            hi, mid, lo = _split3(jnp.broadcast_to(-m, (tile, LANES)))
            patch = jnp.where(mine & (slot == SHIFT_SLOT), hi, jnp.where(mine & (slot == SHIFT_SLOT + 1), mid,
                              jnp.where(mine & (slot == SHIFT_SLOT + 2), lo, 0.0))).astype(bf16)
            if hh == 0:
                o_ref[0, 0, rows_of(r), :] = patch
            else:
                o_ref[0, 0, rows_of(r), :] = o_ref[0, 0, rows_of(r), :] + patch
            return carry

        lax.fori_loop(0, gq, per_tile, 0)


def _row_shift(q, qcat, kb, kcat, kbm, kcm, jmin, mflag, n_meta):
    b, l, _ = q.shape
    tile = min(ATT_TILE, l)
    nq = l // tile
    gq = min(ATT_GROUP, nq)
    rows = gq * tile
    grid_spec = pltpu.PrefetchScalarGridSpec(
        num_scalar_prefetch=2,
        grid=(b, N_HEADS // 2, nq // gq),
        in_specs=[
            pl.BlockSpec((1, rows, LANES), lambda b, p, i, *_: (b, i, p)),
            pl.BlockSpec((1, rows, LANES), lambda b, p, i, *_: (b, i, 0)),
            pl.BlockSpec((1, l, LANES), lambda b, p, i, *_: (b, 0, p)),
            pl.BlockSpec((1, l, LANES), lambda b, p, i, *_: (b, 0, 0)),
            pl.BlockSpec((LANES, LANES), lambda b, p, i, *_: (0, p)),
            pl.BlockSpec((LANES, LANES), lambda b, p, i, *_: (0, 0)),
        ],
        out_specs=pl.BlockSpec((1, 1, rows, LANES), lambda b, p, i, *_: (b, p, i, 0)),
    )
    return pl.pallas_call(
        functools.partial(_rowshift_kernel, tile, nq, gq, n_meta),
        grid_spec=grid_spec,
        out_shape=jax.ShapeDtypeStruct((b, N_HEADS // 2, l, LANES), bf16),
        compiler_params=_cparams("parallel", "parallel", "arbitrary"),
        name="row_shift",
    )(jmin, mflag, q, qcat, kb, kcat, kbm, kcm)


def _sattn_kernel(past, tn, q_ref, kn_ref, vn_ref, ckt_ref, cvt_ref, lft_ref, o_ref):
    s_len = lft_ref.shape[-1]
    rows = N_HEADS * tn
    c = lft_ref[0]
    lane_h = lax.broadcasted_iota(jnp.int32, c.shape, 1)
    sh = 1
    while sh < s_len:
        c = c + jnp.where(lane_h >= sh, pltpu.roll(c, sh, axis=1), 0.0)
        sh *= 2
    cexp = jnp.concatenate([jnp.broadcast_to(c[h:h + 1, :], (tn, s_len)) for h in range(N_HEADS)], axis=0)
    rix = lax.broadcasted_iota(jnp.int32, (rows, s_len), 0)
    lix = lax.broadcasted_iota(jnp.int32, (rows, s_len), 1)
    qpos = past + rix % tn
    cq = jnp.sum(jnp.where(lix == qpos, cexp, 0.0), axis=1, keepdims=True)
    pad = jnp.zeros((s_len - past - tn, D_ATT), bf16)
    k_new = jnp.concatenate([kn_ref[0], pad], axis=0)
    v_new = jnp.concatenate([vn_ref[0], pad], axis=0)
    qrep = jnp.concatenate([q_ref[0].astype(f32)] * N_HEADS, axis=0)
    r5 = lax.broadcasted_iota(jnp.int32, (rows, D_ATT), 0)
    l5 = lax.broadcasted_iota(jnp.int32, (rows, D_ATT), 1)
    qrows = jnp.where(l5 // HEAD_DIM == r5 // tn, qrep, 0.0).astype(bf16)
    s = jnp.concatenate([_dot(qrows, ckt_ref[0].astype(bf16)), _dot_nt(qrows, k_new)], axis=1)
    s = s + (cq - cexp) * LOG2E
    s = jnp.where(lix <= qpos, s, NEG_INF)
    m = jnp.max(s, axis=1, keepdims=True)
    pe = jnp.exp2(s - m)
    den = jnp.sum(pe, axis=1, keepdims=True)
    pb = pe.astype(bf16)
    o = (_dot_nt(pb[:, :past], cvt_ref[0].astype(bf16)) + _dot(pb[:, past:], v_new)) / den
    lo = lax.broadcasted_iota(jnp.int32, (tn, D_ATT), 1)
    out = jnp.zeros((tn, D_ATT), f32)
    for h in range(N_HEADS):
        out = out + jnp.where(lo // HEAD_DIM == h, o[h * tn:(h + 1) * tn, :], 0.0)
    o_ref[0] = out.astype(bf16)


def _sample_attention(q, kn, vn, ckt, cvt, lft):
    bs, tn, _ = q.shape
    past = ckt.shape[2]
    s_len = lft.shape[-1]
    blk = lambda n, w: pl.BlockSpec((1, n, w), lambda b: (b, 0, 0))
    return pl.pallas_call(
        functools.partial(_sattn_kernel, past, tn),
        grid=(bs,),
        in_specs=[blk(tn, D_ATT), blk(tn, D_ATT), blk(tn, D_ATT), blk(D_ATT, past), blk(D_ATT, past),
                  blk(N_HEADS, s_len)],
        out_specs=blk(tn, D_ATT),
        out_shape=jax.ShapeDtypeStruct((bs, tn, D_ATT), bf16),
        compiler_params=_cparams("parallel"),
        name="sample_attention",
    )(q, kn, vn, ckt, cvt, lft)


def _s5prep_kernel(n_sq, are_ref, aim_ref, ldt_ref, btr_ref, bti_ref, cre_ref, cim_ref,
                   aqr_ref, aqi_ref, apr_ref, api_ref, wr_ref, wi_ref, ctr_ref, cti_ref, kk_ref):
    are = are_ref[...]
    aim = aim_ref[...]
    dt = jnp.exp(ldt_ref[...])
    mag = jnp.exp(are * dt)
    ph = aim * dt
    abr = mag * jnp.cos(ph)
    abi = mag * jnp.sin(ph)
    nr = abr - 1.0
    den = are * are + aim * aim
    fr = (nr * are + abi * aim) / den
    fi = (abi * are - nr * aim) / den
    pw = [(jnp.ones_like(abr), jnp.zeros_like(abr))]
    for _ in range(S5_Q):
        pr, pi = pw[-1]
        pw.append((pr * abr - pi * abi, pr * abi + pi * abr))
    aqr_ref[...], aqi_ref[...] = pw[S5_Q]
    pr, pi = pw[S5_Q]
    for _ in range(n_sq):
        pr, pi = pr * pr - pi * pi, 2.0 * pr * pi
    apr_ref[...] = pr
    api_ref[...] = pi
    p_states = are.shape[1]
    ident = (lax.broadcasted_iota(jnp.int32, (p_states, p_states), 0)
             == lax.broadcasted_iota(jnp.int32, (p_states, p_states), 1)).astype(bf16)
    lane = lax.broadcasted_iota(jnp.int32, (SSM_GROUP, S5_Q * SSM_GROUP), 1)
    hi = functools.partial(lax.dot_general, dimension_numbers=(((1,), (1,)), ((), ())),
                           precision=lax.Precision.HIGHEST, preferred_element_type=f32)
    for g in range(are.shape[0]):
        row = slice(g, g + 1)
        btr = btr_ref[g]
        bti = bti_ref[g]
        bbr = fr[row] * btr - fi[row] * bti
        bbi = fr[row] * bti + fi[row] * btr
        cre = cre_ref[g]
        cim = cim_ref[g]
        car_all, cai_all = [cre], [cim]
        for q in range(S5_Q):
            er, ei = pw[S5_Q - 1 - q]
            wr_ref[q, g] = er[row] * bbr - ei[row] * bbi
            wi_ref[q, g] = er[row] * bbi + ei[row] * bbr
            gr, gi = pw[q + 1]
            car_all.append(cre * gr[row] - cim * gi[row])
            cai_all.append(cre * gi[row] + cim * gr[row])
        ctr_ref[g] = _dot_nt(ident, jnp.concatenate(car_all[1:], axis=0).astype(bf16))
        cti_ref[g] = _dot_nt(ident, jnp.concatenate(cai_all[1:], axis=0).astype(bf16))
        kt = (hi(bbr, jnp.concatenate(car_all[:S5_Q], axis=0))
              - hi(bbi, jnp.concatenate(cai_all[:S5_Q], axis=0)))
        kk_ref[g] = jnp.concatenate(
            [kt if qi == 0 else jnp.where(lane >= qi * SSM_GROUP, pltpu.roll(kt, qi * SSM_GROUP, axis=1), 0.0)
             for qi in range(S5_Q)], axis=0)


def _s5_prep(a_re, a_im, log_dt, bt_re, bt_im, c_re, c_im, n_sq):
    g, p = a_re.shape
    sd = jax.ShapeDtypeStruct
    qh = S5_Q * SSM_GROUP
    return pl.pallas_call(
        functools.partial(_s5prep_kernel, n_sq),
        out_shape=[sd((g, p), f32)] * 4 + [sd((S5_Q,) + bt_re.shape, f32)] * 2 + [sd((g, p, qh), f32)] * 2
        + [sd((g, qh, qh), f32)],
        name="s5_prep",
    )(a_re, a_im, log_dt, bt_re, bt_im, c_re, c_im)


def _s5_kernel(chain, n_seg, n_sb, u_ref, w8_ref, toep_ref, cpow_ref, aqr_ref, aqi_ref, apr_ref, api_ref, d8_ref,
               x0r_ref, x0i_ref, y_ref, sr_ref, si_ref, xr_ref, xi_ref, cr_ref, ci_ref):
    aqr = aqr_ref[0]
    aqi = aqi_ref[0]
    w8 = w8_ref[0]
    toep = toep_ref[0]
    cpow = cpow_ref[0]
    d8 = d8_ref[0]

    def step(with_y, sb):
        rows = pl.ds(sb, n_seg, stride=n_sb) if n_sb > 1 else pl.ds(0, n_seg)
        uf = jnp.concatenate([u_ref[0, 0, q, rows, :] for q in range(S5_Q)], axis=1)
        ub = uf.astype(bf16)
        xr = xr_ref[...]
        xi = xi_ref[...]
        if with_y:
            xc = jnp.concatenate([xr, xi], axis=1).astype(bf16)
            y = _dot(ub, toep) + _dot(xc, cpow) + d8 * uf
            for q in range(S5_Q):
                y_ref[0, 0, q, rows, :] = y[:, q * LANES:(q + 1) * LANES]
        pj = _dot(ub, w8)
        xr_ref[...] = aqr * xr - aqi * xi + pj[:, :STATE_BLOCK]
        xi_ref[...] = aqr * xi + aqi * xr + pj[:, STATE_BLOCK:]

    if chain:
        xr_ref[...] = jnp.zeros(xr_ref.shape, f32)
        xi_ref[...] = jnp.zeros(xi_ref.shape, f32)
        for sb in range(n_sb):
            step(False, sb)
        apr = apr_ref[0]
        api = api_ref[0]

        def scan(c, carry):
            kr, ki = carry
            cr_ref[pl.ds(c, 1), :] = kr
            ci_ref[pl.ds(c, 1), :] = ki
            er = xr_ref[pl.ds(c, 1), :]
            ei = xi_ref[pl.ds(c, 1), :]
            return apr * kr - api * ki + er, apr * ki + api * kr + ei

        kr, ki = lax.fori_loop(0, n_seg, scan, (x0r_ref[0], x0i_ref[0]))
        sr_ref[0] = kr
        si_ref[0] = ki
        xr_ref[...] = cr_ref[...]
        xi_ref[...] = ci_ref[...]
        for sb in range(n_sb):
            step(True, sb)
    else:
        xr_ref[...] = x0r_ref[0]
        xi_ref[...] = x0i_ref[0]
        for sb in range(n_sb):
            step(True, sb)
        sr_ref[0] = xr_ref[...]
        si_ref[0] = xi_ref[...]


def _s5(u, x0r, x0i, weights, n_seg, chain, name):
    bq, _, _, m, _ = u.shape
    n_sb = m // n_seg
    r0 = x0r.shape[1]
    per_k = lambda a: pl.BlockSpec((1,) + a.shape[1:], lambda b, k: (k, 0, 0))
    st = pl.BlockSpec((1, r0, STATE_BLOCK), lambda b, k: (b, 0, k))
    tok = lambda **kw: pl.BlockSpec((1, 1, S5_Q, m, LANES), lambda b, k: (b, k, 0, 0, 0), **kw)
    sd = jax.ShapeDtypeStruct
    return pl.pallas_call(
        functools.partial(_s5_kernel, chain, n_seg, n_sb),
        grid=(bq, N_SSM_BLOCKS),
        in_specs=[tok(pipeline_mode=pl.Buffered(1))] + [per_k(a) for a in weights] + [st, st],
        out_specs=[tok(), st, st],
        out_shape=[sd(u.shape, f32), sd(x0r.shape, f32), sd(x0i.shape, f32)],
        scratch_shapes=[pltpu.VMEM((n_seg, STATE_BLOCK), f32)] * 4,
        compiler_params=_cparams("parallel", "parallel"),
        name=name,
    )(u, *weights, x0r, x0i)


def _out_kernel(x_ref, att_ref, ga_ref, ys_ref, gs_ref, wg_ref, bg_ref, wo_ref, y_ref, ys_scr):
    ga = ga_ref[0].astype(f32)
    a = att_ref[0].astype(f32) * (ga * jax.nn.sigmoid(ga))
    rows = x_ref.shape[1]
    for kb in range(N_SSM_BLOCKS):
        for qq in range(S5_Q):
            ys_scr[kb, pl.ds(qq, rows // S5_Q, stride=S5_Q), :] = ys_ref[0, kb, qq]
    ys = jnp.concatenate([ys_scr[kb] for kb in range(N_SSM_BLOCKS)], axis=1)
    z = 0.5 * ys * (1.0 + jnp.tanh(0.7978845608028654 * (ys + 0.044715 * (ys * ys * ys))))
    gate = jax.nn.sigmoid(_dot(z.astype(bf16), wg_ref[...]) + bg_ref[...])
    gs = gs_ref[0].astype(f32)
    s5 = z * gate * (gs * jax.nn.sigmoid(gs))
    cat = jnp.concatenate([a, s5], axis=1).astype(bf16)
    y_ref[0] = x_ref[0] + _dot(cat, wo_ref[...])


def _outproj(x, att, ga, ys, gs, wg, bg, wo, tile, name):
    bx, lx, d = x.shape
    full = lambda a: pl.BlockSpec(a.shape, lambda b, j: (0,) * a.ndim)
    row = lambda w: pl.BlockSpec((1, tile, w), lambda b, j: (b, j, 0))
    return pl.pallas_call(
        _out_kernel,
        grid=(bx, lx // tile),
        in_specs=[row(d), row(D_ATT), row(D_ATT), _s5_tokens_spec(tile, lambda b, j: (b, 0, 0, j, 0)), row(D_SSM),
                  full(wg), full(bg), full(wo)],
        out_specs=row(d),
        out_shape=jax.ShapeDtypeStruct(x.shape, f32),
        scratch_shapes=[pltpu.VMEM((N_SSM_BLOCKS, tile, LANES), f32)],
        compiler_params=_cparams("parallel", "parallel"),
        name=name,
    )(x, att, ga, ys, gs, wg, bg, wo)


def _lower_tri(n):
    r = jnp.arange(n)
    return (r[None, :] <= r[:, None]).astype(bf16)


def kernel(x_prompt, x_sample, cache_k, cache_v, cache_logf, state_s5_re, state_s5_im, meta_tokens, norm_g,
           w_in, b_f, q_norm_g, k_norm_g, s5_a_re, s5_a_im, s5_log_dt, s5_b_re, s5_b_im, s5_c_re, s5_c_im,
           s5_d, w_glu, b_glu, w_out):
    assert norm_g.shape[0] == 1, "single-layer step"
    bp, lp, d = x_prompt.shape
    bs, ts, _ = x_sample.shape
    n_meta = meta_tokens.shape[0]
    past = cache_k.shape[2]
    G, P = N_GROUPS, SSM_STATE

    w = w_in[0]
    o3 = 3 * D_ATT
    o4 = o3 + N_HEADS
    wm = jnp.concatenate([w[:, :o3], w[:, o4:]], axis=1).astype(bf16)
    wf = jnp.repeat(w[:, o3:o4], AUG, axis=1).astype(bf16)
    bfr = jnp.repeat(b_f[0], AUG)[None, :].astype(f32)
    g_in = norm_g[0][None, :].astype(f32)
    qg = (jnp.tile(q_norm_g[0], N_HEADS) * (HEAD_DIM ** -0.5 * LOG2E))[None, :].astype(f32)
    kg = jnp.tile(k_norm_g[0], N_HEADS)[None, :].astype(f32)
    hid = jnp.arange(D_ATT) // HEAD_DIM
    eh = ((hid[:, None] == hid[None, :]).astype(f32) / HEAD_DIM).astype(bf16)
    sel = (jnp.arange(LANES)[None, :] == AUG * jnp.arange(N_HEADS)[:, None]).astype(bf16)
    m_bound = 8.0 * jnp.max(jnp.abs(q_norm_g[0])) * jnp.max(jnp.abs(k_norm_g[0])) * 1.02
    robust = m_bound > SHIFT_SAFE
    negm = jnp.full((1, LANES), -LOG2E, f32) * jnp.where(robust, 0.0, m_bound)
    params = (g_in, wm, wf, bfr, qg, kg, eh, negm, sel)

    tm = n_meta
    meta = _inproj(meta_tokens[None].astype(f32), jnp.zeros((1, 1, LANES), f32), params, tm, _lower_tri(tm), True,
                   "inproj_meta")
    (_, kb_m, vb_m, kf_m, vf_m, lft_m, edge_m, _, kcat_m, _, u_m, _) = meta
    cum_meta_end = edge_m[:, -1, 1:2, :]
    c0 = jnp.broadcast_to(cum_meta_end, (bp, 1, LANES))
    tile_p = min(ROW_TILE, lp)
    tb_p = min(LANES, tile_p)
    meta_cols = lambda a: jnp.pad(jnp.swapaxes(a, 1, 2), ((0, 0), (0, 0), (0, LANES - tm)))
    (q_p, kb_p, vb_p, kt_p, vt_p, lft_p, edge_p, qcat_p, kcat_p, ga_p, u_p, gs_p) = _inproj(
        x_prompt, c0, params, tile_p, _lower_tri(tb_p), True, "inproj_prompt",
        meta_t=(meta_cols(kf_m), meta_cols(vf_m), n_meta))
    ls = bs * ts
    tile_s = min(ROW_TILE, ls)
    (q_s, kb_s, vb_s, kf_s, vf_s, lft_s, _, _, _, ga_s, u_s, gs_s) = _inproj(
        x_sample.reshape(1, ls, d), jnp.zeros((1, 1, LANES), f32), params, tile_s,
        _lower_tri(min(LANES, tile_s)), False, "inproj_sample")

    tile_a = min(ATT_TILE, lp)
    nq = lp // tile_a
    per_tile = tile_a // tb_p
    edges = edge_p[:, :, :2, ::AUG]
    q_start = edges[:, 0::per_tile, 0, :]
    k_end = edges[:, per_tile - 1::per_tile, 1, :]
    dmax = q_start[:, :, None, :] - k_end[:, None, :, :]
    padm = lambda a: jnp.pad(a[0], ((0, LANES - tm), (0, 0)))
    kbm, vbm, kcm = padm(kb_m), padm(vb_m), padm(kcat_m)

    def skip_tables(thr):
        jmin = jnp.sum((dmax < -thr).astype(jnp.int32), axis=2)
        jmin = jnp.minimum(jmin, jnp.arange(nq, dtype=jnp.int32)[None, :, None])
        mflag = (q_start - cum_meta_end[:, :, ::AUG] >= -thr).astype(jnp.int32)
        return jnp.transpose(jmin, (0, 2, 1)).reshape(-1), jnp.transpose(mflag, (0, 2, 1)).reshape(-1)

    def attend_bounded():
        return _prompt_attention(q_p, qcat_p, kb_p, vb_p, kcat_p, kbm, vbm, kcm, *skip_tables(EXP_ZERO), n_meta)

    def attend_robust():
        tables = skip_tables(EXP_ZERO + 2.0 * m_bound)
        shift = _row_shift(q_p, qcat_p, kb_p, kcat_p, kbm, kcm, *tables, n_meta)
        qcat_r = (qcat_p.astype(f32) + jnp.sum(shift.astype(f32), axis=1)).astype(bf16)
        return _prompt_attention(q_p, qcat_r, kb_p, vb_p, kcat_p, kbm, vbm, kcm, *tables, n_meta)

    att_p = lax.cond(robust, attend_robust, attend_bounded)

    lft_new = jnp.transpose(lft_s.reshape(N_HEADS, bs, ts), (1, 0, 2))
    s_len = -(-(past + ts) // LANES) * LANES
    lft = jnp.concatenate([jnp.transpose(cache_logf[0].astype(f32), (0, 2, 1)), lft_new,
                           jnp.zeros((bs, N_HEADS, s_len - past - ts), f32)], axis=2)
    cache_t = lambda c: jnp.transpose(c[0], (0, 2, 3, 1)).reshape(bs, D_ATT, past)
    att_s = _sample_attention(q_s.reshape(bs, ts, D_ATT), kb_s.reshape(bs, ts, D_ATT), vb_s.reshape(bs, ts, D_ATT),
                              cache_t(cache_k), cache_t(cache_v), lft)

    n_seg_p = min(PROMPT_SEGMENTS, lp // S5_Q)
    n_sb_p = lp // (S5_Q * n_seg_p)
    n_sq = n_sb_p.bit_length() - 1
    assert n_sb_p == 1 << n_sq and ts % S5_Q == 0
    tr = lambda a: jnp.transpose(a[0], (0, 2, 1)).astype(f32)
    aqr, aqi, apr, api, wr, wi, ctr, cti, kk = _s5_prep(
        s5_a_re[0].astype(f32), s5_a_im[0].astype(f32), s5_log_dt[0][:, None].astype(f32),
        tr(s5_b_re), tr(s5_b_im), s5_c_re[0].astype(f32), s5_c_im[0].astype(f32), n_sq)
    NB, G8, Q = N_SSM_BLOCKS, GROUPS_PER_BLOCK, S5_Q
    qgh = Q * LANES

    def spread(x, col_of_src, row_group, col_group):
        n_src = x.shape[-1]
        n_rows = x.shape[1]
        n_cols = col_of_src.shape[0]
        sel = (col_of_src[None, :] == jnp.arange(n_src)[:, None]).astype(bf16)
        keep = row_group(jnp.arange(n_rows))[:, None] == col_group(jnp.arange(n_cols))[None, :]
        y = jnp.einsum("krs,sc->krc", x.astype(bf16), sel, preferred_element_type=f32)
        return jnp.where(keep[None], y, 0.0).astype(bf16)

    cols_state = jnp.arange(STATE_BLOCK) % P
    cols_qgh = (jnp.arange(qgh) // LANES) * SSM_GROUP + jnp.arange(qgh) % SSM_GROUP
    group_of_qgh = lambda i: (i // SSM_GROUP) % G8
    by_block = lambda a: jnp.transpose(a.reshape(Q, NB, G8 * SSM_GROUP, P), (1, 0, 2, 3)).reshape(NB, qgh, P)
    w_part = lambda a: spread(by_block(a), cols_state, group_of_qgh, lambda i: i // P)
    w8 = jnp.concatenate([w_part(wr), w_part(wi)], axis=2)
    c_part = lambda a: spread(a.reshape(NB, STATE_BLOCK, Q * SSM_GROUP), cols_qgh, lambda i: i // P, group_of_qgh)
    cpow = jnp.concatenate([c_part(ctr), c_part(-cti)], axis=1)
    kk_rows = jnp.transpose(kk.reshape(NB, G8, Q, SSM_GROUP, Q * SSM_GROUP), (0, 2, 1, 3, 4)).reshape(NB, qgh, -1)
    toep = spread(kk_rows, cols_qgh, group_of_qgh, group_of_qgh)
    lane_k = lambda a: a.reshape(NB, 1, -1)
    d8 = jnp.tile(lane_k(s5_d[0].astype(f32)), (1, 1, Q))
    weights = lambda pr_, pi_: (w8, toep, cpow, lane_k(aqr), lane_k(aqi), lane_k(pr_), lane_k(pi_), d8)

    assert tm == ts
    n_small = -(-(bs + 1) // SUBLANES) * SUBLANES
    u_small = jnp.concatenate([u_s, u_m, jnp.zeros(_s5_tokens_shape(1, (n_small - bs - 1) * ts), f32)], axis=3)
    pad_state = lambda a: jnp.pad(a[0].reshape(bs, G * P).astype(f32), ((0, n_small - bs), (0, 0)))[None]
    ys_small, sr_small, si_small = _s5(u_small, pad_state(state_s5_re), pad_state(state_s5_im),
                                       weights(aqr, aqi), n_small, False, "s5_short")
    x0r = jnp.broadcast_to(sr_small[:, bs:bs + 1, :], (bp, 1, G * P))
    x0i = jnp.broadcast_to(si_small[:, bs:bs + 1, :], (bp, 1, G * P))
    ys_p, sr_p, si_p = _s5(u_p, x0r, x0i, weights(apr, api), n_seg_p, True, "s5_prompt")

    wg = w_glu[0].astype(bf16)
    bg = b_glu[0][None, :].astype(f32)
    wo = w_out[0].astype(bf16)
    y_prompt = _outproj(x_prompt, att_p, ga_p, ys_p, gs_p, wg, bg, wo, tile_p, "outproj_prompt")
    y_sample = _outproj(x_sample.reshape(1, ls, d), att_s.reshape(1, ls, D_ATT), ga_s, ys_small[:, :, :, :ls // S5_Q],
                        gs_s, wg, bg, wo, tile_s, "outproj_sample").reshape(bs, ts, d)

    heads_t = lambda a: jnp.transpose(a.reshape(bp, N_HEADS, HEAD_DIM, n_meta + lp), (0, 3, 1, 2))[None]
    new_k_prompt = heads_t(kt_p)
    new_v_prompt = heads_t(vt_p)
    lft_all = jnp.concatenate([jnp.broadcast_to(lft_m, (bp, N_HEADS, n_meta)), lft_p], axis=2)
    new_logf_prompt = jnp.transpose(lft_all, (0, 2, 1))[None]
    new_s5_re_prompt = sr_p.reshape(1, bp, G, P)
    new_s5_im_prompt = si_p.reshape(1, bp, G, P)
    new_k_sample = kf_s.reshape(1, bs, ts, N_HEADS, HEAD_DIM)
    new_v_sample = vf_s.reshape(1, bs, ts, N_HEADS, HEAD_DIM)
    new_logf_sample = jnp.transpose(lft_new, (0, 2, 1))[None]
    new_s5_re_sample = sr_small[:, :bs].reshape(1, bs, G, P)
    new_s5_im_sample = si_small[:, :bs].reshape(1, bs, G, P)
    return (y_prompt, y_sample, new_k_prompt, new_v_prompt, new_logf_prompt, new_s5_re_prompt, new_s5_im_prompt,
            new_k_sample, new_v_sample, new_logf_sample, new_s5_re_sample, new_s5_im_sample)
```

```python
import functools

import jax
import jax.numpy as jnp
from jax import lax
from jax.experimental import pallas as pl
from jax.experimental.pallas import tpu as pltpu

N_HEADS = 8
HEAD_DIM = 64
D_ATT = N_HEADS * HEAD_DIM
SSM_GROUP = 16
SSM_STATE = 64
D_SSM = 512
N_GROUPS = D_SSM // SSM_GROUP
EPS = 1e-6
NEG_INF = -1e30

LANES = 128
SUBLANES = 8
AUG = LANES // N_HEADS
SHIFT_SLOT = 7
GROUPS_PER_BLOCK = LANES // SSM_GROUP
STATE_BLOCK = GROUPS_PER_BLOCK * SSM_STATE
N_SSM_BLOCKS = D_SSM // LANES
VMEM_LIMIT = 56 * 1024 * 1024
EXP_ZERO = 105.0
SHIFT_SAFE = 32.0

ROW_TILE = 512
ATT_TILE = 256
ATT_GROUP = 32
ATT_UNROLL = 32
LOG2E = 1.4426950408889634
PROMPT_SEGMENTS = 512
S5_Q = 8

bf16 = jnp.bfloat16
f32 = jnp.float32


def _cparams(*sem):
    return pltpu.CompilerParams(dimension_semantics=sem, vmem_limit_bytes=VMEM_LIMIT)


def _split3(x):
    hi = x.astype(bf16).astype(f32)
    r1 = x - hi
    mid = r1.astype(bf16).astype(f32)
    lo = (r1 - mid).astype(bf16).astype(f32)
    return hi, mid, lo


def _dot(a, b):
    return jnp.dot(a, b, preferred_element_type=f32)


def _dot_nt(a, b):
    return lax.dot_general(a, b, (((1,), (1,)), ((), ())), preferred_element_type=f32)


def _s5_tokens_shape(b, l):
    return (b, N_SSM_BLOCKS, S5_Q, l // S5_Q, LANES)


def _s5_tokens_spec(rows, index_map, **kw):
    return pl.BlockSpec((1, N_SSM_BLOCKS, S5_Q, rows // S5_Q, LANES), index_map, **kw)


def _inproj_kernel(chain, n_blk, join, *refs):
    n_in = 12
    if not join:
        _inproj_tile(chain, n_blk, False, *refs)
        return
    kmt_ref, vmt_ref = refs[n_in:n_in + 2]
    refs = refs[:n_in] + refs[n_in + 2:]
    kb_ref, kf_ref, vf_ref, kcat_ref = refs[n_in + 1], refs[n_in + 3], refs[n_in + 4], refs[n_in + 8]
    prev_ref, prevc_ref, tails_ref = refs[-3:]
    j = pl.program_id(1)
    n_tiles = pl.num_programs(1) - 2
    lane_t = lax.broadcasted_iota(jnp.int32, (D_ATT, LANES), 1)

    def emit():
        kb_ref[0] = prev_ref[0].T.astype(bf16)
        kcat_ref[0] = prevc_ref[...].T.astype(bf16)
        for idx, t_ref in enumerate((kf_ref, vf_ref)):
            rolled = pltpu.roll(prev_ref[idx].T, join, axis=1)
            first = jnp.where(lane_t < join, tails_ref[idx], rolled[:, :LANES])
            t_ref[0] = first if rolled.shape[1] == LANES else jnp.concatenate([first, rolled[:, LANES:]], axis=1)
            tails_ref[idx] = jnp.where(j > 0, rolled[:, :LANES], tails_ref[idx])

    @pl.when(j == 0)
    def _():
        tails_ref[0] = kmt_ref[0]
        tails_ref[1] = vmt_ref[0]
        prev_ref[...] = jnp.zeros(prev_ref.shape, f32)
        prevc_ref[...] = jnp.zeros(prevc_ref.shape, f32)

    @pl.when(j < n_tiles)
    def _():
        emit()
        _inproj_tile(chain, n_blk, True, *refs)

    pl.when(j == n_tiles)(emit)

    @pl.when(j == n_tiles + 1)
    def _():
        for idx, t_ref in enumerate((kf_ref, vf_ref)):
            t_ref[0] = jnp.zeros(t_ref.shape[1:], f32)
            t_ref[0, :, :LANES] = tails_ref[idx]


def _inproj_tile(chain, n_blk, join, x_ref, c0_ref, g_ref, wm_ref, wf_ref, bf_ref, qg_ref, kg_ref, eh_ref,
                 tri_ref, negm_ref, sel_ref, q_ref, kb_ref, vb_ref, kf_ref, vf_ref, lft_ref, edge_ref,
                 qcat_ref, kcat_ref, ga_ref, u_ref, gs_ref, carry_ref, us_ref, prev_ref=None, prevc_ref=None,
                 tails_ref=None):
    x = x_ref[0]
    ms = jnp.mean(x * x, axis=-1, keepdims=True)
    xn = (x * lax.rsqrt(ms + EPS) * g_ref[...]).astype(bf16)
    z = _dot(xn, wm_ref[...])
    q = z[:, 0 * D_ATT:1 * D_ATT]
    k = z[:, 1 * D_ATT:2 * D_ATT]
    v = z[:, 2 * D_ATT:3 * D_ATT]
    eh = eh_ref[...]
    qn = q * lax.rsqrt(_dot((q * q).astype(bf16), eh) + EPS) * qg_ref[...]
    kn = k * lax.rsqrt(_dot((k * k).astype(bf16), eh) + EPS) * kg_ref[...]
    q_ref[0] = qn.astype(bf16)
    vb_ref[0] = v.astype(bf16)
    if join:
        prev_ref[0] = kn
        prev_ref[1] = v
    else:
        kb_ref[0] = kn.astype(bf16)
        kf_ref[0] = kn
        vf_ref[0] = v
    ga_ref[0] = z[:, 3 * D_ATT:4 * D_ATT].astype(bf16)
    gs_ref[0] = z[:, 4 * D_ATT + D_SSM:].astype(bf16)
    for kb in range(N_SSM_BLOCKS):
        us_ref[kb] = z[:, 4 * D_ATT + kb * LANES:4 * D_ATT + (kb + 1) * LANES]
    for kb in range(N_SSM_BLOCKS):
        for qq in range(S5_Q):
            u_ref[0, kb, qq] = us_ref[kb, pl.ds(qq, x.shape[0] // S5_Q, stride=S5_Q), :]

    zf = _dot(xn, wf_ref[...]) + bf_ref[...]
    lf = jnp.minimum(zf, 0.0) - jnp.log1p(jnp.exp(-jnp.abs(zf)))
    sel = sel_ref[...]
    lft_ref[0] = sum(_dot_nt(sel, part.astype(bf16)) for part in _split3(lf))

    if chain:
        @pl.when(pl.program_id(1) == 0)
        def _():
            carry_ref[...] = c0_ref[0]
        carry = carry_ref[...]
    else:
        carry = jnp.zeros((1, LANES), f32)
    tri = tri_ref[...]
    tb = x.shape[0] // n_blk
    cums = []
    for r in range(n_blk):
        hi, mid, lo = _split3(lf[r * tb:(r + 1) * tb])
        c = _dot(tri, hi.astype(bf16)) + _dot(tri, mid.astype(bf16)) + _dot(tri, lo.astype(bf16)) + carry
        cums.append(c)
        carry = c[tb - 1:tb, :]
        edge_ref[0, r] = jnp.concatenate([c[0:1, :], carry, jnp.zeros((SUBLANES - 2, LANES), f32)], axis=0)
    cum = cums[0] if n_blk == 1 else jnp.concatenate(cums, axis=0)
    if chain:
        carry_ref[...] = carry

    slot = lax.broadcasted_iota(jnp.int32, (1, LANES), 1) % AUG
    hi, mid, lo = _split3(cum * LOG2E)
    qc = jnp.where(slot == 0, hi, jnp.where(slot == 1, mid, jnp.where(slot == 2, lo,
                   jnp.where(slot <= 6, 1.0, 0.0))))
    kc = jnp.where(slot <= 2, 1.0, jnp.where(slot == 3, -hi, jnp.where(slot == 4, -mid,
                   jnp.where(slot == 5, -lo, jnp.where(slot == 6, negm_ref[...],
                             jnp.where(slot <= SHIFT_SLOT + 2, 1.0, 0.0))))))
    qcat_ref[0] = qc.astype(bf16)
    if join:
        prevc_ref[...] = kc
    else:
        kcat_ref[0] = kc.astype(bf16)


def _inproj(x, c0, params, tile, tri, chain, name, meta_t=None):
    bx, lx, d = x.shape
    n_tiles = lx // tile
    n_blk = tile // tri.shape[0]
    join = 0 if meta_t is None else meta_t[2]
    last = n_tiles - 1
    tj = (lambda j: jnp.minimum(j, last)) if join else (lambda j: j)
    tprev = lambda j: jnp.maximum(j - 1, 0)
    full = lambda a: pl.BlockSpec(a.shape, lambda b, j: (0,) * a.ndim)
    row = lambda w: pl.BlockSpec((1, tile, w), lambda b, j: (b, tj(j), 0))
    sd = jax.ShapeDtypeStruct
    rows = lambda w, t: (row(w), sd((bx, lx, w), t))
    kv = (rows(D_ATT, f32) if not join else
          (pl.BlockSpec((1, D_ATT, tile), lambda b, j: (b, 0, tprev(j))), sd((bx, D_ATT, join + lx), f32)))
    cols = lambda w: (pl.BlockSpec((1, w, tile), lambda b, j: (b, 0, jnp.minimum(tprev(j), last))), sd((bx, w, lx), bf16))
    outs = [rows(D_ATT, bf16), cols(D_ATT) if join else rows(D_ATT, bf16), rows(D_ATT, bf16), kv, kv,
            (pl.BlockSpec((1, N_HEADS, tile), lambda b, j: (b, 0, tj(j))), sd((bx, N_HEADS, lx), f32)),
            (pl.BlockSpec((1, n_blk, SUBLANES, LANES), lambda b, j: (b, tj(j), 0, 0)),
             sd((bx, n_tiles * n_blk, SUBLANES, LANES), f32)),
            rows(LANES, bf16), cols(LANES) if join else rows(LANES, bf16), rows(D_ATT, bf16),
            (_s5_tokens_spec(tile, lambda b, j: (b, 0, 0, tj(j), 0)), sd(_s5_tokens_shape(bx, lx), f32)),
            rows(D_SSM, bf16)]
    g_in, wm, wf, bfr, qg, kg, eh, negm, sel = params
    operands = (g_in, wm, wf, bfr, qg, kg, eh, tri, negm, sel) + (tuple(meta_t[:2]) if join else ())
    scratch = [pltpu.VMEM((1, LANES), f32), pltpu.VMEM((N_SSM_BLOCKS, tile, LANES), f32)]
    if join:
        scratch += [pltpu.VMEM((2, tile, D_ATT), f32), pltpu.VMEM((tile, LANES), f32), pltpu.VMEM((2, D_ATT, LANES), f32)]
    return pl.pallas_call(
        functools.partial(_inproj_kernel, chain, n_blk, join),
        grid=(bx, n_tiles + (2 if join else 0)),
        in_specs=[row(d), pl.BlockSpec((1, 1, LANES), lambda b, j: (b, 0, 0))] + [full(a) for a in operands],
        out_specs=[o[0] for o in outs],
        out_shape=[o[1] for o in outs],
        scratch_shapes=scratch,
        compiler_params=_cparams("parallel", "arbitrary"),
        name=name,
    )(x, c0, *operands)


def _attn_kernel(tile, nq, gq, n_meta, jmin_ref, mflag_ref, q_ref, qc_ref, kb_ref, vb_ref, kc_ref,
                 kbm_ref, vbm_ref, kcm_ref, o_ref, acc_ref, bnd_ref):
    b = pl.program_id(0)
    p = pl.program_id(1)
    g = pl.program_id(2)
    lane = lax.broadcasted_iota(jnp.int32, (1, LANES), 1)
    mcol = lax.broadcasted_iota(jnp.int32, (tile, LANES), 1)
    col = lax.broadcasted_iota(jnp.int32, (tile, tile), 1)
    row = lax.broadcasted_iota(jnp.int32, (tile, tile), 0)

    @pl.when(g == 0)
    def _():
        bnd_ref[0] = jnp.where(col <= row, -NEG_INF, NEG_INF)
        bnd_ref[1] = jnp.full((tile, tile), -NEG_INF, f32)
        bnd_ref[2] = jnp.full((tile, tile), NEG_INF, f32)

    def rows_of(j):
        return pl.ds(pl.multiple_of(j * tile, tile), tile)

    n_slots = 2 * gq
    base = (b * N_HEADS + 2 * p) * nq + g * gq

    def table_index(t):
        return base + (t // gq) * nq + t % gq

    def clear(t, carry):
        acc_ref[t] = jnp.zeros(acc_ref.shape[1:], f32)
        return carry

    lax.fori_loop(0, n_slots, clear, 0)

    def head_masks(hh):
        own = lane // HEAD_DIM == hh
        return own.astype(bf16), (~own).astype(bf16), (lane // AUG == 2 * p + hh).astype(bf16)

    def scores(tc, kt, kct):
        m_q, _, m_c = head_masks(tc // gq)
        rows = rows_of(tc % gq)
        q_op = jnp.concatenate([q_ref[0, rows, :] * m_q, qc_ref[0, rows, :] * m_c], axis=1)
        return _dot(q_op, jnp.concatenate([kt, kct], axis=0))

    def pv(tc, pe, vs):
        m_q, m_o, _ = head_masks(tc // gq)
        return _dot(pe, vs * m_q + m_o)

    def item_probs(t, j):
        tc = jnp.minimum(t, n_slots - 1)
        kind = jnp.where(t < n_slots, jnp.where(j == g * gq + tc % gq, 0, 1), 2)
        s = scores(tc, kb_ref[0, :, rows_of(j)], kc_ref[0, :, rows_of(j)])
        return jnp.exp2(jnp.minimum(s, bnd_ref[kind])).astype(bf16), tc

    def item_pv(pe, tc, j):
        acc_ref[tc] += pv(tc, pe, vb_ref[0, rows_of(j), :])

    def advance(t, j):
        last = jnp.logical_and(t < n_slots, j == g * gq + t % gq)
        t2 = jnp.where(last, t + 1, t)
        j2 = jnp.where(last, jmin_ref[table_index(jnp.minimum(t + 1, n_slots - 1))],
                       jnp.where(t < n_slots, j + 1, j))
        return t2.astype(jnp.int32), j2.astype(jnp.int32)

    def body(trip, carry):
        pe, tp, jp, t, j = carry
        for _ in range(ATT_UNROLL):
            pn, tn = item_probs(t, j)
            item_pv(pe, tp, jp)
            pe, tp, jp = pn, tn, j
            t, j = advance(t, j)
        return pe, tp, jp, t, j

    n_items = sum(g * gq + t % gq - jmin_ref[table_index(t)] + 1 for t in range(n_slots))
    trips = (n_items + ATT_UNROLL - 2) // ATT_UNROLL
    j0 = jmin_ref[base]
    pe0, t0 = item_probs(jnp.int32(0), j0)
    pe, tp, jp, _, _ = lax.fori_loop(0, trips, body, (pe0, t0, j0) + advance(jnp.int32(0), j0))
    item_pv(pe, tp, jp)

    def meta_tile(t, carry):
        @pl.when(mflag_ref[table_index(t)] > 0)
        def _():
            s = jnp.where(mcol < n_meta, scores(t, kbm_ref[...], kcm_ref[...]), NEG_INF)
            acc_ref[t] += pv(t, jnp.exp2(s).astype(bf16), vbm_ref[...])
        return carry

    lax.fori_loop(0, n_slots, meta_tile, 0)

    for r in range(gq):
        a0 = acc_ref[r]
        a1 = acc_ref[gq + r]
        own = lane < HEAD_DIM
        sums = pltpu.roll(jnp.where(own, a1, a0), HEAD_DIM, axis=1)
        o_ref[0, r * tile:(r + 1) * tile, :] = (jnp.where(own, a0, a1) / sums).astype(bf16)


def _prompt_attention(q, qcat, kb, vb, kcat, kbm, vbm, kcm, jmin, mflag, n_meta):
    b, l, _ = q.shape
    tile = min(ATT_TILE, l)
    nq = l // tile
    gq = min(ATT_GROUP, nq)
    rows = gq * tile
    grid_spec = pltpu.PrefetchScalarGridSpec(
        num_scalar_prefetch=2,
        grid=(b, N_HEADS // 2, nq // gq),
        in_specs=[
            pl.BlockSpec((1, rows, LANES), lambda b, p, i, *_: (b, i, p)),
            pl.BlockSpec((1, rows, LANES), lambda b, p, i, *_: (b, i, 0)),
            pl.BlockSpec((1, LANES, l), lambda b, p, i, *_: (b, p, 0)),
            pl.BlockSpec((1, l, LANES), lambda b, p, i, *_: (b, 0, p)),
            pl.BlockSpec((1, LANES, l), lambda b, p, i, *_: (b, 0, 0)),
            pl.BlockSpec((LANES, LANES), lambda b, p, i, *_: (p, 0)),
            pl.BlockSpec((LANES, LANES), lambda b, p, i, *_: (0, p)),
            pl.BlockSpec((LANES, LANES), lambda b, p, i, *_: (0, 0)),
        ],
        out_specs=pl.BlockSpec((1, rows, LANES), lambda b, p, i, *_: (b, i, p)),
        scratch_shapes=[pltpu.VMEM((2 * gq, tile, LANES), f32), pltpu.VMEM((3, tile, tile), f32)],
    )
    return pl.pallas_call(
        functools.partial(_attn_kernel, tile, nq, gq, n_meta),
        grid_spec=grid_spec,
        out_shape=jax.ShapeDtypeStruct((b, l, D_ATT), bf16),
        compiler_params=_cparams("parallel", "parallel", "arbitrary"),
        name="prompt_attention",
    )(jmin, mflag, q, qcat, kb, vb, kcat, kbm, vbm, kcm)


def _rowshift_kernel(tile, nq, gq, n_meta, jmin_ref, mflag_ref, q_ref, qc_ref, kb_ref, kc_ref, kbm_ref, kcm_ref,
                     o_ref):
    b = pl.program_id(0)
    p = pl.program_id(1)
    g = pl.program_id(2)
    lane = lax.broadcasted_iota(jnp.int32, (1, LANES), 1)
    slot = lane % AUG
    mcol = lax.broadcasted_iota(jnp.int32, (tile, LANES), 1)
    col = lax.broadcasted_iota(jnp.int32, (tile, tile), 1)
    row = lax.broadcasted_iota(jnp.int32, (tile, tile), 0)

    def rows_of(j):
        return pl.ds(pl.multiple_of(j * tile, tile), tile)

    def row_max(x):
        return jnp.max(x, axis=1, keepdims=True)

    for hh in range(2):
        h = 2 * p + hh
        base = (b * N_HEADS + h) * nq + g * gq
        m_q = (lane // HEAD_DIM == hh).astype(bf16)
        mine = lane // AUG == h
        m_c = mine.astype(bf16)

        def per_tile(r, carry):
            i = g * gq + r
            q_op = jnp.concatenate([q_ref[0, rows_of(r), :] * m_q, qc_ref[0, rows_of(r), :] * m_c], axis=1)

            def scores(kt, kct):
                return _dot(q_op, jnp.concatenate([kt, kct], axis=0))

            m = row_max(jnp.where(col <= row, scores(kb_ref[0, :, rows_of(i)], kc_ref[0, :, rows_of(i)]), NEG_INF))
            m = lax.fori_loop(jmin_ref[base + r], i, lambda j, m: jnp.maximum(
                m, row_max(scores(kb_ref[0, :, rows_of(j)], kc_ref[0, :, rows_of(j)]))), m)
            mm = row_max(jnp.where(mcol < n_meta, scores(kbm_ref[...], kcm_ref[...]), NEG_INF))
            m = jnp.where(mflag_ref[base + r] > 0, jnp.maximum(m, mm), m)
            hi, mid, lo = _split3(jnp.broadcast_to(-m, (tile, LANES)))
            patch = jnp.where(mine & (slot == SHIFT_SLOT), hi, jnp.where(mine & (slot == SHIFT_SLOT + 1), mid,
                              jnp.where(mine & (slot == SHIFT_SLOT + 2), lo, 0.0))).astype(bf16)
            if hh == 0:
                o_ref[0, 0, rows_of(r), :] = patch
            else:
                o_ref[0, 0, rows_of(r), :] = o_ref[0, 0, rows_of(r), :] + patch
            return carry

        lax.fori_loop(0, gq, per_tile, 0)


def _row_shift(q, qcat, kb, kcat, kbm, kcm, jmin, mflag, n_meta):
    b, l, _ = q.shape
    tile = min(ATT_TILE, l)
    nq = l // tile
    gq = min(ATT_GROUP, nq)
    rows = gq * tile
    grid_spec = pltpu.PrefetchScalarGridSpec(
        num_scalar_prefetch=2,
        grid=(b, N_HEADS // 2, nq // gq),
        in_specs=[
            pl.BlockSpec((1, rows, LANES), lambda b, p, i, *_: (b, i, p)),
            pl.BlockSpec((1, rows, LANES), lambda b, p, i, *_: (b, i, 0)),
            pl.BlockSpec((1, LANES, l), lambda b, p, i, *_: (b, p, 0)),
            pl.BlockSpec((1, LANES, l), lambda b, p, i, *_: (b, 0, 0)),
            pl.BlockSpec((LANES, LANES), lambda b, p, i, *_: (p, 0)),
            pl.BlockSpec((LANES, LANES), lambda b, p, i, *_: (0, 0)),
        ],
        out_specs=pl.BlockSpec((1, 1, rows, LANES), lambda b, p, i, *_: (b, p, i, 0)),
    )
    return pl.pallas_call(
        functools.partial(_rowshift_kernel, tile, nq, gq, n_meta),
        grid_spec=grid_spec,
        out_shape=jax.ShapeDtypeStruct((b, N_HEADS // 2, l, LANES), bf16),
        compiler_params=_cparams("parallel", "parallel", "arbitrary"),
        name="row_shift",
    )(jmin, mflag, q, qcat, kb, kcat, kbm, kcm)


def _sattn_kernel(past, tn, q_ref, kn_ref, vn_ref, ckt_ref, cvt_ref, lft_ref, o_ref):
    s_len = lft_ref.shape[-1]
    rows = N_HEADS * tn
    c = lft_ref[0]
    lane_h = lax.broadcasted_iota(jnp.int32, c.shape, 1)
    sh = 1
    while sh < s_len:
        c = c + jnp.where(lane_h >= sh, pltpu.roll(c, sh, axis=1), 0.0)
        sh *= 2
    cexp = jnp.concatenate([jnp.broadcast_to(c[h:h + 1, :], (tn, s_len)) for h in range(N_HEADS)], axis=0)
    rix = lax.broadcasted_iota(jnp.int32, (rows, s_len), 0)
    lix = lax.broadcasted_iota(jnp.int32, (rows, s_len), 1)
    qpos = past + rix % tn
    cq = jnp.sum(jnp.where(lix == qpos, cexp, 0.0), axis=1, keepdims=True)
    pad = jnp.zeros((s_len - past - tn, D_ATT), bf16)
    k_new = jnp.concatenate([kn_ref[0], pad], axis=0)
    v_new = jnp.concatenate([vn_ref[0], pad], axis=0)
    qrep = jnp.concatenate([q_ref[0].astype(f32)] * N_HEADS, axis=0)
    r5 = lax.broadcasted_iota(jnp.int32, (rows, D_ATT), 0)
    l5 = lax.broadcasted_iota(jnp.int32, (rows, D_ATT), 1)
    qrows = jnp.where(l5 // HEAD_DIM == r5 // tn, qrep, 0.0).astype(bf16)
    s = jnp.concatenate([_dot(qrows, ckt_ref[0].astype(bf16)), _dot_nt(qrows, k_new)], axis=1)
    s = s + (cq - cexp) * LOG2E
    s = jnp.where(lix <= qpos, s, NEG_INF)
    m = jnp.max(s, axis=1, keepdims=True)
    pe = jnp.exp2(s - m)
    den = jnp.sum(pe, axis=1, keepdims=True)
    pb = pe.astype(bf16)
    o = (_dot_nt(pb[:, :past], cvt_ref[0].astype(bf16)) + _dot(pb[:, past:], v_new)) / den
    lo = lax.broadcasted_iota(jnp.int32, (tn, D_ATT), 1)
    out = jnp.zeros((tn, D_ATT), f32)
    for h in range(N_HEADS):
        out = out + jnp.where(lo // HEAD_DIM == h, o[h * tn:(h + 1) * tn, :], 0.0)
    o_ref[0] = out.astype(bf16)


def _sample_attention(q, kn, vn, ckt, cvt, lft):
    bs, tn, _ = q.shape
    past = ckt.shape[2]
    s_len = lft.shape[-1]
    blk = lambda n, w: pl.BlockSpec((1, n, w), lambda b: (b, 0, 0))
    return pl.pallas_call(
        functools.partial(_sattn_kernel, past, tn),
        grid=(bs,),
        in_specs=[blk(tn, D_ATT), blk(tn, D_ATT), blk(tn, D_ATT), blk(D_ATT, past), blk(D_ATT, past),
                  blk(N_HEADS, s_len)],
        out_specs=blk(tn, D_ATT),
        out_shape=jax.ShapeDtypeStruct((bs, tn, D_ATT), bf16),
        compiler_params=_cparams("parallel"),
        name="sample_attention",
    )(q, kn, vn, ckt, cvt, lft)


def _s5prep_kernel(n_sq, are_ref, aim_ref, ldt_ref, btr_ref, bti_ref, cre_ref, cim_ref,
                   aqr_ref, aqi_ref, apr_ref, api_ref, wr_ref, wi_ref, ctr_ref, cti_ref, kk_ref):
    are = are_ref[...]
    aim = aim_ref[...]
    dt = jnp.exp(ldt_ref[...])
    mag = jnp.exp(are * dt)
    ph = aim * dt
    abr = mag * jnp.cos(ph)
    abi = mag * jnp.sin(ph)
    nr = abr - 1.0
    den = are * are + aim * aim
    fr = (nr * are + abi * aim) / den
    fi = (abi * are - nr * aim) / den
    pw = [(jnp.ones_like(abr), jnp.zeros_like(abr))]
    for _ in range(S5_Q):
        pr, pi = pw[-1]
        pw.append((pr * abr - pi * abi, pr * abi + pi * abr))
    aqr_ref[...], aqi_ref[...] = pw[S5_Q]
    pr, pi = pw[S5_Q]
    for _ in range(n_sq):
        pr, pi = pr * pr - pi * pi, 2.0 * pr * pi
    apr_ref[...] = pr
    api_ref[...] = pi
    p_states = are.shape[1]
    ident = (lax.broadcasted_iota(jnp.int32, (p_states, p_states), 0)
             == lax.broadcasted_iota(jnp.int32, (p_states, p_states), 1)).astype(bf16)
    lane = lax.broadcasted_iota(jnp.int32, (SSM_GROUP, S5_Q * SSM_GROUP), 1)
    hi = functools.partial(lax.dot_general, dimension_numbers=(((1,), (1,)), ((), ())),
                           precision=lax.Precision.HIGHEST, preferred_element_type=f32)
    for g in range(are.shape[0]):
        row = slice(g, g + 1)
        btr = btr_ref[g]
        bti = bti_ref[g]
        bbr = fr[row] * btr - fi[row] * bti
        bbi = fr[row] * bti + fi[row] * btr
        cre = cre_ref[g]
        cim = cim_ref[g]
        car_all, cai_all = [cre], [cim]
        for q in range(S5_Q):
            er, ei = pw[S5_Q - 1 - q]
            wr_ref[q, g] = er[row] * bbr - ei[row] * bbi
            wi_ref[q, g] = er[row] * bbi + ei[row] * bbr
            gr, gi = pw[q + 1]
            car_all.append(cre * gr[row] - cim * gi[row])
            cai_all.append(cre * gi[row] + cim * gr[row])
        ctr_ref[g] = _dot_nt(ident, jnp.concatenate(car_all[1:], axis=0).astype(bf16))
        cti_ref[g] = _dot_nt(ident, jnp.concatenate(cai_all[1:], axis=0).astype(bf16))
        kt = (hi(bbr, jnp.concatenate(car_all[:S5_Q], axis=0))
              - hi(bbi, jnp.concatenate(cai_all[:S5_Q], axis=0)))
        kk_ref[g] = jnp.concatenate(
            [kt if qi == 0 else jnp.where(lane >= qi * SSM_GROUP, pltpu.roll(kt, qi * SSM_GROUP, axis=1), 0.0)
             for qi in range(S5_Q)], axis=0)


def _s5_prep(a_re, a_im, log_dt, bt_re, bt_im, c_re, c_im, n_sq):
    g, p = a_re.shape
    sd = jax.ShapeDtypeStruct
    qh = S5_Q * SSM_GROUP
    return pl.pallas_call(
        functools.partial(_s5prep_kernel, n_sq),
        out_shape=[sd((g, p), f32)] * 4 + [sd((S5_Q,) + bt_re.shape, f32)] * 2 + [sd((g, p, qh), f32)] * 2
        + [sd((g, qh, qh), f32)],
        name="s5_prep",
    )(a_re, a_im, log_dt, bt_re, bt_im, c_re, c_im)


def _s5_kernel(chain, n_seg, n_sb, u_ref, w8_ref, toep_ref, cpow_ref, aqr_ref, aqi_ref, apr_ref, api_ref, d8_ref,
               x0r_ref, x0i_ref, y_ref, sr_ref, si_ref, xr_ref, xi_ref, cr_ref, ci_ref):
    aqr = aqr_ref[0]
    aqi = aqi_ref[0]
    w8 = w8_ref[0]
    toep = toep_ref[0]
    cpow = cpow_ref[0]
    d8 = d8_ref[0]

    def step(with_y, sb):
        rows = pl.ds(sb, n_seg, stride=n_sb) if n_sb > 1 else pl.ds(0, n_seg)
        uf = jnp.concatenate([u_ref[0, 0, q, rows, :] for q in range(S5_Q)], axis=1)
        ub = uf.astype(bf16)
        xr = xr_ref[...]
        xi = xi_ref[...]
        if with_y:
            xc = jnp.concatenate([xr, xi], axis=1).astype(bf16)
            y = _dot(ub, toep) + _dot(xc, cpow) + d8 * uf
            for q in range(S5_Q):
                y_ref[0, 0, q, rows, :] = y[:, q * LANES:(q + 1) * LANES]
        pj = _dot(ub, w8)
        xr_ref[...] = aqr * xr - aqi * xi + pj[:, :STATE_BLOCK]
        xi_ref[...] = aqr * xi + aqi * xr + pj[:, STATE_BLOCK:]

    if chain:
        xr_ref[...] = jnp.zeros(xr_ref.shape, f32)
        xi_ref[...] = jnp.zeros(xi_ref.shape, f32)
        for sb in range(n_sb):
            step(False, sb)
        apr = apr_ref[0]
        api = api_ref[0]

        def scan(c, carry):
            kr, ki = carry
            cr_ref[pl.ds(c, 1), :] = kr
            ci_ref[pl.ds(c, 1), :] = ki
            er = xr_ref[pl.ds(c, 1), :]
            ei = xi_ref[pl.ds(c, 1), :]
            return apr * kr - api * ki + er, apr * ki + api * kr + ei

        kr, ki = lax.fori_loop(0, n_seg, scan, (x0r_ref[0], x0i_ref[0]))
        sr_ref[0] = kr
        si_ref[0] = ki
        xr_ref[...] = cr_ref[...]
        xi_ref[...] = ci_ref[...]
        for sb in range(n_sb):
            step(True, sb)
    else:
        xr_ref[...] = x0r_ref[0]
        xi_ref[...] = x0i_ref[0]
        for sb in range(n_sb):
            step(True, sb)
        sr_ref[0] = xr_ref[...]
        si_ref[0] = xi_ref[...]


def _s5(u, x0r, x0i, weights, n_seg, chain, name):
    bq, _, _, m, _ = u.shape
    n_sb = m // n_seg
    r0 = x0r.shape[1]
    per_k = lambda a: pl.BlockSpec((1,) + a.shape[1:], lambda b, k: (k, 0, 0))
    st = pl.BlockSpec((1, r0, STATE_BLOCK), lambda b, k: (b, 0, k))
    tok = lambda **kw: pl.BlockSpec((1, 1, S5_Q, m, LANES), lambda b, k: (b, k, 0, 0, 0), **kw)
    sd = jax.ShapeDtypeStruct
    return pl.pallas_call(
        functools.partial(_s5_kernel, chain, n_seg, n_sb),
        grid=(bq, N_SSM_BLOCKS),
        in_specs=[tok(pipeline_mode=pl.Buffered(1))] + [per_k(a) for a in weights] + [st, st],
        out_specs=[tok(), st, st],
        out_shape=[sd(u.shape, f32), sd(x0r.shape, f32), sd(x0i.shape, f32)],
        scratch_shapes=[pltpu.VMEM((n_seg, STATE_BLOCK), f32)] * 4,
        compiler_params=_cparams("parallel", "parallel"),
        name=name,
    )(u, *weights, x0r, x0i)


def _out_kernel(x_ref, att_ref, ga_ref, ys_ref, gs_ref, wg_ref, bg_ref, wo_ref, y_ref, ys_scr):
    ga = ga_ref[0].astype(f32)
    a = att_ref[0].astype(f32) * (ga * jax.nn.sigmoid(ga))
    rows = x_ref.shape[1]
    for kb in range(N_SSM_BLOCKS):
        for qq in range(S5_Q):
            ys_scr[kb, pl.ds(qq, rows // S5_Q, stride=S5_Q), :] = ys_ref[0, kb, qq]
    ys = jnp.concatenate([ys_scr[kb] for kb in range(N_SSM_BLOCKS)], axis=1)
    z = 0.5 * ys * (1.0 + jnp.tanh(0.7978845608028654 * (ys + 0.044715 * (ys * ys * ys))))
    gate = jax.nn.sigmoid(_dot(z.astype(bf16), wg_ref[...]) + bg_ref[...])
    gs = gs_ref[0].astype(f32)
    s5 = z * gate * (gs * jax.nn.sigmoid(gs))
    cat = jnp.concatenate([a, s5], axis=1).astype(bf16)
    y_ref[0] = x_ref[0] + _dot(cat, wo_ref[...])


def _outproj(x, att, ga, ys, gs, wg, bg, wo, tile, name):
    bx, lx, d = x.shape
    full = lambda a: pl.BlockSpec(a.shape, lambda b, j: (0,) * a.ndim)
    row = lambda w: pl.BlockSpec((1, tile, w), lambda b, j: (b, j, 0))
    return pl.pallas_call(
        _out_kernel,
        grid=(bx, lx // tile),
        in_specs=[row(d), row(D_ATT), row(D_ATT), _s5_tokens_spec(tile, lambda b, j: (b, 0, 0, j, 0)), row(D_SSM),
                  full(wg), full(bg), full(wo)],
        out_specs=row(d),
        out_shape=jax.ShapeDtypeStruct(x.shape, f32),
        scratch_shapes=[pltpu.VMEM((N_SSM_BLOCKS, tile, LANES), f32)],
        compiler_params=_cparams("parallel", "parallel"),
        name=name,
    )(x, att, ga, ys, gs, wg, bg, wo)


def _lower_tri(n):
    r = jnp.arange(n)
    return (r[None, :] <= r[:, None]).astype(bf16)


def kernel(x_prompt, x_sample, cache_k, cache_v, cache_logf, state_s5_re, state_s5_im, meta_tokens, norm_g,
           w_in, b_f, q_norm_g, k_norm_g, s5_a_re, s5_a_im, s5_log_dt, s5_b_re, s5_b_im, s5_c_re, s5_c_im,
           s5_d, w_glu, b_glu, w_out):
    assert norm_g.shape[0] == 1, "single-layer step"
    bp, lp, d = x_prompt.shape
    bs, ts, _ = x_sample.shape
    n_meta = meta_tokens.shape[0]
    past = cache_k.shape[2]
    G, P = N_GROUPS, SSM_STATE

    w = w_in[0]
    o3 = 3 * D_ATT
    o4 = o3 + N_HEADS
    wm = jnp.concatenate([w[:, :o3], w[:, o4:]], axis=1).astype(bf16)
    wf = jnp.repeat(w[:, o3:o4], AUG, axis=1).astype(bf16)
    bfr = jnp.repeat(b_f[0], AUG)[None, :].astype(f32)
    g_in = norm_g[0][None, :].astype(f32)
    qg = (jnp.tile(q_norm_g[0], N_HEADS) * (HEAD_DIM ** -0.5 * LOG2E))[None, :].astype(f32)
    kg = jnp.tile(k_norm_g[0], N_HEADS)[None, :].astype(f32)
    hid = jnp.arange(D_ATT) // HEAD_DIM
    eh = ((hid[:, None] == hid[None, :]).astype(f32) / HEAD_DIM).astype(bf16)
    sel = (jnp.arange(LANES)[None, :] == AUG * jnp.arange(N_HEADS)[:, None]).astype(bf16)
    m_bound = 8.0 * jnp.max(jnp.abs(q_norm_g[0])) * jnp.max(jnp.abs(k_norm_g[0])) * 1.02
    robust = m_bound > SHIFT_SAFE
    negm = jnp.full((1, LANES), -LOG2E, f32) * jnp.where(robust, 0.0, m_bound)
    params = (g_in, wm, wf, bfr, qg, kg, eh, negm, sel)

    tm = n_meta
    meta = _inproj(meta_tokens[None].astype(f32), jnp.zeros((1, 1, LANES), f32), params, tm, _lower_tri(tm), True,
                   "inproj_meta")
    (_, kb_m, vb_m, kf_m, vf_m, lft_m, edge_m, _, kcat_m, _, u_m, _) = meta
    cum_meta_end = edge_m[:, -1, 1:2, :]
    c0 = jnp.broadcast_to(cum_meta_end, (bp, 1, LANES))
    tile_p = min(ROW_TILE, lp)
    tb_p = min(LANES, tile_p)
    meta_cols = lambda a: jnp.pad(jnp.swapaxes(a, 1, 2), ((0, 0), (0, 0), (0, LANES - tm)))
    (q_p, kb_p, vb_p, kt_p, vt_p, lft_p, edge_p, qcat_p, kcat_p, ga_p, u_p, gs_p) = _inproj(
        x_prompt, c0, params, tile_p, _lower_tri(tb_p), True, "inproj_prompt",
        meta_t=(meta_cols(kf_m), meta_cols(vf_m), n_meta))
    ls = bs * ts
    tile_s = min(ROW_TILE, ls)
    (q_s, kb_s, vb_s, kf_s, vf_s, lft_s, _, _, _, ga_s, u_s, gs_s) = _inproj(
        x_sample.reshape(1, ls, d), jnp.zeros((1, 1, LANES), f32), params, tile_s,
        _lower_tri(min(LANES, tile_s)), False, "inproj_sample")

    tile_a = min(ATT_TILE, lp)
    nq = lp // tile_a
    per_tile = tile_a // tb_p
    edges = edge_p[:, :, :2, ::AUG]
    q_start = edges[:, 0::per_tile, 0, :]
    k_end = edges[:, per_tile - 1::per_tile, 1, :]
    dmax = q_start[:, :, None, :] - k_end[:, None, :, :]
    padm = lambda a: jnp.pad(a[0], ((0, LANES - tm), (0, 0)))
    kbm, vbm, kcm = padm(kb_m).T, padm(vb_m), padm(kcat_m).T

    def skip_tables(thr):
        jmin = jnp.sum((dmax < -thr).astype(jnp.int32), axis=2)
        jmin = jnp.minimum(jmin, jnp.arange(nq, dtype=jnp.int32)[None, :, None])
        mflag = (q_start - cum_meta_end[:, :, ::AUG] >= -thr).astype(jnp.int32)
        return jnp.transpose(jmin, (0, 2, 1)).reshape(-1), jnp.transpose(mflag, (0, 2, 1)).reshape(-1)

    def attend_bounded():
        return _prompt_attention(q_p, qcat_p, kb_p, vb_p, kcat_p, kbm, vbm, kcm, *skip_tables(EXP_ZERO), n_meta)

    def attend_robust():
        tables = skip_tables(EXP_ZERO + 2.0 * m_bound)
        shift = _row_shift(q_p, qcat_p, kb_p, kcat_p, kbm, kcm, *tables, n_meta)
        qcat_r = (qcat_p.astype(f32) + jnp.sum(shift.astype(f32), axis=1)).astype(bf16)
        return _prompt_attention(q_p, qcat_r, kb_p, vb_p, kcat_p, kbm, vbm, kcm, *tables, n_meta)

    att_p = lax.cond(robust, attend_robust, attend_bounded)

    lft_new = jnp.transpose(lft_s.reshape(N_HEADS, bs, ts), (1, 0, 2))
    s_len = -(-(past + ts) // LANES) * LANES
    lft = jnp.concatenate([jnp.transpose(cache_logf[0].astype(f32), (0, 2, 1)), lft_new,
                           jnp.zeros((bs, N_HEADS, s_len - past - ts), f32)], axis=2)
    cache_t = lambda c: jnp.transpose(c[0], (0, 2, 3, 1)).reshape(bs, D_ATT, past)
    att_s = _sample_attention(q_s.reshape(bs, ts, D_ATT), kb_s.reshape(bs, ts, D_ATT), vb_s.reshape(bs, ts, D_ATT),
                              cache_t(cache_k), cache_t(cache_v), lft)

    n_seg_p = min(PROMPT_SEGMENTS, lp // S5_Q)
    n_sb_p = lp // (S5_Q * n_seg_p)
    n_sq = n_sb_p.bit_length() - 1
    assert n_sb_p == 1 << n_sq and ts % S5_Q == 0
    tr = lambda a: jnp.transpose(a[0], (0, 2, 1)).astype(f32)
    aqr, aqi, apr, api, wr, wi, ctr, cti, kk = _s5_prep(
        s5_a_re[0].astype(f32), s5_a_im[0].astype(f32), s5_log_dt[0][:, None].astype(f32),
        tr(s5_b_re), tr(s5_b_im), s5_c_re[0].astype(f32), s5_c_im[0].astype(f32), n_sq)
    NB, G8, Q = N_SSM_BLOCKS, GROUPS_PER_BLOCK, S5_Q
    qgh = Q * LANES

    def spread(x, col_of_src, row_group, col_group):
        n_src = x.shape[-1]
        n_rows = x.shape[1]
        n_cols = col_of_src.shape[0]
        sel = (col_of_src[None, :] == jnp.arange(n_src)[:, None]).astype(bf16)
        keep = row_group(jnp.arange(n_rows))[:, None] == col_group(jnp.arange(n_cols))[None, :]
        y = jnp.einsum("krs,sc->krc", x.astype(bf16), sel, preferred_element_type=f32)
        return jnp.where(keep[None], y, 0.0).astype(bf16)

    cols_state = jnp.arange(STATE_BLOCK) % P
    cols_qgh = (jnp.arange(qgh) // LANES) * SSM_GROUP + jnp.arange(qgh) % SSM_GROUP
    group_of_qgh = lambda i: (i // SSM_GROUP) % G8
    by_block = lambda a: jnp.transpose(a.reshape(Q, NB, G8 * SSM_GROUP, P), (1, 0, 2, 3)).reshape(NB, qgh, P)
    w_part = lambda a: spread(by_block(a), cols_state, group_of_qgh, lambda i: i // P)
    w8 = jnp.concatenate([w_part(wr), w_part(wi)], axis=2)
    c_part = lambda a: spread(a.reshape(NB, STATE_BLOCK, Q * SSM_GROUP), cols_qgh, lambda i: i // P, group_of_qgh)
    cpow = jnp.concatenate([c_part(ctr), c_part(-cti)], axis=1)
    kk_rows = jnp.transpose(kk.reshape(NB, G8, Q, SSM_GROUP, Q * SSM_GROUP), (0, 2, 1, 3, 4)).reshape(NB, qgh, -1)
    toep = spread(kk_rows, cols_qgh, group_of_qgh, group_of_qgh)
    lane_k = lambda a: a.reshape(NB, 1, -1)
    d8 = jnp.tile(lane_k(s5_d[0].astype(f32)), (1, 1, Q))
    weights = lambda pr_, pi_: (w8, toep, cpow, lane_k(aqr), lane_k(aqi), lane_k(pr_), lane_k(pi_), d8)

    assert tm == ts
    n_small = -(-(bs + 1) // SUBLANES) * SUBLANES
    u_small = jnp.concatenate([u_s, u_m, jnp.zeros(_s5_tokens_shape(1, (n_small - bs - 1) * ts), f32)], axis=3)
    pad_state = lambda a: jnp.pad(a[0].reshape(bs, G * P).astype(f32), ((0, n_small - bs), (0, 0)))[None]
    ys_small, sr_small, si_small = _s5(u_small, pad_state(state_s5_re), pad_state(state_s5_im),
                                       weights(aqr, aqi), n_small, False, "s5_short")
    x0r = jnp.broadcast_to(sr_small[:, bs:bs + 1, :], (bp, 1, G * P))
    x0i = jnp.broadcast_to(si_small[:, bs:bs + 1, :], (bp, 1, G * P))
    ys_p, sr_p, si_p = _s5(u_p, x0r, x0i, weights(apr, api), n_seg_p, True, "s5_prompt")

    wg = w_glu[0].astype(bf16)
    bg = b_glu[0][None, :].astype(f32)
    wo = w_out[0].astype(bf16)
    y_prompt = _outproj(x_prompt, att_p, ga_p, ys_p, gs_p, wg, bg, wo, tile_p, "outproj_prompt")
    y_sample = _outproj(x_sample.reshape(1, ls, d), att_s.reshape(1, ls, D_ATT), ga_s, ys_small[:, :, :, :ls // S5_Q],
                        gs_s, wg, bg, wo, tile_s, "outproj_sample").reshape(bs, ts, d)

    heads_t = lambda a: jnp.transpose(a.reshape(bp, N_HEADS, HEAD_DIM, n_meta + lp), (0, 3, 1, 2))[None]
    new_k_prompt = heads_t(kt_p)
    new_v_prompt = heads_t(vt_p)
    lft_all = jnp.concatenate([jnp.broadcast_to(lft_m, (bp, N_HEADS, n_meta)), lft_p], axis=2)
    new_logf_prompt = jnp.transpose(lft_all, (0, 2, 1))[None]
    new_s5_re_prompt = sr_p.reshape(1, bp, G, P)
    new_s5_im_prompt = si_p.reshape(1, bp, G, P)
    new_k_sample = kf_s.reshape(1, bs, ts, N_HEADS, HEAD_DIM)
    new_v_sample = vf_s.reshape(1, bs, ts, N_HEADS, HEAD_DIM)
    new_logf_sample = jnp.transpose(lft_new, (0, 2, 1))[None]
    new_s5_re_sample = sr_small[:, :bs].reshape(1, bs, G, P)
    new_s5_im_sample = si_small[:, :bs].reshape(1, bs, G, P)
    return (y_prompt, y_sample, new_k_prompt, new_v_prompt, new_logf_prompt, new_s5_re_prompt, new_s5_im_prompt,
            new_k_sample, new_v_sample, new_logf_sample, new_s5_re_sample, new_s5_im_sample)
```

```python
import functools

import jax
import jax.numpy as jnp
from jax import lax
from jax.experimental import pallas as pl
from jax.experimental.pallas import tpu as pltpu

N_HEADS = 8
HEAD_DIM = 64
D_ATT = N_HEADS * HEAD_DIM
SSM_GROUP = 16
SSM_STATE = 64
D_SSM = 512
N_GROUPS = D_SSM // SSM_GROUP
EPS = 1e-6
NEG_INF = -1e30

LANES = 128
SUBLANES = 8
AUG = LANES // N_HEADS
SHIFT_SLOT = 7
GROUPS_PER_BLOCK = LANES // SSM_GROUP
STATE_BLOCK = GROUPS_PER_BLOCK * SSM_STATE
N_SSM_BLOCKS = D_SSM // LANES
VMEM_LIMIT = 56 * 1024 * 1024
EXP_ZERO = 105.0
SHIFT_SAFE = 32.0

ROW_TILE = 512
ATT_TILE = 256
ATT_GROUP = 32
ATT_UNROLL = 32
LOG2E = 1.4426950408889634
PROMPT_SEGMENTS = 512
S5_Q = 8

bf16 = jnp.bfloat16
f32 = jnp.float32


def _cparams(*sem):
    return pltpu.CompilerParams(dimension_semantics=sem, vmem_limit_bytes=VMEM_LIMIT)


def _split3(x):
    hi = x.astype(bf16).astype(f32)
    r1 = x - hi
    mid = r1.astype(bf16).astype(f32)
    lo = (r1 - mid).astype(bf16).astype(f32)
    return hi, mid, lo


def _dot(a, b):
    return jnp.dot(a, b, preferred_element_type=f32)


def _dot_nt(a, b):
    return lax.dot_general(a, b, (((1,), (1,)), ((), ())), preferred_element_type=f32)


def _s5_tokens_shape(b, l):
    return (b, N_SSM_BLOCKS, S5_Q, l // S5_Q, LANES)


def _s5_tokens_spec(rows, index_map, **kw):
    return pl.BlockSpec((1, N_SSM_BLOCKS, S5_Q, rows // S5_Q, LANES), index_map, **kw)


def _inproj_kernel(chain, n_blk, join, *refs):
    n_in = 12
    if chain:
        c0_ref, carry_ref = refs[1], refs[-5 if join else -2]

        @pl.when(pl.program_id(1) == 0)
        def _():
            carry_ref[...] = c0_ref[0]
    if not join:
        _inproj_tile(chain, n_blk, False, *refs)
        return
    kmt_ref, vmt_ref = refs[n_in:n_in + 2]
    refs = refs[:n_in] + refs[n_in + 2:]
    kb_ref, kf_ref, vf_ref, kcat_ref = refs[n_in + 1], refs[n_in + 3], refs[n_in + 4], refs[n_in + 8]
    prev_ref, prevc_ref, tails_ref = refs[-3:]
    j = pl.program_id(1)
    n_tiles = pl.num_programs(1) - 2
    lane_t = lax.broadcasted_iota(jnp.int32, (D_ATT, LANES), 1)

    def emit():
        kb_ref[0] = prev_ref[0].T.astype(bf16)
        kcat_ref[0] = prevc_ref[...].T.astype(bf16)
        for idx, t_ref in enumerate((kf_ref, vf_ref)):
            rolled = pltpu.roll(prev_ref[idx].T, join, axis=1)
            first = jnp.where(lane_t < join, tails_ref[idx], rolled[:, :LANES])
            t_ref[0] = first if rolled.shape[1] == LANES else jnp.concatenate([first, rolled[:, LANES:]], axis=1)
            tails_ref[idx] = jnp.where(j > 0, rolled[:, :LANES], tails_ref[idx])

    @pl.when(j == 0)
    def _():
        tails_ref[0] = kmt_ref[0]
        tails_ref[1] = vmt_ref[0]
        prev_ref[...] = jnp.zeros(prev_ref.shape, f32)
        prevc_ref[...] = jnp.zeros(prevc_ref.shape, f32)

    @pl.when(j < n_tiles)
    def _():
        emit()
        _inproj_tile(chain, n_blk, True, *refs)

    pl.when(j == n_tiles)(emit)

    @pl.when(j == n_tiles + 1)
    def _():
        for idx, t_ref in enumerate((kf_ref, vf_ref)):
            t_ref[0] = jnp.zeros(t_ref.shape[1:], f32)
            t_ref[0, :, :LANES] = tails_ref[idx]


def _inproj_tile(chain, n_blk, join, x_ref, c0_ref, g_ref, wm_ref, wf_ref, bf_ref, qg_ref, kg_ref, eh_ref,
                 tri_ref, negm_ref, sel_ref, q_ref, kb_ref, vb_ref, kf_ref, vf_ref, lft_ref, edge_ref,
                 qcat_ref, kcat_ref, ga_ref, u_ref, gs_ref, carry_ref, us_ref, prev_ref=None, prevc_ref=None,
                 tails_ref=None):
    x = x_ref[0]
    ms = jnp.mean(x * x, axis=-1, keepdims=True)
    xn = (x * lax.rsqrt(ms + EPS) * g_ref[...]).astype(bf16)
    z = _dot(xn, wm_ref[...])
    q = z[:, 0 * D_ATT:1 * D_ATT]
    k = z[:, 1 * D_ATT:2 * D_ATT]
    v = z[:, 2 * D_ATT:3 * D_ATT]
    eh = eh_ref[...]
    qn = q * lax.rsqrt(_dot((q * q).astype(bf16), eh) + EPS) * qg_ref[...]
    kn = k * lax.rsqrt(_dot((k * k).astype(bf16), eh) + EPS) * kg_ref[...]
    q_ref[0] = qn.astype(bf16)
    vb_ref[0] = v.astype(bf16)
    if join:
        prev_ref[0] = kn
        prev_ref[1] = v
    else:
        kb_ref[0] = kn.astype(bf16)
        kf_ref[0] = kn
        vf_ref[0] = v
    ga_ref[0] = z[:, 3 * D_ATT:4 * D_ATT].astype(bf16)
    gs_ref[0] = z[:, 4 * D_ATT + D_SSM:].astype(bf16)
    for kb in range(N_SSM_BLOCKS):
        us_ref[kb] = z[:, 4 * D_ATT + kb * LANES:4 * D_ATT + (kb + 1) * LANES]
    for kb in range(N_SSM_BLOCKS):
        for qq in range(S5_Q):
            u_ref[0, kb, qq] = us_ref[kb, pl.ds(qq, x.shape[0] // S5_Q, stride=S5_Q), :]

    zf = _dot(xn, wf_ref[...]) + bf_ref[...]
    lf = jnp.minimum(zf, 0.0) - jnp.log1p(jnp.exp(-jnp.abs(zf)))
    sel = sel_ref[...]
    lft_ref[0] = sum(_dot_nt(sel, part.astype(bf16)) for part in _split3(lf))

    carry = carry_ref[...] if chain else jnp.zeros((1, LANES), f32)
    tri = tri_ref[...]
    tb = x.shape[0] // n_blk
    cums = []
    for r in range(n_blk):
        hi, mid, lo = _split3(lf[r * tb:(r + 1) * tb])
        c = _dot(tri, hi.astype(bf16)) + _dot(tri, mid.astype(bf16)) + _dot(tri, lo.astype(bf16)) + carry
        cums.append(c)
        carry = c[tb - 1:tb, :]
        edge_ref[0, r] = jnp.concatenate([c[0:1, :], carry, jnp.zeros((SUBLANES - 2, LANES), f32)], axis=0)
    cum = cums[0] if n_blk == 1 else jnp.concatenate(cums, axis=0)
    if chain:
        carry_ref[...] = carry

    slot = lax.broadcasted_iota(jnp.int32, (1, LANES), 1) % AUG
    hi, mid, lo = _split3(cum * LOG2E)
    qc = jnp.where(slot == 0, hi, jnp.where(slot == 1, mid, jnp.where(slot == 2, lo,
                   jnp.where(slot <= 6, 1.0, 0.0))))
    kc = jnp.where(slot <= 2, 1.0, jnp.where(slot == 3, -hi, jnp.where(slot == 4, -mid,
                   jnp.where(slot == 5, -lo, jnp.where(slot == 6, negm_ref[...],
                             jnp.where(slot <= SHIFT_SLOT + 2, 1.0, 0.0))))))
    qcat_ref[0] = qc.astype(bf16)
    if join:
        prevc_ref[...] = kc
    else:
        kcat_ref[0] = kc.astype(bf16)


def _inproj(x, c0, params, tile, tri, chain, name, meta_t=None):
    bx, lx, d = x.shape
    n_tiles = lx // tile
    n_blk = tile // tri.shape[0]
    join = 0 if meta_t is None else meta_t[2]
    last = n_tiles - 1
    tj = (lambda j: jnp.minimum(j, last)) if join else (lambda j: j)
    tprev = lambda j: jnp.maximum(j - 1, 0)
    full = lambda a: pl.BlockSpec(a.shape, lambda b, j: (0,) * a.ndim)
    row = lambda w: pl.BlockSpec((1, tile, w), lambda b, j: (b, tj(j), 0))
    sd = jax.ShapeDtypeStruct
    rows = lambda w, t: (row(w), sd((bx, lx, w), t))
    kv = (rows(D_ATT, f32) if not join else
          (pl.BlockSpec((1, D_ATT, tile), lambda b, j: (b, 0, tprev(j))), sd((bx, D_ATT, join + lx), f32)))
    cols = lambda w: (pl.BlockSpec((1, w, tile), lambda b, j: (b, 0, jnp.minimum(tprev(j), last))), sd((bx, w, lx), bf16))
    outs = [rows(D_ATT, bf16), cols(D_ATT) if join else rows(D_ATT, bf16), rows(D_ATT, bf16), kv, kv,
            (pl.BlockSpec((1, N_HEADS, tile), lambda b, j: (b, 0, tj(j))), sd((bx, N_HEADS, lx), f32)),
            (pl.BlockSpec((1, n_blk, SUBLANES, LANES), lambda b, j: (b, tj(j), 0, 0)),
             sd((bx, n_tiles * n_blk, SUBLANES, LANES), f32)),
            rows(LANES, bf16), cols(LANES) if join else rows(LANES, bf16), rows(D_ATT, bf16),
            (_s5_tokens_spec(tile, lambda b, j: (b, 0, 0, tj(j), 0)), sd(_s5_tokens_shape(bx, lx), f32)),
            rows(D_SSM, bf16)]
    g_in, wm, wf, bfr, qg, kg, eh, negm, sel = params
    operands = (g_in, wm, wf, bfr, qg, kg, eh, tri, negm, sel) + (tuple(meta_t[:2]) if join else ())
    scratch = [pltpu.VMEM((1, LANES), f32), pltpu.VMEM((N_SSM_BLOCKS, tile, LANES), f32)]
    if join:
        scratch += [pltpu.VMEM((2, tile, D_ATT), f32), pltpu.VMEM((tile, LANES), f32), pltpu.VMEM((2, D_ATT, LANES), f32)]
    return pl.pallas_call(
        functools.partial(_inproj_kernel, chain, n_blk, join),
        grid=(bx, n_tiles + (2 if join else 0)),
        in_specs=[row(d), pl.BlockSpec((1, 1, LANES), lambda b, j: (b, 0, 0))] + [full(a) for a in operands],
        out_specs=[o[0] for o in outs],
        out_shape=[o[1] for o in outs],
        scratch_shapes=scratch,
        compiler_params=_cparams("parallel", "arbitrary"),
        name=name,
    )(x, c0, *operands)


def _attn_kernel(tile, nq, gq, n_meta, jmin_ref, mflag_ref, q_ref, qc_ref, kb_ref, vb_ref, kc_ref,
                 kbm_ref, vbm_ref, kcm_ref, o_ref, acc_ref, bnd_ref):
    b = pl.program_id(0)
    p = pl.program_id(1)
    g = pl.program_id(2)
    lane = lax.broadcasted_iota(jnp.int32, (1, LANES), 1)
    mcol = lax.broadcasted_iota(jnp.int32, (tile, LANES), 1)
    col = lax.broadcasted_iota(jnp.int32, (tile, tile), 1)
    row = lax.broadcasted_iota(jnp.int32, (tile, tile), 0)

    @pl.when(g == 0)
    def _():
        bnd_ref[0] = jnp.where(col <= row, -NEG_INF, NEG_INF)
        bnd_ref[1] = jnp.full((tile, tile), -NEG_INF, f32)
        bnd_ref[2] = jnp.full((tile, tile), NEG_INF, f32)

    def rows_of(j):
        return pl.ds(pl.multiple_of(j * tile, tile), tile)

    n_slots = 2 * gq
    base = (b * N_HEADS + 2 * p) * nq + g * gq

    def table_index(t):
        return base + (t // gq) * nq + t % gq

    def clear(t, carry):
        acc_ref[t] = jnp.zeros(acc_ref.shape[1:], f32)
        return carry

    lax.fori_loop(0, n_slots, clear, 0)

    def head_masks(hh):
        own = lane // HEAD_DIM == hh
        return own.astype(bf16), (~own).astype(bf16), (lane // AUG == 2 * p + hh).astype(bf16)

    def scores(tc, kt, kct):
        m_q, _, m_c = head_masks(tc // gq)
        rows = rows_of(tc % gq)
        q_op = jnp.concatenate([q_ref[0, rows, :] * m_q, qc_ref[0, rows, :] * m_c], axis=1)
        return _dot(q_op, jnp.concatenate([kt, kct], axis=0))

    def pv(tc, pe, vs):
        m_q, m_o, _ = head_masks(tc // gq)
        return _dot(pe, vs * m_q + m_o)

    def item_probs(t, j):
        tc = jnp.minimum(t, n_slots - 1)
        kind = jnp.where(t < n_slots, jnp.where(j == g * gq + tc % gq, 0, 1), 2)
        s = scores(tc, kb_ref[0, :, rows_of(j)], kc_ref[0, :, rows_of(j)])
        return jnp.exp2(jnp.minimum(s, bnd_ref[kind])).astype(bf16), tc

    def item_pv(pe, tc, j):
        acc_ref[tc] += pv(tc, pe, vb_ref[0, rows_of(j), :])

    def advance(t, j):
        last = jnp.logical_and(t < n_slots, j == g * gq + t % gq)
        t2 = jnp.where(last, t + 1, t)
        j2 = jnp.where(last, jmin_ref[table_index(jnp.minimum(t + 1, n_slots - 1))],
                       jnp.where(t < n_slots, j + 1, j))
        return t2.astype(jnp.int32), j2.astype(jnp.int32)

    def body(trip, carry):
        pe, tp, jp, t, j = carry
        for _ in range(ATT_UNROLL):
            pn, tn = item_probs(t, j)
            item_pv(pe, tp, jp)
            pe, tp, jp = pn, tn, j
            t, j = advance(t, j)
        return pe, tp, jp, t, j

    n_items = sum(g * gq + t % gq - jmin_ref[table_index(t)] + 1 for t in range(n_slots))
    trips = (n_items + ATT_UNROLL - 2) // ATT_UNROLL
    j0 = jmin_ref[base]
    pe0, t0 = item_probs(jnp.int32(0), j0)
    pe, tp, jp, _, _ = lax.fori_loop(0, trips, body, (pe0, t0, j0) + advance(jnp.int32(0), j0))
    item_pv(pe, tp, jp)

    def meta_tile(t, carry):
        @pl.when(mflag_ref[table_index(t)] > 0)
        def _():
            s = jnp.where(mcol < n_meta, scores(t, kbm_ref[...], kcm_ref[...]), NEG_INF)
            acc_ref[t] += pv(t, jnp.exp2(s).astype(bf16), vbm_ref[...])
        return carry

    lax.fori_loop(0, n_slots, meta_tile, 0)

    for r in range(gq):
        a0 = acc_ref[r]
        a1 = acc_ref[gq + r]
        own = lane < HEAD_DIM
        sums = pltpu.roll(jnp.where(own, a1, a0), HEAD_DIM, axis=1)
        o_ref[0, r * tile:(r + 1) * tile, :] = (jnp.where(own, a0, a1) / sums).astype(bf16)


def _prompt_attention(q, qcat, kb, vb, kcat, kbm, vbm, kcm, jmin, mflag, n_meta):
    b, l, _ = q.shape
    tile = min(ATT_TILE, l)
    nq = l // tile
    gq = min(ATT_GROUP, nq)
    rows = gq * tile
    grid_spec = pltpu.PrefetchScalarGridSpec(
        num_scalar_prefetch=2,
        grid=(b, N_HEADS // 2, nq // gq),
        in_specs=[
            pl.BlockSpec((1, rows, LANES), lambda b, p, i, *_: (b, i, p)),
            pl.BlockSpec((1, rows, LANES), lambda b, p, i, *_: (b, i, 0)),
            pl.BlockSpec((1, LANES, l), lambda b, p, i, *_: (b, p, 0)),
            pl.BlockSpec((1, l, LANES), lambda b, p, i, *_: (b, 0, p)),
            pl.BlockSpec((1, LANES, l), lambda b, p, i, *_: (b, 0, 0)),
            pl.BlockSpec((LANES, LANES), lambda b, p, i, *_: (p, 0)),
            pl.BlockSpec((LANES, LANES), lambda b, p, i, *_: (0, p)),
            pl.BlockSpec((LANES, LANES), lambda b, p, i, *_: (0, 0)),
        ],
        out_specs=pl.BlockSpec((1, rows, LANES), lambda b, p, i, *_: (b, i, p)),
        scratch_shapes=[pltpu.VMEM((2 * gq, tile, LANES), f32), pltpu.VMEM((3, tile, tile), f32)],
    )
    return pl.pallas_call(
        functools.partial(_attn_kernel, tile, nq, gq, n_meta),
        grid_spec=grid_spec,
        out_shape=jax.ShapeDtypeStruct((b, l, D_ATT), bf16),
        compiler_params=_cparams("parallel", "parallel", "arbitrary"),
        name="prompt_attention",
    )(jmin, mflag, q, qcat, kb, vb, kcat, kbm, vbm, kcm)


def _rowshift_kernel(tile, nq, gq, n_meta, jmin_ref, mflag_ref, q_ref, qc_ref, kb_ref, kc_ref, kbm_ref, kcm_ref,
                     o_ref):
    b = pl.program_id(0)
    p = pl.program_id(1)
    g = pl.program_id(2)
    lane = lax.broadcasted_iota(jnp.int32, (1, LANES), 1)
    slot = lane % AUG
    mcol = lax.broadcasted_iota(jnp.int32, (tile, LANES), 1)
    col = lax.broadcasted_iota(jnp.int32, (tile, tile), 1)
    row = lax.broadcasted_iota(jnp.int32, (tile, tile), 0)

    def rows_of(j):
        return pl.ds(pl.multiple_of(j * tile, tile), tile)

    def row_max(x):
        return jnp.max(x, axis=1, keepdims=True)

    for hh in range(2):
        h = 2 * p + hh
        base = (b * N_HEADS + h) * nq + g * gq
        m_q = (lane // HEAD_DIM == hh).astype(bf16)
        mine = lane // AUG == h
        m_c = mine.astype(bf16)

        def per_tile(r, carry):
            i = g * gq + r
            q_op = jnp.concatenate([q_ref[0, rows_of(r), :] * m_q, qc_ref[0, rows_of(r), :] * m_c], axis=1)

            def scores(kt, kct):
                return _dot(q_op, jnp.concatenate([kt, kct], axis=0))

            m = row_max(jnp.where(col <= row, scores(kb_ref[0, :, rows_of(i)], kc_ref[0, :, rows_of(i)]), NEG_INF))
            m = lax.fori_loop(jmin_ref[base + r], i, lambda j, m: jnp.maximum(
                m, row_max(scores(kb_ref[0, :, rows_of(j)], kc_ref[0, :, rows_of(j)]))), m)
            mm = row_max(jnp.where(mcol < n_meta, scores(kbm_ref[...], kcm_ref[...]), NEG_INF))
            m = jnp.where(mflag_ref[base + r] > 0, jnp.maximum(m, mm), m)
            hi, mid, lo = _split3(jnp.broadcast_to(-m, (tile, LANES)))
            patch = jnp.where(mine & (slot == SHIFT_SLOT), hi, jnp.where(mine & (slot == SHIFT_SLOT + 1), mid,
                              jnp.where(mine & (slot == SHIFT_SLOT + 2), lo, 0.0))).astype(bf16)
            if hh == 0:
                o_ref[0, 0, rows_of(r), :] = patch
            else:
                o_ref[0, 0, rows_of(r), :] = o_ref[0, 0, rows_of(r), :] + patch
            return carry

        lax.fori_loop(0, gq, per_tile, 0)


def _row_shift(q, qcat, kb, kcat, kbm, kcm, jmin, mflag, n_meta):
    b, l, _ = q.shape
    tile = min(ATT_TILE, l)
    nq = l // tile
    gq = min(ATT_GROUP, nq)
    rows = gq * tile
    grid_spec = pltpu.PrefetchScalarGridSpec(
        num_scalar_prefetch=2,
        grid=(b, N_HEADS // 2, nq // gq),
        in_specs=[
            pl.BlockSpec((1, rows, LANES), lambda b, p, i, *_: (b, i, p)),
            pl.BlockSpec((1, rows, LANES), lambda b, p, i, *_: (b, i, 0)),
            pl.BlockSpec((1, LANES, l), lambda b, p, i, *_: (b, p, 0)),
            pl.BlockSpec((1, LANES, l), lambda b, p, i, *_: (b, 0, 0)),
            pl.BlockSpec((LANES, LANES), lambda b, p, i, *_: (p, 0)),
            pl.BlockSpec((LANES, LANES), lambda b, p, i, *_: (0, 0)),
        ],
        out_specs=pl.BlockSpec((1, 1, rows, LANES), lambda b, p, i, *_: (b, p, i, 0)),
    )
    return pl.pallas_call(
        functools.partial(_rowshift_kernel, tile, nq, gq, n_meta),
        grid_spec=grid_spec,
        out_shape=jax.ShapeDtypeStruct((b, N_HEADS // 2, l, LANES), bf16),
        compiler_params=_cparams("parallel", "parallel", "arbitrary"),
        name="row_shift",
    )(jmin, mflag, q, qcat, kb, kcat, kbm, kcm)


def _sattn_kernel(past, tn, q_ref, kn_ref, vn_ref, ckt_ref, cvt_ref, lft_ref, o_ref):
    s_len = lft_ref.shape[-1]
    rows = N_HEADS * tn
    c = lft_ref[0]
    lane_h = lax.broadcasted_iota(jnp.int32, c.shape, 1)
    sh = 1
    while sh < s_len:
        c = c + jnp.where(lane_h >= sh, pltpu.roll(c, sh, axis=1), 0.0)
        sh *= 2
    cexp = jnp.concatenate([jnp.broadcast_to(c[h:h + 1, :], (tn, s_len)) for h in range(N_HEADS)], axis=0)
    rix = lax.broadcasted_iota(jnp.int32, (rows, s_len), 0)
    lix = lax.broadcasted_iota(jnp.int32, (rows, s_len), 1)
    qpos = past + rix % tn
    cq = jnp.sum(jnp.where(lix == qpos, cexp, 0.0), axis=1, keepdims=True)
    pad = jnp.zeros((s_len - past - tn, D_ATT), bf16)
    k_new = jnp.concatenate([kn_ref[0], pad], axis=0)
    v_new = jnp.concatenate([vn_ref[0], pad], axis=0)
    qrep = jnp.concatenate([q_ref[0].astype(f32)] * N_HEADS, axis=0)
    r5 = lax.broadcasted_iota(jnp.int32, (rows, D_ATT), 0)
    l5 = lax.broadcasted_iota(jnp.int32, (rows, D_ATT), 1)
    qrows = jnp.where(l5 // HEAD_DIM == r5 // tn, qrep, 0.0).astype(bf16)
    s = jnp.concatenate([_dot(qrows, ckt_ref[0].astype(bf16)), _dot_nt(qrows, k_new)], axis=1)
    s = s + (cq - cexp) * LOG2E
    s = jnp.where(lix <= qpos, s, NEG_INF)
    m = jnp.max(s, axis=1, keepdims=True)
    pe = jnp.exp2(s - m)
    den = jnp.sum(pe, axis=1, keepdims=True)
    pb = pe.astype(bf16)
    o = (_dot_nt(pb[:, :past], cvt_ref[0].astype(bf16)) + _dot(pb[:, past:], v_new)) / den
    lo = lax.broadcasted_iota(jnp.int32, (tn, D_ATT), 1)
    out = jnp.zeros((tn, D_ATT), f32)
    for h in range(N_HEADS):
        out = out + jnp.where(lo // HEAD_DIM == h, o[h * tn:(h + 1) * tn, :], 0.0)
    o_ref[0] = out.astype(bf16)


def _sample_attention(q, kn, vn, ckt, cvt, lft):
    bs, tn, _ = q.shape
    past = ckt.shape[2]
    s_len = lft.shape[-1]
    blk = lambda n, w: pl.BlockSpec((1, n, w), lambda b: (b, 0, 0))
    return pl.pallas_call(
        functools.partial(_sattn_kernel, past, tn),
        grid=(bs,),
        in_specs=[blk(tn, D_ATT), blk(tn, D_ATT), blk(tn, D_ATT), blk(D_ATT, past), blk(D_ATT, past),
                  blk(N_HEADS, s_len)],
        out_specs=blk(tn, D_ATT),
        out_shape=jax.ShapeDtypeStruct((bs, tn, D_ATT), bf16),
        compiler_params=_cparams("parallel"),
        name="sample_attention",
    )(q, kn, vn, ckt, cvt, lft)


def _s5prep_kernel(n_sq, are_ref, aim_ref, ldt_ref, btr_ref, bti_ref, cre_ref, cim_ref,
                   aqr_ref, aqi_ref, apr_ref, api_ref, wr_ref, wi_ref, ctr_ref, cti_ref, kk_ref):
    are = are_ref[...]
    aim = aim_ref[...]
    dt = jnp.exp(ldt_ref[...])
    mag = jnp.exp(are * dt)
    ph = aim * dt
    abr = mag * jnp.cos(ph)
    abi = mag * jnp.sin(ph)
    nr = abr - 1.0
    den = are * are + aim * aim
    fr = (nr * are + abi * aim) / den
    fi = (abi * are - nr * aim) / den
    pw = [(jnp.ones_like(abr), jnp.zeros_like(abr))]
    for _ in range(S5_Q):
        pr, pi = pw[-1]
        pw.append((pr * abr - pi * abi, pr * abi + pi * abr))
    aqr_ref[...], aqi_ref[...] = pw[S5_Q]
    pr, pi = pw[S5_Q]
    for _ in range(n_sq):
        pr, pi = pr * pr - pi * pi, 2.0 * pr * pi
    apr_ref[...] = pr
    api_ref[...] = pi
    p_states = are.shape[1]
    ident = (lax.broadcasted_iota(jnp.int32, (p_states, p_states), 0)
             == lax.broadcasted_iota(jnp.int32, (p_states, p_states), 1)).astype(bf16)
    lane = lax.broadcasted_iota(jnp.int32, (SSM_GROUP, S5_Q * SSM_GROUP), 1)
    hi = functools.partial(lax.dot_general, dimension_numbers=(((1,), (1,)), ((), ())),
                           precision=lax.Precision.HIGHEST, preferred_element_type=f32)
    for g in range(are.shape[0]):
        row = slice(g, g + 1)
        btr = btr_ref[g]
        bti = bti_ref[g]
        bbr = fr[row] * btr - fi[row] * bti
        bbi = fr[row] * bti + fi[row] * btr
        cre = cre_ref[g]
        cim = cim_ref[g]
        car_all, cai_all = [cre], [cim]
        for q in range(S5_Q):
            er, ei = pw[S5_Q - 1 - q]
            wr_ref[q, g] = er[row] * bbr - ei[row] * bbi
            wi_ref[q, g] = er[row] * bbi + ei[row] * bbr
            gr, gi = pw[q + 1]
            car_all.append(cre * gr[row] - cim * gi[row])
            cai_all.append(cre * gi[row] + cim * gr[row])
        ctr_ref[g] = _dot_nt(ident, jnp.concatenate(car_all[1:], axis=0).astype(bf16))
        cti_ref[g] = _dot_nt(ident, jnp.concatenate(cai_all[1:], axis=0).astype(bf16))
        kt = (hi(bbr, jnp.concatenate(car_all[:S5_Q], axis=0))
              - hi(bbi, jnp.concatenate(cai_all[:S5_Q], axis=0)))
        kk_ref[g] = jnp.concatenate(
            [kt if qi == 0 else jnp.where(lane >= qi * SSM_GROUP, pltpu.roll(kt, qi * SSM_GROUP, axis=1), 0.0)
             for qi in range(S5_Q)], axis=0)


def _s5_prep(a_re, a_im, log_dt, bt_re, bt_im, c_re, c_im, n_sq):
    g, p = a_re.shape
    sd = jax.ShapeDtypeStruct
    qh = S5_Q * SSM_GROUP
    return pl.pallas_call(
        functools.partial(_s5prep_kernel, n_sq),
        out_shape=[sd((g, p), f32)] * 4 + [sd((S5_Q,) + bt_re.shape, f32)] * 2 + [sd((g, p, qh), f32)] * 2
        + [sd((g, qh, qh), f32)],
        name="s5_prep",
    )(a_re, a_im, log_dt, bt_re, bt_im, c_re, c_im)


def _s5_kernel(chain, n_seg, n_sb, u_ref, w8_ref, toep_ref, cpow_ref, aqr_ref, aqi_ref, apr_ref, api_ref, d8_ref,
               x0r_ref, x0i_ref, y_ref, sr_ref, si_ref, xr_ref, xi_ref):
    aqr = aqr_ref[0]
    aqi = aqi_ref[0]
    w8 = w8_ref[0]
    toep = toep_ref[0]
    cpow = cpow_ref[0]
    d8 = d8_ref[0]

    def step(with_y, sb):
        rows = pl.ds(sb, n_seg, stride=n_sb) if n_sb > 1 else pl.ds(0, n_seg)
        uf = jnp.concatenate([u_ref[0, 0, q, rows, :] for q in range(S5_Q)], axis=1)
        ub = uf.astype(bf16)
        xr = xr_ref[...]
        xi = xi_ref[...]
        if with_y:
            xc = jnp.concatenate([xr, xi], axis=1).astype(bf16)
            y = _dot(ub, toep) + _dot(xc, cpow) + d8 * uf
            for q in range(S5_Q):
                y_ref[0, 0, q, rows, :] = y[:, q * LANES:(q + 1) * LANES]
        pj = _dot(ub, w8)
        xr_ref[...] = aqr * xr - aqi * xi + pj[:, :STATE_BLOCK]
        xi_ref[...] = aqr * xi + aqi * xr + pj[:, STATE_BLOCK:]

    if chain:
        xr_ref[...] = jnp.zeros(xr_ref.shape, f32)
        xi_ref[...] = jnp.zeros(xi_ref.shape, f32)
        for sb in range(n_sb):
            step(False, sb)
        apr = apr_ref[0]
        api = api_ref[0]

        def scan(c, carry):
            kr, ki = carry
            er = xr_ref[pl.ds(c, 1), :]
            ei = xi_ref[pl.ds(c, 1), :]
            xr_ref[pl.ds(c, 1), :] = kr
            xi_ref[pl.ds(c, 1), :] = ki
            return apr * kr - api * ki + er, apr * ki + api * kr + ei

        kr, ki = lax.fori_loop(0, n_seg, scan, (x0r_ref[0], x0i_ref[0]))
        sr_ref[0] = kr
        si_ref[0] = ki
        for sb in range(n_sb):
            step(True, sb)
    else:
        xr_ref[...] = x0r_ref[0]
        xi_ref[...] = x0i_ref[0]
        for sb in range(n_sb):
            step(True, sb)
        sr_ref[0] = xr_ref[...]
        si_ref[0] = xi_ref[...]


def _s5(u, x0r, x0i, weights, n_seg, chain, name):
    bq, _, _, m, _ = u.shape
    n_sb = m // n_seg
    r0 = x0r.shape[1]
    per_k = lambda a: pl.BlockSpec((1,) + a.shape[1:], lambda b, k: (k, 0, 0))
    st = pl.BlockSpec((1, r0, STATE_BLOCK), lambda b, k: (b, 0, k))
    tok = lambda: pl.BlockSpec((1, 1, S5_Q, m, LANES), lambda b, k: (b, k, 0, 0, 0))
    sd = jax.ShapeDtypeStruct
    return pl.pallas_call(
        functools.partial(_s5_kernel, chain, n_seg, n_sb),
        grid=(bq, N_SSM_BLOCKS),
        in_specs=[tok()] + [per_k(a) for a in weights] + [st, st],
        out_specs=[tok(), st, st],
        out_shape=[sd(u.shape, f32), sd(x0r.shape, f32), sd(x0i.shape, f32)],
        scratch_shapes=[pltpu.VMEM((n_seg, STATE_BLOCK), f32)] * 2,
        compiler_params=_cparams("parallel", "parallel"),
        name=name,
    )(u, *weights, x0r, x0i)


def _out_kernel(x_ref, att_ref, ga_ref, ys_ref, gs_ref, wg_ref, bg_ref, wo_ref, y_ref, ys_scr):
    ga = ga_ref[0].astype(f32)
    a = att_ref[0].astype(f32) * (ga * jax.nn.sigmoid(ga))
    rows = x_ref.shape[1]
    for kb in range(N_SSM_BLOCKS):
        for qq in range(S5_Q):
            ys_scr[kb, pl.ds(qq, rows // S5_Q, stride=S5_Q), :] = ys_ref[0, kb, qq]
    ys = jnp.concatenate([ys_scr[kb] for kb in range(N_SSM_BLOCKS)], axis=1)
    z = 0.5 * ys * (1.0 + jnp.tanh(0.7978845608028654 * (ys + 0.044715 * (ys * ys * ys))))
    gate = jax.nn.sigmoid(_dot(z.astype(bf16), wg_ref[...]) + bg_ref[...])
    gs = gs_ref[0].astype(f32)
    s5 = z * gate * (gs * jax.nn.sigmoid(gs))
    cat = jnp.concatenate([a, s5], axis=1).astype(bf16)
    y_ref[0] = x_ref[0] + _dot(cat, wo_ref[...])


def _outproj(x, att, ga, ys, gs, wg, bg, wo, tile, name):
    bx, lx, d = x.shape
    full = lambda a: pl.BlockSpec(a.shape, lambda b, j: (0,) * a.ndim)
    row = lambda w: pl.BlockSpec((1, tile, w), lambda b, j: (b, j, 0))
    return pl.pallas_call(
        _out_kernel,
        grid=(bx, lx // tile),
        in_specs=[row(d), row(D_ATT), row(D_ATT), _s5_tokens_spec(tile, lambda b, j: (b, 0, 0, j, 0)), row(D_SSM),
                  full(wg), full(bg), full(wo)],
        out_specs=row(d),
        out_shape=jax.ShapeDtypeStruct(x.shape, f32),
        scratch_shapes=[pltpu.VMEM((N_SSM_BLOCKS, tile, LANES), f32)],
        compiler_params=_cparams("parallel", "parallel"),
        name=name,
    )(x, att, ga, ys, gs, wg, bg, wo)


def _lower_tri(n):
    r = jnp.arange(n)
    return (r[None, :] <= r[:, None]).astype(bf16)


def kernel(x_prompt, x_sample, cache_k, cache_v, cache_logf, state_s5_re, state_s5_im, meta_tokens, norm_g,
           w_in, b_f, q_norm_g, k_norm_g, s5_a_re, s5_a_im, s5_log_dt, s5_b_re, s5_b_im, s5_c_re, s5_c_im,
           s5_d, w_glu, b_glu, w_out):
    assert norm_g.shape[0] == 1, "single-layer step"
    bp, lp, d = x_prompt.shape
    bs, ts, _ = x_sample.shape
    n_meta = meta_tokens.shape[0]
    past = cache_k.shape[2]
    G, P = N_GROUPS, SSM_STATE

    w = w_in[0]
    o3 = 3 * D_ATT
    o4 = o3 + N_HEADS
    wm = jnp.concatenate([w[:, :o3], w[:, o4:]], axis=1).astype(bf16)
    wf = jnp.repeat(w[:, o3:o4], AUG, axis=1).astype(bf16)
    bfr = jnp.repeat(b_f[0], AUG)[None, :].astype(f32)
    g_in = norm_g[0][None, :].astype(f32)
    qg = (jnp.tile(q_norm_g[0], N_HEADS) * (HEAD_DIM ** -0.5 * LOG2E))[None, :].astype(f32)
    kg = jnp.tile(k_norm_g[0], N_HEADS)[None, :].astype(f32)
    hid = jnp.arange(D_ATT) // HEAD_DIM
    eh = ((hid[:, None] == hid[None, :]).astype(f32) / HEAD_DIM).astype(bf16)
    sel = (jnp.arange(LANES)[None, :] == AUG * jnp.arange(N_HEADS)[:, None]).astype(bf16)
    m_bound = 8.0 * jnp.max(jnp.abs(q_norm_g[0])) * jnp.max(jnp.abs(k_norm_g[0])) * 1.02
    robust = m_bound > SHIFT_SAFE
    negm = jnp.full((1, LANES), -LOG2E, f32) * jnp.where(robust, 0.0, m_bound)
    params = (g_in, wm, wf, bfr, qg, kg, eh, negm, sel)

    tm = n_meta
    meta = _inproj(meta_tokens[None].astype(f32), jnp.zeros((1, 1, LANES), f32), params, tm, _lower_tri(tm), True,
                   "inproj_meta")
    (_, kb_m, vb_m, kf_m, vf_m, lft_m, edge_m, _, kcat_m, _, u_m, _) = meta
    cum_meta_end = edge_m[:, -1, 1:2, :]
    c0 = jnp.broadcast_to(cum_meta_end, (bp, 1, LANES))
    tile_p = min(ROW_TILE, lp)
    tb_p = min(LANES, tile_p)
    meta_cols = lambda a: jnp.pad(jnp.swapaxes(a, 1, 2), ((0, 0), (0, 0), (0, LANES - tm)))
    (q_p, kb_p, vb_p, kt_p, vt_p, lft_p, edge_p, qcat_p, kcat_p, ga_p, u_p, gs_p) = _inproj(
        x_prompt, c0, params, tile_p, _lower_tri(tb_p), True, "inproj_prompt",
        meta_t=(meta_cols(kf_m), meta_cols(vf_m), n_meta))
    ls = bs * ts
    tile_s = min(ROW_TILE, ls)
    (q_s, kb_s, vb_s, kf_s, vf_s, lft_s, _, _, _, ga_s, u_s, gs_s) = _inproj(
        x_sample.reshape(1, ls, d), jnp.zeros((1, 1, LANES), f32), params, tile_s,
        _lower_tri(min(LANES, tile_s)), False, "inproj_sample")

    tile_a = min(ATT_TILE, lp)
    nq = lp // tile_a
    per_tile = tile_a // tb_p
    edges = edge_p[:, :, :2, ::AUG]
    q_start = edges[:, 0::per_tile, 0, :]
    k_end = edges[:, per_tile - 1::per_tile, 1, :]
    dmax = q_start[:, :, None, :] - k_end[:, None, :, :]
    padm = lambda a: jnp.pad(a[0], ((0, LANES - tm), (0, 0)))
    kbm, vbm, kcm = padm(kb_m).T, padm(vb_m), padm(kcat_m).T

    def skip_tables(thr):
        jmin = jnp.sum((dmax < -thr).astype(jnp.int32), axis=2)
        jmin = jnp.minimum(jmin, jnp.arange(nq, dtype=jnp.int32)[None, :, None])
        mflag = (q_start - cum_meta_end[:, :, ::AUG] >= -thr).astype(jnp.int32)
        return jnp.transpose(jmin, (0, 2, 1)).reshape(-1), jnp.transpose(mflag, (0, 2, 1)).reshape(-1)

    def attend_bounded():
        return _prompt_attention(q_p, qcat_p, kb_p, vb_p, kcat_p, kbm, vbm, kcm, *skip_tables(EXP_ZERO), n_meta)

    def attend_robust():
        tables = skip_tables(EXP_ZERO + 2.0 * m_bound)
        shift = _row_shift(q_p, qcat_p, kb_p, kcat_p, kbm, kcm, *tables, n_meta)
        qcat_r = (qcat_p.astype(f32) + jnp.sum(shift.astype(f32), axis=1)).astype(bf16)
        return _prompt_attention(q_p, qcat_r, kb_p, vb_p, kcat_p, kbm, vbm, kcm, *tables, n_meta)

    att_p = lax.cond(robust, attend_robust, attend_bounded)

    lft_new = jnp.transpose(lft_s.reshape(N_HEADS, bs, ts), (1, 0, 2))
    s_len = -(-(past + ts) // LANES) * LANES
    lft = jnp.concatenate([jnp.transpose(cache_logf[0].astype(f32), (0, 2, 1)), lft_new,
                           jnp.zeros((bs, N_HEADS, s_len - past - ts), f32)], axis=2)
    cache_t = lambda c: jnp.transpose(c[0], (0, 2, 3, 1)).reshape(bs, D_ATT, past)
    att_s = _sample_attention(q_s.reshape(bs, ts, D_ATT), kb_s.reshape(bs, ts, D_ATT), vb_s.reshape(bs, ts, D_ATT),
                              cache_t(cache_k), cache_t(cache_v), lft)

    n_seg_p = min(PROMPT_SEGMENTS, lp // S5_Q)
    n_sb_p = lp // (S5_Q * n_seg_p)
    n_sq = n_sb_p.bit_length() - 1
    assert n_sb_p == 1 << n_sq and ts % S5_Q == 0
    tr = lambda a: jnp.transpose(a[0], (0, 2, 1)).astype(f32)
    aqr, aqi, apr, api, wr, wi, ctr, cti, kk = _s5_prep(
        s5_a_re[0].astype(f32), s5_a_im[0].astype(f32), s5_log_dt[0][:, None].astype(f32),
        tr(s5_b_re), tr(s5_b_im), s5_c_re[0].astype(f32), s5_c_im[0].astype(f32), n_sq)
    NB, G8, Q = N_SSM_BLOCKS, GROUPS_PER_BLOCK, S5_Q
    qgh = Q * LANES

    def spread(x, col_of_src, row_group, col_group):
        n_src = x.shape[-1]
        n_rows = x.shape[1]
        n_cols = col_of_src.shape[0]
        sel = (col_of_src[None, :] == jnp.arange(n_src)[:, None]).astype(bf16)
        keep = row_group(jnp.arange(n_rows))[:, None] == col_group(jnp.arange(n_cols))[None, :]
        y = jnp.einsum("krs,sc->krc", x.astype(bf16), sel, preferred_element_type=f32)
        return jnp.where(keep[None], y, 0.0).astype(bf16)

    cols_state = jnp.arange(STATE_BLOCK) % P
    cols_qgh = (jnp.arange(qgh) // LANES) * SSM_GROUP + jnp.arange(qgh) % SSM_GROUP
    group_of_qgh = lambda i: (i // SSM_GROUP) % G8
    by_block = lambda a: jnp.transpose(a.reshape(Q, NB, G8 * SSM_GROUP, P), (1, 0, 2, 3)).reshape(NB, qgh, P)
    w_part = lambda a: spread(by_block(a), cols_state, group_of_qgh, lambda i: i // P)
    w8 = jnp.concatenate([w_part(wr), w_part(wi)], axis=2)
    c_part = lambda a: spread(a.reshape(NB, STATE_BLOCK, Q * SSM_GROUP), cols_qgh, lambda i: i // P, group_of_qgh)
    cpow = jnp.concatenate([c_part(ctr), c_part(-cti)], axis=1)
    kk_rows = jnp.transpose(kk.reshape(NB, G8, Q, SSM_GROUP, Q * SSM_GROUP), (0, 2, 1, 3, 4)).reshape(NB, qgh, -1)
    toep = spread(kk_rows, cols_qgh, group_of_qgh, group_of_qgh)
    lane_k = lambda a: a.reshape(NB, 1, -1)
    d8 = jnp.tile(lane_k(s5_d[0].astype(f32)), (1, 1, Q))
    weights = lambda pr_, pi_: (w8, toep, cpow, lane_k(aqr), lane_k(aqi), lane_k(pr_), lane_k(pi_), d8)

    assert tm == ts
    n_small = -(-(bs + 1) // SUBLANES) * SUBLANES
    u_small = jnp.concatenate([u_s, u_m, jnp.zeros(_s5_tokens_shape(1, (n_small - bs - 1) * ts), f32)], axis=3)
    pad_state = lambda a: jnp.pad(a[0].reshape(bs, G * P).astype(f32), ((0, n_small - bs), (0, 0)))[None]
    ys_small, sr_small, si_small = _s5(u_small, pad_state(state_s5_re), pad_state(state_s5_im),
                                       weights(aqr, aqi), n_small, False, "s5_short")
    x0r = jnp.broadcast_to(sr_small[:, bs:bs + 1, :], (bp, 1, G * P))
    x0i = jnp.broadcast_to(si_small[:, bs:bs + 1, :], (bp, 1, G * P))
    ys_p, sr_p, si_p = _s5(u_p, x0r, x0i, weights(apr, api), n_seg_p, True, "s5_prompt")

    wg = w_glu[0].astype(bf16)
    bg = b_glu[0][None, :].astype(f32)
    wo = w_out[0].astype(bf16)
    y_prompt = _outproj(x_prompt, att_p, ga_p, ys_p, gs_p, wg, bg, wo, tile_p, "outproj_prompt")
    y_sample = _outproj(x_sample.reshape(1, ls, d), att_s.reshape(1, ls, D_ATT), ga_s, ys_small[:, :, :, :ls // S5_Q],
                        gs_s, wg, bg, wo, tile_s, "outproj_sample").reshape(bs, ts, d)

    heads_t = lambda a: jnp.transpose(a.reshape(bp, N_HEADS, HEAD_DIM, n_meta + lp), (0, 3, 1, 2))[None]
    new_k_prompt = heads_t(kt_p)
    new_v_prompt = heads_t(vt_p)
    lft_all = jnp.concatenate([jnp.broadcast_to(lft_m, (bp, N_HEADS, n_meta)), lft_p], axis=2)
    new_logf_prompt = jnp.transpose(lft_all, (0, 2, 1))[None]
    new_s5_re_prompt = sr_p.reshape(1, bp, G, P)
    new_s5_im_prompt = si_p.reshape(1, bp, G, P)
    new_k_sample = kf_s.reshape(1, bs, ts, N_HEADS, HEAD_DIM)
    new_v_sample = vf_s.reshape(1, bs, ts, N_HEADS, HEAD_DIM)
    new_logf_sample = jnp.transpose(lft_new, (0, 2, 1))[None]
    new_s5_re_sample = sr_small[:, :bs].reshape(1, bs, G, P)
    new_s5_im_sample = si_small[:, :bs].reshape(1, bs, G, P)
    return (y_prompt, y_sample, new_k_prompt, new_v_prompt, new_logf_prompt, new_s5_re_prompt, new_s5_im_prompt,
            new_k_sample, new_v_sample, new_logf_sample, new_s5_re_sample, new_s5_im_sample)
```

```python
import functools

import jax
import jax.numpy as jnp
from jax import lax
from jax.experimental import pallas as pl
from jax.experimental.pallas import tpu as pltpu

N_HEADS = 8
HEAD_DIM = 64
D_ATT = N_HEADS * HEAD_DIM
SSM_GROUP = 16
SSM_STATE = 64
D_SSM = 512
N_GROUPS = D_SSM // SSM_GROUP
EPS = 1e-6
NEG_INF = -1e30

LANES = 128
SUBLANES = 8
AUG = LANES // N_HEADS
SHIFT_SLOT = 7
GROUPS_PER_BLOCK = LANES // SSM_GROUP
STATE_BLOCK = GROUPS_PER_BLOCK * SSM_STATE
N_SSM_BLOCKS = D_SSM // LANES
VMEM_LIMIT = 56 * 1024 * 1024
EXP_ZERO = 105.0
SHIFT_SAFE = 32.0

ROW_TILE = 512
OUT_TILE = 1024
ATT_TILE = 256
ATT_GROUP = 32
ATT_UNROLL = 32
LOG2E = 1.4426950408889634
PROMPT_SEGMENTS = 512
S5_Q = 8

bf16 = jnp.bfloat16
f32 = jnp.float32


def _cparams(*sem):
    return pltpu.CompilerParams(dimension_semantics=sem, vmem_limit_bytes=VMEM_LIMIT)


def _split3(x):
    hi = x.astype(bf16).astype(f32)
    r1 = x - hi
    mid = r1.astype(bf16).astype(f32)
    lo = (r1 - mid).astype(bf16).astype(f32)
    return hi, mid, lo


def _dot(a, b):
    return jnp.dot(a, b, preferred_element_type=f32)


def _dot_nt(a, b):
    return lax.dot_general(a, b, (((1,), (1,)), ((), ())), preferred_element_type=f32)


def _s5_tokens_shape(b, l):
    return (b, N_SSM_BLOCKS, S5_Q, l // S5_Q, LANES)


def _s5_tokens_spec(rows, index_map, **kw):
    return pl.BlockSpec((1, N_SSM_BLOCKS, S5_Q, rows // S5_Q, LANES), index_map, **kw)


def _inproj_kernel(chain, n_blk, join, *refs):
    n_in = 12
    if chain:
        c0_ref, carry_ref = refs[1], refs[-5 if join else -2]

        @pl.when(pl.program_id(1) == 0)
        def _():
            carry_ref[...] = c0_ref[0]
    if not join:
        _inproj_tile(chain, n_blk, False, *refs)
        return
    kmt_ref, vmt_ref = refs[n_in:n_in + 2]
    refs = refs[:n_in] + refs[n_in + 2:]
    kb_ref, kf_ref, vf_ref, kcat_ref = refs[n_in + 1], refs[n_in + 3], refs[n_in + 4], refs[n_in + 8]
    prev_ref, prevc_ref, tails_ref = refs[-3:]
    j = pl.program_id(1)
    n_tiles = pl.num_programs(1) - 2
    lane_t = lax.broadcasted_iota(jnp.int32, (D_ATT, LANES), 1)

    def emit():
        kb_ref[0] = prev_ref[0].T.astype(bf16)
        kcat_ref[0] = prevc_ref[...].T.astype(bf16)
        for idx, t_ref in enumerate((kf_ref, vf_ref)):
            rolled = pltpu.roll(prev_ref[idx].T, join, axis=1)
            first = jnp.where(lane_t < join, tails_ref[idx], rolled[:, :LANES])
            t_ref[0] = first if rolled.shape[1] == LANES else jnp.concatenate([first, rolled[:, LANES:]], axis=1)
            tails_ref[idx] = jnp.where(j > 0, rolled[:, :LANES], tails_ref[idx])

    @pl.when(j == 0)
    def _():
        tails_ref[0] = kmt_ref[0]
        tails_ref[1] = vmt_ref[0]
        prev_ref[...] = jnp.zeros(prev_ref.shape, f32)
        prevc_ref[...] = jnp.zeros(prevc_ref.shape, f32)

    @pl.when(j < n_tiles)
    def _():
        emit()
        _inproj_tile(chain, n_blk, True, *refs)

    pl.when(j == n_tiles)(emit)

    @pl.when(j == n_tiles + 1)
    def _():
        for idx, t_ref in enumerate((kf_ref, vf_ref)):
            t_ref[0] = jnp.zeros(t_ref.shape[1:], f32)
            t_ref[0, :, :LANES] = tails_ref[idx]


def _inproj_tile(chain, n_blk, join, x_ref, c0_ref, g_ref, wm_ref, wf_ref, bf_ref, qg_ref, kg_ref, eh_ref,
                 tri_ref, negm_ref, sel_ref, q_ref, kb_ref, vb_ref, kf_ref, vf_ref, lft_ref, edge_ref,
                 qcat_ref, kcat_ref, ga_ref, u_ref, gs_ref, carry_ref, us_ref, prev_ref=None, prevc_ref=None,
                 tails_ref=None):
    x = x_ref[0]
    ms = jnp.mean(x * x, axis=-1, keepdims=True)
    xn = (x * lax.rsqrt(ms + EPS) * g_ref[...]).astype(bf16)
    z = _dot(xn, wm_ref[...])
    q = z[:, 0 * D_ATT:1 * D_ATT]
    k = z[:, 1 * D_ATT:2 * D_ATT]
    v = z[:, 2 * D_ATT:3 * D_ATT]
    eh = eh_ref[...]
    qn = q * lax.rsqrt(_dot((q * q).astype(bf16), eh) + EPS) * qg_ref[...]
    kn = k * lax.rsqrt(_dot((k * k).astype(bf16), eh) + EPS) * kg_ref[...]
    q_ref[0] = qn.astype(bf16)
    vb_ref[0] = v.astype(bf16)
    if join:
        prev_ref[0] = kn
        prev_ref[1] = v
    else:
        kb_ref[0] = kn.astype(bf16)
        kf_ref[0] = kn
        vf_ref[0] = v
    ga_ref[0] = z[:, 3 * D_ATT:4 * D_ATT].astype(bf16)
    gs_ref[0] = z[:, 4 * D_ATT + D_SSM:].astype(bf16)
    for kb in range(N_SSM_BLOCKS):
        us_ref[kb] = z[:, 4 * D_ATT + kb * LANES:4 * D_ATT + (kb + 1) * LANES]
    for kb in range(N_SSM_BLOCKS):
        for qq in range(S5_Q):
            u_ref[0, kb, qq] = us_ref[kb, pl.ds(qq, x.shape[0] // S5_Q, stride=S5_Q), :]

    zf = _dot(xn, wf_ref[...]) + bf_ref[...]
    lf = jnp.minimum(zf, 0.0) - jnp.log1p(jnp.exp(-jnp.abs(zf)))
    sel = sel_ref[...]
    lft_ref[0] = sum(_dot_nt(sel, part.astype(bf16)) for part in _split3(lf))

    carry = carry_ref[...] if chain else jnp.zeros((1, LANES), f32)
    tri = tri_ref[...]
    tb = x.shape[0] // n_blk
    cums = []
    for r in range(n_blk):
        hi, mid, lo = _split3(lf[r * tb:(r + 1) * tb])
        c = _dot(tri, hi.astype(bf16)) + _dot(tri, mid.astype(bf16)) + _dot(tri, lo.astype(bf16)) + carry
        cums.append(c)
        carry = c[tb - 1:tb, :]
        edge_ref[0, r] = jnp.concatenate([c[0:1, :], carry, jnp.zeros((SUBLANES - 2, LANES), f32)], axis=0)
    cum = cums[0] if n_blk == 1 else jnp.concatenate(cums, axis=0)
    if chain:
        carry_ref[...] = carry

    slot = lax.broadcasted_iota(jnp.int32, (1, LANES), 1) % AUG
    hi, mid, lo = _split3(cum * LOG2E)
    qc = jnp.where(slot == 0, hi, jnp.where(slot == 1, mid, jnp.where(slot == 2, lo,
                   jnp.where(slot <= 6, 1.0, 0.0))))
    kc = jnp.where(slot <= 2, 1.0, jnp.where(slot == 3, -hi, jnp.where(slot == 4, -mid,
                   jnp.where(slot == 5, -lo, jnp.where(slot == 6, negm_ref[...],
                             jnp.where(slot <= SHIFT_SLOT + 2, 1.0, 0.0))))))
    qcat_ref[0] = qc.astype(bf16)
    if join:
        prevc_ref[...] = kc
    else:
        kcat_ref[0] = kc.astype(bf16)


def _inproj(x, c0, params, tile, tri, chain, name, meta_t=None):
    bx, lx, d = x.shape
    n_tiles = lx // tile
    n_blk = tile // tri.shape[0]
    join = 0 if meta_t is None else meta_t[2]
    last = n_tiles - 1
    tj = (lambda j: jnp.minimum(j, last)) if join else (lambda j: j)
    tprev = lambda j: jnp.maximum(j - 1, 0)
    full = lambda a: pl.BlockSpec(a.shape, lambda b, j: (0,) * a.ndim)
    row = lambda w: pl.BlockSpec((1, tile, w), lambda b, j: (b, tj(j), 0))
    sd = jax.ShapeDtypeStruct
    rows = lambda w, t: (row(w), sd((bx, lx, w), t))
    kv = (rows(D_ATT, f32) if not join else
          (pl.BlockSpec((1, D_ATT, tile), lambda b, j: (b, 0, tprev(j))), sd((bx, D_ATT, join + lx), f32)))
    cols = lambda w: (pl.BlockSpec((1, w, tile), lambda b, j: (b, 0, jnp.minimum(tprev(j), last))), sd((bx, w, lx), bf16))
    outs = [rows(D_ATT, bf16), cols(D_ATT) if join else rows(D_ATT, bf16), rows(D_ATT, bf16), kv, kv,
            (pl.BlockSpec((1, N_HEADS, tile), lambda b, j: (b, 0, tj(j))), sd((bx, N_HEADS, lx), f32)),
            (pl.BlockSpec((1, n_blk, SUBLANES, LANES), lambda b, j: (b, tj(j), 0, 0)),
             sd((bx, n_tiles * n_blk, SUBLANES, LANES), f32)),
            rows(LANES, bf16), cols(LANES) if join else rows(LANES, bf16), rows(D_ATT, bf16),
            (_s5_tokens_spec(tile, lambda b, j: (b, 0, 0, tj(j), 0)), sd(_s5_tokens_shape(bx, lx), f32)),
            rows(D_SSM, bf16)]
    g_in, wm, wf, bfr, qg, kg, eh, negm, sel = params
    operands = (g_in, wm, wf, bfr, qg, kg, eh, tri, negm, sel) + (tuple(meta_t[:2]) if join else ())
    scratch = [pltpu.VMEM((1, LANES), f32), pltpu.VMEM((N_SSM_BLOCKS, tile, LANES), f32)]
    if join:
        scratch += [pltpu.VMEM((2, tile, D_ATT), f32), pltpu.VMEM((tile, LANES), f32), pltpu.VMEM((2, D_ATT, LANES), f32)]
    return pl.pallas_call(
        functools.partial(_inproj_kernel, chain, n_blk, join),
        grid=(bx, n_tiles + (2 if join else 0)),
        in_specs=[row(d), pl.BlockSpec((1, 1, LANES), lambda b, j: (b, 0, 0))] + [full(a) for a in operands],
        out_specs=[o[0] for o in outs],
        out_shape=[o[1] for o in outs],
        scratch_shapes=scratch,
        compiler_params=_cparams("parallel", "arbitrary"),
        name=name,
    )(x, c0, *operands)


def _attn_kernel(tile, nq, gq, n_meta, jmin_ref, mflag_ref, q_ref, qc_ref, kb_ref, vb_ref, kc_ref,
                 kbm_ref, vbm_ref, kcm_ref, o_ref, acc_ref, bnd_ref):
    b = pl.program_id(0)
    p = pl.program_id(1)
    g = pl.program_id(2)
    lane = lax.broadcasted_iota(jnp.int32, (1, LANES), 1)
    mcol = lax.broadcasted_iota(jnp.int32, (tile, LANES), 1)
    col = lax.broadcasted_iota(jnp.int32, (tile, tile), 1)
    row = lax.broadcasted_iota(jnp.int32, (tile, tile), 0)

    @pl.when(g == 0)
    def _():
        bnd_ref[0] = jnp.where(col <= row, -NEG_INF, NEG_INF)
        bnd_ref[1] = jnp.full((tile, tile), -NEG_INF, f32)
        bnd_ref[2] = jnp.full((tile, tile), NEG_INF, f32)

    def rows_of(j):
        return pl.ds(pl.multiple_of(j * tile, tile), tile)

    n_slots = 2 * gq
    base = (b * N_HEADS + 2 * p) * nq + g * gq

    def table_index(t):
        return base + (t // gq) * nq + t % gq

    def clear(t, carry):
        acc_ref[t] = jnp.zeros(acc_ref.shape[1:], f32)
        return carry

    lax.fori_loop(0, n_slots, clear, 0)

    def head_masks(hh):
        own = lane // HEAD_DIM == hh
        return own.astype(bf16), (~own).astype(bf16), (lane // AUG == 2 * p + hh).astype(bf16)

    def scores(tc, kt, kct):
        m_q, _, m_c = head_masks(tc // gq)
        rows = rows_of(tc % gq)
        q_op = jnp.concatenate([q_ref[0, rows, :] * m_q, qc_ref[0, rows, :] * m_c], axis=1)
        return _dot(q_op, jnp.concatenate([kt, kct], axis=0))

    def pv(tc, pe, vs):
        m_q, m_o, _ = head_masks(tc // gq)
        return _dot(pe, vs * m_q + m_o)

    def item_probs(t, j):
        tc = jnp.minimum(t, n_slots - 1)
        kind = jnp.where(t < n_slots, jnp.where(j == g * gq + tc % gq, 0, 1), 2)
        s = scores(tc, kb_ref[0, :, rows_of(j)], kc_ref[0, :, rows_of(j)])
        return jnp.exp2(jnp.minimum(s, bnd_ref[kind])).astype(bf16), tc

    def item_pv(pe, tc, j):
        acc_ref[tc] += pv(tc, pe, vb_ref[0, rows_of(j), :])

    def advance(t, j):
        last = jnp.logical_and(t < n_slots, j == g * gq + t % gq)
        t2 = jnp.where(last, t + 1, t)
        j2 = jnp.where(last, jmin_ref[table_index(jnp.minimum(t + 1, n_slots - 1))],
                       jnp.where(t < n_slots, j + 1, j))
        return t2.astype(jnp.int32), j2.astype(jnp.int32)

    def body(trip, carry):
        pe, tp, jp, t, j = carry
        for _ in range(ATT_UNROLL):
            pn, tn = item_probs(t, j)
            item_pv(pe, tp, jp)
            pe, tp, jp = pn, tn, j
            t, j = advance(t, j)
        return pe, tp, jp, t, j

    n_items = sum(g * gq + t % gq - jmin_ref[table_index(t)] + 1 for t in range(n_slots))
    trips = (n_items + ATT_UNROLL - 2) // ATT_UNROLL
    j0 = jmin_ref[base]
    pe0, t0 = item_probs(jnp.int32(0), j0)
    pe, tp, jp, _, _ = lax.fori_loop(0, trips, body, (pe0, t0, j0) + advance(jnp.int32(0), j0))
    item_pv(pe, tp, jp)

    def meta_tile(t, carry):
        @pl.when(mflag_ref[table_index(t)] > 0)
        def _():
            s = jnp.where(mcol < n_meta, scores(t, kbm_ref[...], kcm_ref[...]), NEG_INF)
            acc_ref[t] += pv(t, jnp.exp2(s).astype(bf16), vbm_ref[...])
        return carry

    lax.fori_loop(0, n_slots, meta_tile, 0)

    for r in range(gq):
        a0 = acc_ref[r]
        a1 = acc_ref[gq + r]
        own = lane < HEAD_DIM
        sums = pltpu.roll(jnp.where(own, a1, a0), HEAD_DIM, axis=1)
        o_ref[0, r * tile:(r + 1) * tile, :] = (jnp.where(own, a0, a1) / sums).astype(bf16)


def _prompt_attention(q, qcat, kb, vb, kcat, kbm, vbm, kcm, jmin, mflag, n_meta):
    b, l, _ = q.shape
    tile = min(ATT_TILE, l)
    nq = l // tile
    gq = min(ATT_GROUP, nq)
    rows = gq * tile
    grid_spec = pltpu.PrefetchScalarGridSpec(
        num_scalar_prefetch=2,
        grid=(b, N_HEADS // 2, nq // gq),
        in_specs=[
            pl.BlockSpec((1, rows, LANES), lambda b, p, i, *_: (b, i, p)),
            pl.BlockSpec((1, rows, LANES), lambda b, p, i, *_: (b, i, 0)),
            pl.BlockSpec((1, LANES, l), lambda b, p, i, *_: (b, p, 0)),
            pl.BlockSpec((1, l, LANES), lambda b, p, i, *_: (b, 0, p)),
            pl.BlockSpec((1, LANES, l), lambda b, p, i, *_: (b, 0, 0)),
            pl.BlockSpec((LANES, LANES), lambda b, p, i, *_: (p, 0)),
            pl.BlockSpec((LANES, LANES), lambda b, p, i, *_: (0, p)),
            pl.BlockSpec((LANES, LANES), lambda b, p, i, *_: (0, 0)),
        ],
        out_specs=pl.BlockSpec((1, rows, LANES), lambda b, p, i, *_: (b, i, p)),
        scratch_shapes=[pltpu.VMEM((2 * gq, tile, LANES), f32), pltpu.VMEM((3, tile, tile), f32)],
    )
    return pl.pallas_call(
        functools.partial(_attn_kernel, tile, nq, gq, n_meta),
        grid_spec=grid_spec,
        out_shape=jax.ShapeDtypeStruct((b, l, D_ATT), bf16),
        compiler_params=_cparams("parallel", "parallel", "arbitrary"),
        name="prompt_attention",
    )(jmin, mflag, q, qcat, kb, vb, kcat, kbm, vbm, kcm)


def _rowshift_kernel(tile, nq, gq, n_meta, jmin_ref, mflag_ref, q_ref, qc_ref, kb_ref, kc_ref, kbm_ref, kcm_ref,
                     o_ref):
    b = pl.program_id(0)
    p = pl.program_id(1)
    g = pl.program_id(2)
    lane = lax.broadcasted_iota(jnp.int32, (1, LANES), 1)
    slot = lane % AUG
    mcol = lax.broadcasted_iota(jnp.int32, (tile, LANES), 1)
    col = lax.broadcasted_iota(jnp.int32, (tile, tile), 1)
    row = lax.broadcasted_iota(jnp.int32, (tile, tile), 0)

    def rows_of(j):
        return pl.ds(pl.multiple_of(j * tile, tile), tile)

    def row_max(x):
        return jnp.max(x, axis=1, keepdims=True)

    for hh in range(2):
        h = 2 * p + hh
        base = (b * N_HEADS + h) * nq + g * gq
        m_q = (lane // HEAD_DIM == hh).astype(bf16)
        mine = lane // AUG == h
        m_c = mine.astype(bf16)

        def per_tile(r, carry):
            i = g * gq + r
            q_op = jnp.concatenate([q_ref[0, rows_of(r), :] * m_q, qc_ref[0, rows_of(r), :] * m_c], axis=1)

            def scores(kt, kct):
                return _dot(q_op, jnp.concatenate([kt, kct], axis=0))

            m = row_max(jnp.where(col <= row, scores(kb_ref[0, :, rows_of(i)], kc_ref[0, :, rows_of(i)]), NEG_INF))
            m = lax.fori_loop(jmin_ref[base + r], i, lambda j, m: jnp.maximum(
                m, row_max(scores(kb_ref[0, :, rows_of(j)], kc_ref[0, :, rows_of(j)]))), m)
            mm = row_max(jnp.where(mcol < n_meta, scores(kbm_ref[...], kcm_ref[...]), NEG_INF))
            m = jnp.where(mflag_ref[base + r] > 0, jnp.maximum(m, mm), m)
            hi, mid, lo = _split3(jnp.broadcast_to(-m, (tile, LANES)))
            patch = jnp.where(mine & (slot == SHIFT_SLOT), hi, jnp.where(mine & (slot == SHIFT_SLOT + 1), mid,
                              jnp.where(mine & (slot == SHIFT_SLOT + 2), lo, 0.0))).astype(bf16)
            if hh == 0:
                o_ref[0, 0, rows_of(r), :] = patch
            else:
                o_ref[0, 0, rows_of(r), :] = o_ref[0, 0, rows_of(r), :] + patch
            return carry

        lax.fori_loop(0, gq, per_tile, 0)


def _row_shift(q, qcat, kb, kcat, kbm, kcm, jmin, mflag, n_meta):
    b, l, _ = q.shape
    tile = min(ATT_TILE, l)
    nq = l // tile
    gq = min(ATT_GROUP, nq)
    rows = gq * tile
    grid_spec = pltpu.PrefetchScalarGridSpec(
        num_scalar_prefetch=2,
        grid=(b, N_HEADS // 2, nq // gq),
        in_specs=[
            pl.BlockSpec((1, rows, LANES), lambda b, p, i, *_: (b, i, p)),
            pl.BlockSpec((1, rows, LANES), lambda b, p, i, *_: (b, i, 0)),
            pl.BlockSpec((1, LANES, l), lambda b, p, i, *_: (b, p, 0)),
            pl.BlockSpec((1, LANES, l), lambda b, p, i, *_: (b, 0, 0)),
            pl.BlockSpec((LANES, LANES), lambda b, p, i, *_: (p, 0)),
            pl.BlockSpec((LANES, LANES), lambda b, p, i, *_: (0, 0)),
        ],
        out_specs=pl.BlockSpec((1, 1, rows, LANES), lambda b, p, i, *_: (b, p, i, 0)),
    )
    return pl.pallas_call(
        functools.partial(_rowshift_kernel, tile, nq, gq, n_meta),
        grid_spec=grid_spec,
        out_shape=jax.ShapeDtypeStruct((b, N_HEADS // 2, l, LANES), bf16),
        compiler_params=_cparams("parallel", "parallel", "arbitrary"),
        name="row_shift",
    )(jmin, mflag, q, qcat, kb, kcat, kbm, kcm)


def _sattn_kernel(past, tn, q_ref, kn_ref, vn_ref, ckt_ref, cvt_ref, lft_ref, o_ref):
    for i in range(q_ref.shape[0]):
        _sattn_one(past, tn, i, q_ref, kn_ref, vn_ref, ckt_ref, cvt_ref, lft_ref, o_ref)


def _sattn_one(past, tn, i, q_ref, kn_ref, vn_ref, ckt_ref, cvt_ref, lft_ref, o_ref):
    s_len = lft_ref.shape[-1]
    rows = N_HEADS * tn
    c = lft_ref[i]
    lane_h = lax.broadcasted_iota(jnp.int32, c.shape, 1)
    sh = 1
    while sh < s_len:
        c = c + jnp.where(lane_h >= sh, pltpu.roll(c, sh, axis=1), 0.0)
        sh *= 2
    cexp = jnp.concatenate([jnp.broadcast_to(c[h:h + 1, :], (tn, s_len)) for h in range(N_HEADS)], axis=0)
    rix = lax.broadcasted_iota(jnp.int32, (rows, s_len), 0)
    lix = lax.broadcasted_iota(jnp.int32, (rows, s_len), 1)
    qpos = past + rix % tn
    cq = jnp.sum(jnp.where(lix == qpos, cexp, 0.0), axis=1, keepdims=True)
    pad = jnp.zeros((s_len - past - tn, D_ATT), bf16)
    k_new = jnp.concatenate([kn_ref[i], pad], axis=0)
    v_new = jnp.concatenate([vn_ref[i], pad], axis=0)
    qrep = jnp.concatenate([q_ref[i].astype(f32)] * N_HEADS, axis=0)
    r5 = lax.broadcasted_iota(jnp.int32, (rows, D_ATT), 0)
    l5 = lax.broadcasted_iota(jnp.int32, (rows, D_ATT), 1)
    qrows = jnp.where(l5 // HEAD_DIM == r5 // tn, qrep, 0.0).astype(bf16)
    s = jnp.concatenate([_dot(qrows, ckt_ref[i].astype(bf16)), _dot_nt(qrows, k_new)], axis=1)
    s = s + (cq - cexp) * LOG2E
    s = jnp.where(lix <= qpos, s, NEG_INF)
    m = jnp.max(s, axis=1, keepdims=True)
    pe = jnp.exp2(s - m)
    den = jnp.sum(pe, axis=1, keepdims=True)
    pb = pe.astype(bf16)
    o = (_dot_nt(pb[:, :past], cvt_ref[i].astype(bf16)) + _dot(pb[:, past:], v_new)) / den
    lo = lax.broadcasted_iota(jnp.int32, (tn, D_ATT), 1)
    out = jnp.zeros((tn, D_ATT), f32)
    for h in range(N_HEADS):
        out = out + jnp.where(lo // HEAD_DIM == h, o[h * tn:(h + 1) * tn, :], 0.0)
    o_ref[i] = out.astype(bf16)


def _sample_attention(q, kn, vn, ckt, cvt, lft):
    bs, tn, _ = q.shape
    past = ckt.shape[2]
    s_len = lft.shape[-1]
    per_step = 2 if bs % 2 == 0 else 1
    blk = lambda n, w: pl.BlockSpec((per_step, n, w), lambda b: (b, 0, 0))
    return pl.pallas_call(
        functools.partial(_sattn_kernel, past, tn),
        grid=(bs // per_step,),
        in_specs=[blk(tn, D_ATT), blk(tn, D_ATT), blk(tn, D_ATT), blk(D_ATT, past), blk(D_ATT, past),
                  blk(N_HEADS, s_len)],
        out_specs=blk(tn, D_ATT),
        out_shape=jax.ShapeDtypeStruct((bs, tn, D_ATT), bf16),
        compiler_params=_cparams("parallel"),
        name="sample_attention",
    )(q, kn, vn, ckt, cvt, lft)


def _s5prep_kernel(n_sq, are_ref, aim_ref, ldt_ref, btr_ref, bti_ref, cre_ref, cim_ref,
                   aqr_ref, aqi_ref, apr_ref, api_ref, wr_ref, wi_ref, ctr_ref, cti_ref, kk_ref):
    are = are_ref[...]
    aim = aim_ref[...]
    dt = jnp.exp(ldt_ref[...])
    mag = jnp.exp(are * dt)
    ph = aim * dt
    abr = mag * jnp.cos(ph)
    abi = mag * jnp.sin(ph)
    nr = abr - 1.0
    den = are * are + aim * aim
    fr = (nr * are + abi * aim) / den
    fi = (abi * are - nr * aim) / den
    pw = [(jnp.ones_like(abr), jnp.zeros_like(abr))]
    for _ in range(S5_Q):
        pr, pi = pw[-1]
        pw.append((pr * abr - pi * abi, pr * abi + pi * abr))
    aqr_ref[...], aqi_ref[...] = pw[S5_Q]
    pr, pi = pw[S5_Q]
    for _ in range(n_sq):
        pr, pi = pr * pr - pi * pi, 2.0 * pr * pi
    apr_ref[...] = pr
    api_ref[...] = pi
    p_states = are.shape[1]
    ident = (lax.broadcasted_iota(jnp.int32, (p_states, p_states), 0)
             == lax.broadcasted_iota(jnp.int32, (p_states, p_states), 1)).astype(bf16)
    lane = lax.broadcasted_iota(jnp.int32, (SSM_GROUP, S5_Q * SSM_GROUP), 1)
    hi = functools.partial(lax.dot_general, dimension_numbers=(((1,), (1,)), ((), ())),
                           precision=lax.Precision.HIGHEST, preferred_element_type=f32)
    for g in range(are.shape[0]):
        row = slice(g, g + 1)
        btr = btr_ref[g]
        bti = bti_ref[g]
        bbr = fr[row] * btr - fi[row] * bti
        bbi = fr[row] * bti + fi[row] * btr
        cre = cre_ref[g]
        cim = cim_ref[g]
        car_all, cai_all = [cre], [cim]
        for q in range(S5_Q):
            er, ei = pw[S5_Q - 1 - q]
            wr_ref[q, g] = er[row] * bbr - ei[row] * bbi
            wi_ref[q, g] = er[row] * bbi + ei[row] * bbr
            gr, gi = pw[q + 1]
            car_all.append(cre * gr[row] - cim * gi[row])
            cai_all.append(cre * gi[row] + cim * gr[row])
        ctr_ref[g] = _dot_nt(ident, jnp.concatenate(car_all[1:], axis=0).astype(bf16))
        cti_ref[g] = _dot_nt(ident, jnp.concatenate(cai_all[1:], axis=0).astype(bf16))
        kt = (hi(bbr, jnp.concatenate(car_all[:S5_Q], axis=0))
              - hi(bbi, jnp.concatenate(cai_all[:S5_Q], axis=0)))
        kk_ref[g] = jnp.concatenate(
            [kt if qi == 0 else jnp.where(lane >= qi * SSM_GROUP, pltpu.roll(kt, qi * SSM_GROUP, axis=1), 0.0)
             for qi in range(S5_Q)], axis=0)


def _s5_prep(a_re, a_im, log_dt, bt_re, bt_im, c_re, c_im, n_sq):
    g, p = a_re.shape
    sd = jax.ShapeDtypeStruct
    qh = S5_Q * SSM_GROUP
    return pl.pallas_call(
        functools.partial(_s5prep_kernel, n_sq),
        out_shape=[sd((g, p), f32)] * 4 + [sd((S5_Q,) + bt_re.shape, f32)] * 2 + [sd((g, p, qh), f32)] * 2
        + [sd((g, qh, qh), f32)],
        name="s5_prep",
    )(a_re, a_im, log_dt, bt_re, bt_im, c_re, c_im)


def _s5_kernel(chain, n_seg, n_sb, u_ref, w8_ref, toep_ref, cpow_ref, aqr_ref, aqi_ref, apr_ref, api_ref, d8_ref,
               x0r_ref, x0i_ref, y_ref, sr_ref, si_ref, xr_ref, xi_ref):
    aqr = aqr_ref[0]
    aqi = aqi_ref[0]
    w8 = w8_ref[0]
    toep = toep_ref[0]
    cpow = cpow_ref[0]
    d8 = d8_ref[0]

    def step(with_y, sb):
        rows = pl.ds(sb, n_seg, stride=n_sb) if n_sb > 1 else pl.ds(0, n_seg)
        uf = jnp.concatenate([u_ref[0, 0, q, rows, :] for q in range(S5_Q)], axis=1)
        ub = uf.astype(bf16)
        xr = xr_ref[...]
        xi = xi_ref[...]
        if with_y:
            xc = jnp.concatenate([xr, xi], axis=1).astype(bf16)
            y = _dot(ub, toep) + _dot(xc, cpow) + d8 * uf
            for q in range(S5_Q):
                y_ref[0, 0, q, rows, :] = y[:, q * LANES:(q + 1) * LANES]
        pj = _dot(ub, w8)
        xr_ref[...] = aqr * xr - aqi * xi + pj[:, :STATE_BLOCK]
        xi_ref[...] = aqr * xi + aqi * xr + pj[:, STATE_BLOCK:]

    if chain:
        xr_ref[...] = jnp.zeros(xr_ref.shape, f32)
        xi_ref[...] = jnp.zeros(xi_ref.shape, f32)
        for sb in range(n_sb):
            step(False, sb)
        apr = apr_ref[0]
        api = api_ref[0]

        def scan(c, carry):
            kr, ki = carry
            er = xr_ref[pl.ds(c, 1), :]
            ei = xi_ref[pl.ds(c, 1), :]
            xr_ref[pl.ds(c, 1), :] = kr
            xi_ref[pl.ds(c, 1), :] = ki
            return apr * kr - api * ki + er, apr * ki + api * kr + ei

        kr, ki = lax.fori_loop(0, n_seg, scan, (x0r_ref[0], x0i_ref[0]))
        sr_ref[0] = kr
        si_ref[0] = ki
        for sb in range(n_sb):
            step(True, sb)
    else:
        xr_ref[...] = x0r_ref[0]
        xi_ref[...] = x0i_ref[0]
        for sb in range(n_sb):
            step(True, sb)
        sr_ref[0] = xr_ref[...]
        si_ref[0] = xi_ref[...]


def _s5(u, x0r, x0i, weights, n_seg, chain, name):
    bq, _, _, m, _ = u.shape
    n_sb = m // n_seg
    r0 = x0r.shape[1]
    per_k = lambda a: pl.BlockSpec((1,) + a.shape[1:], lambda b, k: (k, 0, 0))
    st = pl.BlockSpec((1, r0, STATE_BLOCK), lambda b, k: (b, 0, k))
    tok = lambda: pl.BlockSpec((1, 1, S5_Q, m, LANES), lambda b, k: (b, k, 0, 0, 0))
    sd = jax.ShapeDtypeStruct
    return pl.pallas_call(
        functools.partial(_s5_kernel, chain, n_seg, n_sb),
        grid=(bq, N_SSM_BLOCKS),
        in_specs=[tok()] + [per_k(a) for a in weights] + [st, st],
        out_specs=[tok(), st, st],
        out_shape=[sd(u.shape, f32), sd(x0r.shape, f32), sd(x0i.shape, f32)],
        scratch_shapes=[pltpu.VMEM((n_seg, STATE_BLOCK), f32)] * 2,
        compiler_params=_cparams("parallel", "parallel"),
        name=name,
    )(u, *weights, x0r, x0i)


def _out_kernel(x_ref, att_ref, ga_ref, ys_ref, gs_ref, wg_ref, bg_ref, wo_ref, y_ref, ys_scr):
    ga = ga_ref[0].astype(f32)
    a = att_ref[0].astype(f32) * (ga * jax.nn.sigmoid(ga))
    rows = x_ref.shape[1]
    for kb in range(N_SSM_BLOCKS):
        for qq in range(S5_Q):
            ys_scr[kb, pl.ds(qq, rows // S5_Q, stride=S5_Q), :] = ys_ref[0, kb, qq]
    ys = jnp.concatenate([ys_scr[kb] for kb in range(N_SSM_BLOCKS)], axis=1)
    z = 0.5 * ys * (1.0 + jnp.tanh(0.7978845608028654 * (ys + 0.044715 * (ys * ys * ys))))
    gate = jax.nn.sigmoid(_dot(z.astype(bf16), wg_ref[...]) + bg_ref[...])
    gs = gs_ref[0].astype(f32)
    s5 = z * gate * (gs * jax.nn.sigmoid(gs))
    cat = jnp.concatenate([a, s5], axis=1).astype(bf16)
    y_ref[0] = x_ref[0] + _dot(cat, wo_ref[...])


def _outproj(x, att, ga, ys, gs, wg, bg, wo, tile, name):
    bx, lx, d = x.shape
    full = lambda a: pl.BlockSpec(a.shape, lambda b, j: (0,) * a.ndim)
    row = lambda w: pl.BlockSpec((1, tile, w), lambda b, j: (b, j, 0))
    return pl.pallas_call(
        _out_kernel,
        grid=(bx, lx // tile),
        in_specs=[row(d), row(D_ATT), row(D_ATT), _s5_tokens_spec(tile, lambda b, j: (b, 0, 0, j, 0)), row(D_SSM),
                  full(wg), full(bg), full(wo)],
        out_specs=row(d),
        out_shape=jax.ShapeDtypeStruct(x.shape, f32),
        scratch_shapes=[pltpu.VMEM((N_SSM_BLOCKS, tile, LANES), f32)],
        compiler_params=_cparams("parallel", "parallel"),
        name=name,
    )(x, att, ga, ys, gs, wg, bg, wo)


def _lower_tri(n):
    r = jnp.arange(n)
    return (r[None, :] <= r[:, None]).astype(bf16)


def kernel(x_prompt, x_sample, cache_k, cache_v, cache_logf, state_s5_re, state_s5_im, meta_tokens, norm_g,
           w_in, b_f, q_norm_g, k_norm_g, s5_a_re, s5_a_im, s5_log_dt, s5_b_re, s5_b_im, s5_c_re, s5_c_im,
           s5_d, w_glu, b_glu, w_out):
    assert norm_g.shape[0] == 1, "single-layer step"
    bp, lp, d = x_prompt.shape
    bs, ts, _ = x_sample.shape
    n_meta = meta_tokens.shape[0]
    past = cache_k.shape[2]
    G, P = N_GROUPS, SSM_STATE

    w = w_in[0]
    o3 = 3 * D_ATT
    o4 = o3 + N_HEADS
    wm = jnp.concatenate([w[:, :o3], w[:, o4:]], axis=1).astype(bf16)
    wf = jnp.repeat(w[:, o3:o4], AUG, axis=1).astype(bf16)
    bfr = jnp.repeat(b_f[0], AUG)[None, :].astype(f32)
    g_in = norm_g[0][None, :].astype(f32)
    qg = (jnp.tile(q_norm_g[0], N_HEADS) * (HEAD_DIM ** -0.5 * LOG2E))[None, :].astype(f32)
    kg = jnp.tile(k_norm_g[0], N_HEADS)[None, :].astype(f32)
    hid = jnp.arange(D_ATT) // HEAD_DIM
    eh = ((hid[:, None] == hid[None, :]).astype(f32) / HEAD_DIM).astype(bf16)
    sel = (jnp.arange(LANES)[None, :] == AUG * jnp.arange(N_HEADS)[:, None]).astype(bf16)
    m_bound = 8.0 * jnp.max(jnp.abs(q_norm_g[0])) * jnp.max(jnp.abs(k_norm_g[0])) * 1.02
    robust = m_bound > SHIFT_SAFE
    negm = jnp.full((1, LANES), -LOG2E, f32) * jnp.where(robust, 0.0, m_bound)
    params = (g_in, wm, wf, bfr, qg, kg, eh, negm, sel)

    tm = n_meta
    meta = _inproj(meta_tokens[None].astype(f32), jnp.zeros((1, 1, LANES), f32), params, tm, _lower_tri(tm), True,
                   "inproj_meta")
    (_, kb_m, vb_m, kf_m, vf_m, lft_m, edge_m, _, kcat_m, _, u_m, _) = meta
    cum_meta_end = edge_m[:, -1, 1:2, :]
    c0 = jnp.broadcast_to(cum_meta_end, (bp, 1, LANES))
    tile_p = min(ROW_TILE, lp)
    tb_p = min(LANES, tile_p)
    meta_cols = lambda a: jnp.pad(jnp.swapaxes(a, 1, 2), ((0, 0), (0, 0), (0, LANES - tm)))
    (q_p, kb_p, vb_p, kt_p, vt_p, lft_p, edge_p, qcat_p, kcat_p, ga_p, u_p, gs_p) = _inproj(
        x_prompt, c0, params, tile_p, _lower_tri(tb_p), True, "inproj_prompt",
        meta_t=(meta_cols(kf_m), meta_cols(vf_m), n_meta))
    ls = bs * ts
    tile_s = min(ROW_TILE, ls)
    (q_s, kb_s, vb_s, kf_s, vf_s, lft_s, _, _, _, ga_s, u_s, gs_s) = _inproj(
        x_sample.reshape(1, ls, d), jnp.zeros((1, 1, LANES), f32), params, tile_s,
        _lower_tri(min(LANES, tile_s)), False, "inproj_sample")

    tile_a = min(ATT_TILE, lp)
    nq = lp // tile_a
    per_tile = tile_a // tb_p
    edges = edge_p[:, :, :2, ::AUG]
    q_start = edges[:, 0::per_tile, 0, :]
    k_end = edges[:, per_tile - 1::per_tile, 1, :]
    dmax = q_start[:, :, None, :] - k_end[:, None, :, :]
    padm = lambda a: jnp.pad(a[0], ((0, LANES - tm), (0, 0)))
    kbm, vbm, kcm = padm(kb_m).T, padm(vb_m), padm(kcat_m).T

    def skip_tables(thr):
        jmin = jnp.sum((dmax < -thr).astype(jnp.int32), axis=2)
        jmin = jnp.minimum(jmin, jnp.arange(nq, dtype=jnp.int32)[None, :, None])
        mflag = (q_start - cum_meta_end[:, :, ::AUG] >= -thr).astype(jnp.int32)
        return jnp.transpose(jmin, (0, 2, 1)).reshape(-1), jnp.transpose(mflag, (0, 2, 1)).reshape(-1)

    def attend_bounded():
        return _prompt_attention(q_p, qcat_p, kb_p, vb_p, kcat_p, kbm, vbm, kcm, *skip_tables(EXP_ZERO), n_meta)

    def attend_robust():
        tables = skip_tables(EXP_ZERO + 2.0 * m_bound)
        shift = _row_shift(q_p, qcat_p, kb_p, kcat_p, kbm, kcm, *tables, n_meta)
        qcat_r = (qcat_p.astype(f32) + jnp.sum(shift.astype(f32), axis=1)).astype(bf16)
        return _prompt_attention(q_p, qcat_r, kb_p, vb_p, kcat_p, kbm, vbm, kcm, *tables, n_meta)

    att_p = lax.cond(robust, attend_robust, attend_bounded)

    lft_new = jnp.transpose(lft_s.reshape(N_HEADS, bs, ts), (1, 0, 2))
    s_len = -(-(past + ts) // LANES) * LANES
    lft = jnp.concatenate([jnp.transpose(cache_logf[0].astype(f32), (0, 2, 1)), lft_new,
                           jnp.zeros((bs, N_HEADS, s_len - past - ts), f32)], axis=2)
    cache_t = lambda c: jnp.transpose(c[0], (0, 2, 3, 1)).reshape(bs, D_ATT, past)
    att_s = _sample_attention(q_s.reshape(bs, ts, D_ATT), kb_s.reshape(bs, ts, D_ATT), vb_s.reshape(bs, ts, D_ATT),
                              cache_t(cache_k), cache_t(cache_v), lft)

    n_seg_p = min(PROMPT_SEGMENTS, lp // S5_Q)
    n_sb_p = lp // (S5_Q * n_seg_p)
    n_sq = n_sb_p.bit_length() - 1
    assert n_sb_p == 1 << n_sq and ts % S5_Q == 0
    tr = lambda a: jnp.transpose(a[0], (0, 2, 1)).astype(f32)
    aqr, aqi, apr, api, wr, wi, ctr, cti, kk = _s5_prep(
        s5_a_re[0].astype(f32), s5_a_im[0].astype(f32), s5_log_dt[0][:, None].astype(f32),
        tr(s5_b_re), tr(s5_b_im), s5_c_re[0].astype(f32), s5_c_im[0].astype(f32), n_sq)
    NB, G8, Q = N_SSM_BLOCKS, GROUPS_PER_BLOCK, S5_Q
    qgh = Q * LANES

    def spread(x, col_of_src, row_group, col_group):
        n_src = x.shape[-1]
        n_rows = x.shape[1]
        n_cols = col_of_src.shape[0]
        sel = (col_of_src[None, :] == jnp.arange(n_src)[:, None]).astype(bf16)
        keep = row_group(jnp.arange(n_rows))[:, None] == col_group(jnp.arange(n_cols))[None, :]
        y = jnp.einsum("krs,sc->krc", x.astype(bf16), sel, preferred_element_type=f32)
        return jnp.where(keep[None], y, 0.0).astype(bf16)

    cols_state = jnp.arange(STATE_BLOCK) % P
    cols_qgh = (jnp.arange(qgh) // LANES) * SSM_GROUP + jnp.arange(qgh) % SSM_GROUP
    group_of_qgh = lambda i: (i // SSM_GROUP) % G8
    by_block = lambda a: jnp.transpose(a.reshape(Q, NB, G8 * SSM_GROUP, P), (1, 0, 2, 3)).reshape(NB, qgh, P)
    w_part = lambda a: spread(by_block(a), cols_state, group_of_qgh, lambda i: i // P)
    w8 = jnp.concatenate([w_part(wr), w_part(wi)], axis=2)
    c_part = lambda a: spread(a.reshape(NB, STATE_BLOCK, Q * SSM_GROUP), cols_qgh, lambda i: i // P, group_of_qgh)
    cpow = jnp.concatenate([c_part(ctr), c_part(-cti)], axis=1)
    kk_rows = jnp.transpose(kk.reshape(NB, G8, Q, SSM_GROUP, Q * SSM_GROUP), (0, 2, 1, 3, 4)).reshape(NB, qgh, -1)
    toep = spread(kk_rows, cols_qgh, group_of_qgh, group_of_qgh)
    lane_k = lambda a: a.reshape(NB, 1, -1)
    d8 = jnp.tile(lane_k(s5_d[0].astype(f32)), (1, 1, Q))
    weights = lambda pr_, pi_: (w8, toep, cpow, lane_k(aqr), lane_k(aqi), lane_k(pr_), lane_k(pi_), d8)

    assert tm == ts
    n_small = -(-(bs + 1) // SUBLANES) * SUBLANES
    u_small = jnp.concatenate([u_s, u_m, jnp.zeros(_s5_tokens_shape(1, (n_small - bs - 1) * ts), f32)], axis=3)
    pad_state = lambda a: jnp.pad(a[0].reshape(bs, G * P).astype(f32), ((0, n_small - bs), (0, 0)))[None]
    ys_small, sr_small, si_small = _s5(u_small, pad_state(state_s5_re), pad_state(state_s5_im),
                                       weights(aqr, aqi), n_small, False, "s5_short")
    x0r = jnp.broadcast_to(sr_small[:, bs:bs + 1, :], (bp, 1, G * P))
    x0i = jnp.broadcast_to(si_small[:, bs:bs + 1, :], (bp, 1, G * P))
    ys_p, sr_p, si_p = _s5(u_p, x0r, x0i, weights(apr, api), n_seg_p, True, "s5_prompt")

    wg = w_glu[0].astype(bf16)
    bg = b_glu[0][None, :].astype(f32)
    wo = w_out[0].astype(bf16)
    y_prompt = _outproj(x_prompt, att_p, ga_p, ys_p, gs_p, wg, bg, wo, min(OUT_TILE, lp), "outproj_prompt")
    y_sample = _outproj(x_sample.reshape(1, ls, d), att_s.reshape(1, ls, D_ATT), ga_s, ys_small[:, :, :, :ls // S5_Q],
                        gs_s, wg, bg, wo, tile_s, "outproj_sample").reshape(bs, ts, d)

    heads_t = lambda a: jnp.transpose(a.reshape(bp, N_HEADS, HEAD_DIM, n_meta + lp), (0, 3, 1, 2))[None]
    new_k_prompt = heads_t(kt_p)
    new_v_prompt = heads_t(vt_p)
    lft_all = jnp.concatenate([jnp.broadcast_to(lft_m, (bp, N_HEADS, n_meta)), lft_p], axis=2)
    new_logf_prompt = jnp.transpose(lft_all, (0, 2, 1))[None]
    new_s5_re_prompt = sr_p.reshape(1, bp, G, P)
    new_s5_im_prompt = si_p.reshape(1, bp, G, P)
    new_k_sample = kf_s.reshape(1, bs, ts, N_HEADS, HEAD_DIM)
    new_v_sample = vf_s.reshape(1, bs, ts, N_HEADS, HEAD_DIM)
    new_logf_sample = jnp.transpose(lft_new, (0, 2, 1))[None]
    new_s5_re_sample = sr_small[:, :bs].reshape(1, bs, G, P)
    new_s5_im_sample = si_small[:, :bs].reshape(1, bs, G, P)
    return (y_prompt, y_sample, new_k_prompt, new_v_prompt, new_logf_prompt, new_s5_re_prompt, new_s5_im_prompt,
            new_k_sample, new_v_sample, new_logf_sample, new_s5_re_sample, new_s5_im_sample)
```

```python
import functools

import jax
import jax.numpy as jnp
from jax import lax
from jax.experimental import pallas as pl
from jax.experimental.pallas import tpu as pltpu

N_HEADS = 8
HEAD_DIM = 64
D_ATT = N_HEADS * HEAD_DIM
SSM_GROUP = 16
SSM_STATE = 64
D_SSM = 512
N_GROUPS = D_SSM // SSM_GROUP
EPS = 1e-6
NEG_INF = -1e30

LANES = 128
SUBLANES = 8
AUG = LANES // N_HEADS
SHIFT_SLOT = 7
GROUPS_PER_BLOCK = LANES // SSM_GROUP
STATE_BLOCK = GROUPS_PER_BLOCK * SSM_STATE
N_SSM_BLOCKS = D_SSM // LANES
VMEM_LIMIT = 56 * 1024 * 1024
EXP_ZERO = 105.0
SHIFT_SAFE = 32.0

ROW_TILE = 512
OUT_TILE = 1024
ATT_TILE = 256
ATT_GROUP = 32
ATT_UNROLL = 32
ATT_TAIL_UNROLL = 8
LOG2E = 1.4426950408889634
PROMPT_SEGMENTS = 512
S5_Q = 8

bf16 = jnp.bfloat16
f32 = jnp.float32


def _cparams(*sem):
    return pltpu.CompilerParams(dimension_semantics=sem, vmem_limit_bytes=VMEM_LIMIT)


def _split3(x):
    hi = x.astype(bf16).astype(f32)
    r1 = x - hi
    mid = r1.astype(bf16).astype(f32)
    lo = (r1 - mid).astype(bf16).astype(f32)
    return hi, mid, lo


def _dot(a, b):
    return jnp.dot(a, b, preferred_element_type=f32)


def _dot_nt(a, b):
    return lax.dot_general(a, b, (((1,), (1,)), ((), ())), preferred_element_type=f32)


def _s5_tokens_shape(b, l):
    return (b, N_SSM_BLOCKS, S5_Q, l // S5_Q, LANES)


def _s5_tokens_spec(rows, index_map, **kw):
    return pl.BlockSpec((1, N_SSM_BLOCKS, S5_Q, rows // S5_Q, LANES), index_map, **kw)


def _inproj_kernel(chain, n_blk, join, *refs):
    n_in = 12
    if chain:
        c0_ref, carry_ref = refs[1], refs[-5 if join else -2]

        @pl.when(pl.program_id(1) == 0)
        def _():
            carry_ref[...] = c0_ref[0]
    if not join:
        _inproj_tile(chain, n_blk, False, *refs)
        return
    kmt_ref, vmt_ref = refs[n_in:n_in + 2]
    refs = refs[:n_in] + refs[n_in + 2:]
    kb_ref, kf_ref, vf_ref, kcat_ref = refs[n_in + 1], refs[n_in + 3], refs[n_in + 4], refs[n_in + 8]
    prev_ref, prevc_ref, tails_ref = refs[-3:]
    j = pl.program_id(1)
    n_tiles = pl.num_programs(1) - 2
    lane_t = lax.broadcasted_iota(jnp.int32, (D_ATT, LANES), 1)

    def emit():
        kb_ref[0] = prev_ref[0].T.astype(bf16)
        kcat_ref[0] = prevc_ref[...].T.astype(bf16)
        for idx, t_ref in enumerate((kf_ref, vf_ref)):
            rolled = pltpu.roll(prev_ref[idx].T, join, axis=1)
            first = jnp.where(lane_t < join, tails_ref[idx], rolled[:, :LANES])
            t_ref[0] = first if rolled.shape[1] == LANES else jnp.concatenate([first, rolled[:, LANES:]], axis=1)
            tails_ref[idx] = jnp.where(j > 0, rolled[:, :LANES], tails_ref[idx])

    @pl.when(j == 0)
    def _():
        tails_ref[0] = kmt_ref[0]
        tails_ref[1] = vmt_ref[0]
        prev_ref[...] = jnp.zeros(prev_ref.shape, f32)
        prevc_ref[...] = jnp.zeros(prevc_ref.shape, f32)

    @pl.when(j < n_tiles)
    def _():
        emit()
        _inproj_tile(chain, n_blk, True, *refs)

    pl.when(j == n_tiles)(emit)

    @pl.when(j == n_tiles + 1)
    def _():
        for idx, t_ref in enumerate((kf_ref, vf_ref)):
            t_ref[0] = jnp.zeros(t_ref.shape[1:], f32)
            t_ref[0, :, :LANES] = tails_ref[idx]


def _inproj_tile(chain, n_blk, join, x_ref, c0_ref, g_ref, wm_ref, wf_ref, bf_ref, qg_ref, kg_ref, eh_ref,
                 tri_ref, negm_ref, sel_ref, q_ref, kb_ref, vb_ref, kf_ref, vf_ref, lft_ref, edge_ref,
                 qcat_ref, kcat_ref, ga_ref, u_ref, gs_ref, carry_ref, us_ref, prev_ref=None, prevc_ref=None,
                 tails_ref=None):
    x = x_ref[0]
    ms = jnp.mean(x * x, axis=-1, keepdims=True)
    xn = (x * lax.rsqrt(ms + EPS) * g_ref[...]).astype(bf16)
    z = _dot(xn, wm_ref[...])
    q = z[:, 0 * D_ATT:1 * D_ATT]
    k = z[:, 1 * D_ATT:2 * D_ATT]
    v = z[:, 2 * D_ATT:3 * D_ATT]
    eh = eh_ref[...]
    qn = q * lax.rsqrt(_dot((q * q).astype(bf16), eh) + EPS) * qg_ref[...]
    kn = k * lax.rsqrt(_dot((k * k).astype(bf16), eh) + EPS) * kg_ref[...]
    q_ref[0] = qn.astype(bf16)
    vb_ref[0] = v.astype(bf16)
    if join:
        prev_ref[0] = kn
        prev_ref[1] = v
    else:
        kb_ref[0] = kn.astype(bf16)
        kf_ref[0] = kn
        vf_ref[0] = v
    ga_ref[0] = z[:, 3 * D_ATT:4 * D_ATT].astype(bf16)
    gs_ref[0] = z[:, 4 * D_ATT + D_SSM:].astype(bf16)
    for kb in range(N_SSM_BLOCKS):
        us_ref[kb] = z[:, 4 * D_ATT + kb * LANES:4 * D_ATT + (kb + 1) * LANES]
    for kb in range(N_SSM_BLOCKS):
        for qq in range(S5_Q):
            u_ref[0, kb, qq] = us_ref[kb, pl.ds(qq, x.shape[0] // S5_Q, stride=S5_Q), :]

    zf = _dot(xn, wf_ref[...]) + bf_ref[...]
    lf = jnp.minimum(zf, 0.0) - jnp.log1p(jnp.exp(-jnp.abs(zf)))
    sel = sel_ref[...]
    lft_ref[0] = sum(_dot_nt(sel, part.astype(bf16)) for part in _split3(lf))

    carry = carry_ref[...] if chain else jnp.zeros((1, LANES), f32)
    tri = tri_ref[...]
    tb = x.shape[0] // n_blk
    cums = []
    for r in range(n_blk):
        hi, mid, lo = _split3(lf[r * tb:(r + 1) * tb])
        c = _dot(tri, hi.astype(bf16)) + _dot(tri, mid.astype(bf16)) + _dot(tri, lo.astype(bf16)) + carry
        cums.append(c)
        carry = c[tb - 1:tb, :]
        edge_ref[0, r] = jnp.concatenate([c[0:1, :], carry, jnp.zeros((SUBLANES - 2, LANES), f32)], axis=0)
    cum = cums[0] if n_blk == 1 else jnp.concatenate(cums, axis=0)
    if chain:
        carry_ref[...] = carry

    slot = lax.broadcasted_iota(jnp.int32, (1, LANES), 1) % AUG
    hi, mid, lo = _split3(cum * LOG2E)
    qc = jnp.where(slot == 0, hi, jnp.where(slot == 1, mid, jnp.where(slot == 2, lo,
                   jnp.where(slot <= 6, 1.0, 0.0))))
    kc = jnp.where(slot <= 2, 1.0, jnp.where(slot == 3, -hi, jnp.where(slot == 4, -mid,
                   jnp.where(slot == 5, -lo, jnp.where(slot == 6, negm_ref[...],
                             jnp.where(slot <= SHIFT_SLOT + 2, 1.0, 0.0))))))
    qcat_ref[0] = qc.astype(bf16)
    if join:
        prevc_ref[...] = kc
    else:
        kcat_ref[0] = kc.astype(bf16)


def _inproj(x, c0, params, tile, tri, chain, name, meta_t=None):
    bx, lx, d = x.shape
    n_tiles = lx // tile
    n_blk = tile // tri.shape[0]
    join = 0 if meta_t is None else meta_t[2]
    last = n_tiles - 1
    tj = (lambda j: jnp.minimum(j, last)) if join else (lambda j: j)
    tprev = lambda j: jnp.maximum(j - 1, 0)
    full = lambda a: pl.BlockSpec(a.shape, lambda b, j: (0,) * a.ndim)
    row = lambda w: pl.BlockSpec((1, tile, w), lambda b, j: (b, tj(j), 0))
    sd = jax.ShapeDtypeStruct
    rows = lambda w, t: (row(w), sd((bx, lx, w), t))
    kv = (rows(D_ATT, f32) if not join else
          (pl.BlockSpec((1, D_ATT, tile), lambda b, j: (b, 0, tprev(j))), sd((bx, D_ATT, join + lx), f32)))
    cols = lambda w: (pl.BlockSpec((1, w, tile), lambda b, j: (b, 0, jnp.minimum(tprev(j), last))), sd((bx, w, lx), bf16))
    outs = [rows(D_ATT, bf16), cols(D_ATT) if join else rows(D_ATT, bf16), rows(D_ATT, bf16), kv, kv,
            (pl.BlockSpec((1, N_HEADS, tile), lambda b, j: (b, 0, tj(j))), sd((bx, N_HEADS, lx), f32)),
            (pl.BlockSpec((1, n_blk, SUBLANES, LANES), lambda b, j: (b, tj(j), 0, 0)),
             sd((bx, n_tiles * n_blk, SUBLANES, LANES), f32)),
            rows(LANES, bf16), cols(LANES) if join else rows(LANES, bf16), rows(D_ATT, bf16),
            (_s5_tokens_spec(tile, lambda b, j: (b, 0, 0, tj(j), 0)), sd(_s5_tokens_shape(bx, lx), f32)),
            rows(D_SSM, bf16)]
    g_in, wm, wf, bfr, qg, kg, eh, negm, sel = params
    operands = (g_in, wm, wf, bfr, qg, kg, eh, tri, negm, sel) + (tuple(meta_t[:2]) if join else ())
    scratch = [pltpu.VMEM((1, LANES), f32), pltpu.VMEM((N_SSM_BLOCKS, tile, LANES), f32)]
    if join:
        scratch += [pltpu.VMEM((2, tile, D_ATT), f32), pltpu.VMEM((tile, LANES), f32), pltpu.VMEM((2, D_ATT, LANES), f32)]
    return pl.pallas_call(
        functools.partial(_inproj_kernel, chain, n_blk, join),
        grid=(bx, n_tiles + (2 if join else 0)),
        in_specs=[row(d), pl.BlockSpec((1, 1, LANES), lambda b, j: (b, 0, 0))] + [full(a) for a in operands],
        out_specs=[o[0] for o in outs],
        out_shape=[o[1] for o in outs],
        scratch_shapes=scratch,
        compiler_params=_cparams("parallel", "arbitrary"),
        name=name,
    )(x, c0, *operands)


def _attn_kernel(tile, nq, gq, n_meta, jmin_ref, mflag_ref, q_ref, qc_ref, kb_ref, vb_ref, kc_ref,
                 kbm_ref, vbm_ref, kcm_ref, o_ref, acc_ref, bnd_ref):
    b = pl.program_id(0)
    p = pl.program_id(1)
    g = pl.program_id(2)
    lane = lax.broadcasted_iota(jnp.int32, (1, LANES), 1)
    mcol = lax.broadcasted_iota(jnp.int32, (tile, LANES), 1)
    col = lax.broadcasted_iota(jnp.int32, (tile, tile), 1)
    row = lax.broadcasted_iota(jnp.int32, (tile, tile), 0)

    @pl.when(g == 0)
    def _():
        bnd_ref[0] = jnp.where(col <= row, -NEG_INF, NEG_INF)
        bnd_ref[1] = jnp.full((tile, tile), -NEG_INF, f32)
        bnd_ref[2] = jnp.full((tile, tile), NEG_INF, f32)

    def rows_of(j):
        return pl.ds(pl.multiple_of(j * tile, tile), tile)

    n_slots = 2 * gq
    base = (b * N_HEADS + 2 * p) * nq + g * gq

    def table_index(t):
        return base + (t // gq) * nq + t % gq

    def clear(t, carry):
        acc_ref[t] = jnp.zeros(acc_ref.shape[1:], f32)
        return carry

    lax.fori_loop(0, n_slots, clear, 0)

    def head_masks(hh):
        own = lane // HEAD_DIM == hh
        return own.astype(bf16), (~own).astype(bf16), (lane // AUG == 2 * p + hh).astype(bf16)

    def scores(tc, kt, kct):
        m_q, _, m_c = head_masks(tc // gq)
        rows = rows_of(tc % gq)
        q_op = jnp.concatenate([q_ref[0, rows, :] * m_q, qc_ref[0, rows, :] * m_c], axis=1)
        return _dot(q_op, jnp.concatenate([kt, kct], axis=0))

    def pv(tc, pe, vs):
        m_q, m_o, _ = head_masks(tc // gq)
        return _dot(pe, vs * m_q + m_o)

    def item_probs(t, j):
        tc = jnp.minimum(t, n_slots - 1)
        kind = jnp.where(t < n_slots, jnp.where(j == g * gq + tc % gq, 0, 1), 2)
        s = scores(tc, kb_ref[0, :, rows_of(j)], kc_ref[0, :, rows_of(j)])
        return jnp.exp2(jnp.minimum(s, bnd_ref[kind])).astype(bf16), tc

    def item_pv(pe, tc, j):
        acc_ref[tc] += pv(tc, pe, vb_ref[0, rows_of(j), :])

    def advance(t, j):
        last = jnp.logical_and(t < n_slots, j == g * gq + t % gq)
        t2 = jnp.where(last, t + 1, t)
        j2 = jnp.where(last, jmin_ref[table_index(jnp.minimum(t + 1, n_slots - 1))],
                       jnp.where(t < n_slots, j + 1, j))
        return t2.astype(jnp.int32), j2.astype(jnp.int32)

    def body(unroll, trip, carry):
        pe, tp, jp, t, j = carry
        for _ in range(unroll):
            pn, tn = item_probs(t, j)
            item_pv(pe, tp, jp)
            pe, tp, jp = pn, tn, j
            t, j = advance(t, j)
        return pe, tp, jp, t, j

    n_items = sum(g * gq + t % gq - jmin_ref[table_index(t)] + 1 for t in range(n_slots))
    long_trips = (n_items - 1) // ATT_UNROLL
    short_trips = (n_items - 1 - long_trips * ATT_UNROLL + ATT_TAIL_UNROLL - 1) // ATT_TAIL_UNROLL
    j0 = jmin_ref[base]
    pe0, t0 = item_probs(jnp.int32(0), j0)
    carry = (pe0, t0, j0) + advance(jnp.int32(0), j0)
    carry = lax.fori_loop(0, long_trips, functools.partial(body, ATT_UNROLL), carry)
    pe, tp, jp, _, _ = lax.fori_loop(0, short_trips, functools.partial(body, ATT_TAIL_UNROLL), carry)
    item_pv(pe, tp, jp)

    def meta_tile(t, carry):
        @pl.when(mflag_ref[table_index(t)] > 0)
        def _():
            s = jnp.where(mcol < n_meta, scores(t, kbm_ref[...], kcm_ref[...]), NEG_INF)
            acc_ref[t] += pv(t, jnp.exp2(s).astype(bf16), vbm_ref[...])
        return carry

    lax.fori_loop(0, n_slots, meta_tile, 0)

    for r in range(gq):
        a0 = acc_ref[r]
        a1 = acc_ref[gq + r]
        own = lane < HEAD_DIM
        sums = pltpu.roll(jnp.where(own, a1, a0), HEAD_DIM, axis=1)
        o_ref[0, r * tile:(r + 1) * tile, :] = (jnp.where(own, a0, a1) / sums).astype(bf16)


def _prompt_attention(q, qcat, kb, vb, kcat, kbm, vbm, kcm, jmin, mflag, n_meta):
    b, l, _ = q.shape
    tile = min(ATT_TILE, l)
    nq = l // tile
    gq = min(ATT_GROUP, nq)
    rows = gq * tile
    grid_spec = pltpu.PrefetchScalarGridSpec(
        num_scalar_prefetch=2,
        grid=(b, N_HEADS // 2, nq // gq),
        in_specs=[
            pl.BlockSpec((1, rows, LANES), lambda b, p, i, *_: (b, i, p)),
            pl.BlockSpec((1, rows, LANES), lambda b, p, i, *_: (b, i, 0)),
            pl.BlockSpec((1, LANES, l), lambda b, p, i, *_: (b, p, 0)),
            pl.BlockSpec((1, l, LANES), lambda b, p, i, *_: (b, 0, p)),
            pl.BlockSpec((1, LANES, l), lambda b, p, i, *_: (b, 0, 0)),
            pl.BlockSpec((LANES, LANES), lambda b, p, i, *_: (p, 0)),
            pl.BlockSpec((LANES, LANES), lambda b, p, i, *_: (0, p)),
            pl.BlockSpec((LANES, LANES), lambda b, p, i, *_: (0, 0)),
        ],
        out_specs=pl.BlockSpec((1, rows, LANES), lambda b, p, i, *_: (b, i, p)),
        scratch_shapes=[pltpu.VMEM((2 * gq, tile, LANES), f32), pltpu.VMEM((3, tile, tile), f32)],
    )
    return pl.pallas_call(
        functools.partial(_attn_kernel, tile, nq, gq, n_meta),
        grid_spec=grid_spec,
        out_shape=jax.ShapeDtypeStruct((b, l, D_ATT), bf16),
        compiler_params=_cparams("parallel", "parallel", "arbitrary"),
        name="prompt_attention",
    )(jmin, mflag, q, qcat, kb, vb, kcat, kbm, vbm, kcm)


def _rowshift_kernel(tile, nq, gq, n_meta, jmin_ref, mflag_ref, q_ref, qc_ref, kb_ref, kc_ref, kbm_ref, kcm_ref,
                     o_ref):
    b = pl.program_id(0)
    p = pl.program_id(1)
    g = pl.program_id(2)
    lane = lax.broadcasted_iota(jnp.int32, (1, LANES), 1)
    slot = lane % AUG
    mcol = lax.broadcasted_iota(jnp.int32, (tile, LANES), 1)
    col = lax.broadcasted_iota(jnp.int32, (tile, tile), 1)
    row = lax.broadcasted_iota(jnp.int32, (tile, tile), 0)

    def rows_of(j):
        return pl.ds(pl.multiple_of(j * tile, tile), tile)

    def row_max(x):
        return jnp.max(x, axis=1, keepdims=True)

    for hh in range(2):
        h = 2 * p + hh
        base = (b * N_HEADS + h) * nq + g * gq
        m_q = (lane // HEAD_DIM == hh).astype(bf16)
        mine = lane // AUG == h
        m_c = mine.astype(bf16)

        def per_tile(r, carry):
            i = g * gq + r
            q_op = jnp.concatenate([q_ref[0, rows_of(r), :] * m_q, qc_ref[0, rows_of(r), :] * m_c], axis=1)

            def scores(kt, kct):
                return _dot(q_op, jnp.concatenate([kt, kct], axis=0))

            m = row_max(jnp.where(col <= row, scores(kb_ref[0, :, rows_of(i)], kc_ref[0, :, rows_of(i)]), NEG_INF))
            m = lax.fori_loop(jmin_ref[base + r], i, lambda j, m: jnp.maximum(
                m, row_max(scores(kb_ref[0, :, rows_of(j)], kc_ref[0, :, rows_of(j)]))), m)
            mm = row_max(jnp.where(mcol < n_meta, scores(kbm_ref[...], kcm_ref[...]), NEG_INF))
            m = jnp.where(mflag_ref[base + r] > 0, jnp.maximum(m, mm), m)
            hi, mid, lo = _split3(jnp.broadcast_to(-m, (tile, LANES)))
            patch = jnp.where(mine & (slot == SHIFT_SLOT), hi, jnp.where(mine & (slot == SHIFT_SLOT + 1), mid,
                              jnp.where(mine & (slot == SHIFT_SLOT + 2), lo, 0.0))).astype(bf16)
            if hh == 0:
                o_ref[0, 0, rows_of(r), :] = patch
            else:
                o_ref[0, 0, rows_of(r), :] = o_ref[0, 0, rows_of(r), :] + patch
            return carry

        lax.fori_loop(0, gq, per_tile, 0)


def _row_shift(q, qcat, kb, kcat, kbm, kcm, jmin, mflag, n_meta):
    b, l, _ = q.shape
    tile = min(ATT_TILE, l)
    nq = l // tile
    gq = min(ATT_GROUP, nq)
    rows = gq * tile
    grid_spec = pltpu.PrefetchScalarGridSpec(
        num_scalar_prefetch=2,
        grid=(b, N_HEADS // 2, nq // gq),
        in_specs=[
            pl.BlockSpec((1, rows, LANES), lambda b, p, i, *_: (b, i, p)),
            pl.BlockSpec((1, rows, LANES), lambda b, p, i, *_: (b, i, 0)),
            pl.BlockSpec((1, LANES, l), lambda b, p, i, *_: (b, p, 0)),
            pl.BlockSpec((1, LANES, l), lambda b, p, i, *_: (b, 0, 0)),
            pl.BlockSpec((LANES, LANES), lambda b, p, i, *_: (p, 0)),
            pl.BlockSpec((LANES, LANES), lambda b, p, i, *_: (0, 0)),
        ],
        out_specs=pl.BlockSpec((1, 1, rows, LANES), lambda b, p, i, *_: (b, p, i, 0)),
    )
    return pl.pallas_call(
        functools.partial(_rowshift_kernel, tile, nq, gq, n_meta),
        grid_spec=grid_spec,
        out_shape=jax.ShapeDtypeStruct((b, N_HEADS // 2, l, LANES), bf16),
        compiler_params=_cparams("parallel", "parallel", "arbitrary"),
        name="row_shift",
    )(jmin, mflag, q, qcat, kb, kcat, kbm, kcm)


def _sattn_kernel(past, tn, q_ref, kn_ref, vn_ref, ckt_ref, cvt_ref, lft_ref, o_ref):
    for i in range(q_ref.shape[0]):
        _sattn_one(past, tn, i, q_ref, kn_ref, vn_ref, ckt_ref, cvt_ref, lft_ref, o_ref)


def _sattn_one(past, tn, i, q_ref, kn_ref, vn_ref, ckt_ref, cvt_ref, lft_ref, o_ref):
    s_len = lft_ref.shape[-1]
    rows = N_HEADS * tn
    c = lft_ref[i]
    lane_h = lax.broadcasted_iota(jnp.int32, c.shape, 1)
    sh = 1
    while sh < s_len:
        c = c + jnp.where(lane_h >= sh, pltpu.roll(c, sh, axis=1), 0.0)
        sh *= 2
    cexp = jnp.concatenate([jnp.broadcast_to(c[h:h + 1, :], (tn, s_len)) for h in range(N_HEADS)], axis=0)
    rix = lax.broadcasted_iota(jnp.int32, (rows, s_len), 0)
    lix = lax.broadcasted_iota(jnp.int32, (rows, s_len), 1)
    qpos = past + rix % tn
    cq = jnp.sum(jnp.where(lix == qpos, cexp, 0.0), axis=1, keepdims=True)
    pad = jnp.zeros((s_len - past - tn, D_ATT), bf16)
    k_new = jnp.concatenate([kn_ref[i], pad], axis=0)
    v_new = jnp.concatenate([vn_ref[i], pad], axis=0)
    qrep = jnp.concatenate([q_ref[i].astype(f32)] * N_HEADS, axis=0)
    r5 = lax.broadcasted_iota(jnp.int32, (rows, D_ATT), 0)
    l5 = lax.broadcasted_iota(jnp.int32, (rows, D_ATT), 1)
    qrows = jnp.where(l5 // HEAD_DIM == r5 // tn, qrep, 0.0).astype(bf16)
    s = jnp.concatenate([_dot(qrows, ckt_ref[i].astype(bf16)), _dot_nt(qrows, k_new)], axis=1)
    s = s + (cq - cexp) * LOG2E
    s = jnp.where(lix <= qpos, s, NEG_INF)
    m = jnp.max(s, axis=1, keepdims=True)
    pe = jnp.exp2(s - m)
    den = jnp.sum(pe, axis=1, keepdims=True)
    pb = pe.astype(bf16)
    o = (_dot_nt(pb[:, :past], cvt_ref[i].astype(bf16)) + _dot(pb[:, past:], v_new)) / den
    lo = lax.broadcasted_iota(jnp.int32, (tn, D_ATT), 1)
    out = jnp.zeros((tn, D_ATT), f32)
    for h in range(N_HEADS):
        out = out + jnp.where(lo // HEAD_DIM == h, o[h * tn:(h + 1) * tn, :], 0.0)
    o_ref[i] = out.astype(bf16)


def _sample_attention(q, kn, vn, ckt, cvt, lft):
    bs, tn, _ = q.shape
    past = ckt.shape[2]
    s_len = lft.shape[-1]
    per_step = 2 if bs % 2 == 0 else 1
    blk = lambda n, w: pl.BlockSpec((per_step, n, w), lambda b: (b, 0, 0))
    return pl.pallas_call(
        functools.partial(_sattn_kernel, past, tn),
        grid=(bs // per_step,),
        in_specs=[blk(tn, D_ATT), blk(tn, D_ATT), blk(tn, D_ATT), blk(D_ATT, past), blk(D_ATT, past),
                  blk(N_HEADS, s_len)],
        out_specs=blk(tn, D_ATT),
        out_shape=jax.ShapeDtypeStruct((bs, tn, D_ATT), bf16),
        compiler_params=_cparams("parallel"),
        name="sample_attention",
    )(q, kn, vn, ckt, cvt, lft)


def _s5prep_kernel(n_sq, are_ref, aim_ref, ldt_ref, btr_ref, bti_ref, cre_ref, cim_ref,
                   aqr_ref, aqi_ref, apr_ref, api_ref, wr_ref, wi_ref, ctr_ref, cti_ref, kk_ref):
    are = are_ref[...]
    aim = aim_ref[...]
    dt = jnp.exp(ldt_ref[...])
    mag = jnp.exp(are * dt)
    ph = aim * dt
    abr = mag * jnp.cos(ph)
    abi = mag * jnp.sin(ph)
    nr = abr - 1.0
    den = are * are + aim * aim
    fr = (nr * are + abi * aim) / den
    fi = (abi * are - nr * aim) / den
    pw = [(jnp.ones_like(abr), jnp.zeros_like(abr))]
    for _ in range(S5_Q):
        pr, pi = pw[-1]
        pw.append((pr * abr - pi * abi, pr * abi + pi * abr))
    aqr_ref[...], aqi_ref[...] = pw[S5_Q]
    pr, pi = pw[S5_Q]
    for _ in range(n_sq):
        pr, pi = pr * pr - pi * pi, 2.0 * pr * pi
    apr_ref[...] = pr
    api_ref[...] = pi
    p_states = are.shape[1]
    ident = (lax.broadcasted_iota(jnp.int32, (p_states, p_states), 0)
             == lax.broadcasted_iota(jnp.int32, (p_states, p_states), 1)).astype(bf16)
    lane = lax.broadcasted_iota(jnp.int32, (SSM_GROUP, S5_Q * SSM_GROUP), 1)
    hi = functools.partial(lax.dot_general, dimension_numbers=(((1,), (1,)), ((), ())),
                           precision=lax.Precision.HIGHEST, preferred_element_type=f32)
    for g in range(are.shape[0]):
        row = slice(g, g + 1)
        btr = btr_ref[g]
        bti = bti_ref[g]
        bbr = fr[row] * btr - fi[row] * bti
        bbi = fr[row] * bti + fi[row] * btr
        cre = cre_ref[g]
        cim = cim_ref[g]
        car_all, cai_all = [cre], [cim]
        for q in range(S5_Q):
            er, ei = pw[S5_Q - 1 - q]
            wr_ref[q, g] = er[row] * bbr - ei[row] * bbi
            wi_ref[q, g] = er[row] * bbi + ei[row] * bbr
            gr, gi = pw[q + 1]
            car_all.append(cre * gr[row] - cim * gi[row])
            cai_all.append(cre * gi[row] + cim * gr[row])
        ctr_ref[g] = _dot_nt(ident, jnp.concatenate(car_all[1:], axis=0).astype(bf16))
        cti_ref[g] = _dot_nt(ident, jnp.concatenate(cai_all[1:], axis=0).astype(bf16))
        kt = (hi(bbr, jnp.concatenate(car_all[:S5_Q], axis=0))
              - hi(bbi, jnp.concatenate(cai_all[:S5_Q], axis=0)))
        kk_ref[g] = jnp.concatenate(
            [kt if qi == 0 else jnp.where(lane >= qi * SSM_GROUP, pltpu.roll(kt, qi * SSM_GROUP, axis=1), 0.0)
             for qi in range(S5_Q)], axis=0)


def _s5_prep(a_re, a_im, log_dt, bt_re, bt_im, c_re, c_im, n_sq):
    g, p = a_re.shape
    sd = jax.ShapeDtypeStruct
    qh = S5_Q * SSM_GROUP
    return pl.pallas_call(
        functools.partial(_s5prep_kernel, n_sq),
        out_shape=[sd((g, p), f32)] * 4 + [sd((S5_Q,) + bt_re.shape, f32)] * 2 + [sd((g, p, qh), f32)] * 2
        + [sd((g, qh, qh), f32)],
        name="s5_prep",
    )(a_re, a_im, log_dt, bt_re, bt_im, c_re, c_im)


def _s5_kernel(chain, n_seg, n_sb, u_ref, w8_ref, toep_ref, cpow_ref, aqr_ref, aqi_ref, apr_ref, api_ref, d8_ref,
               x0r_ref, x0i_ref, y_ref, sr_ref, si_ref, xr_ref, xi_ref):
    aqr = aqr_ref[0]
    aqi = aqi_ref[0]
    w8 = w8_ref[0]
    toep = toep_ref[0]
    cpow = cpow_ref[0]
    d8 = d8_ref[0]

    def step(with_y, sb):
        rows = pl.ds(sb, n_seg, stride=n_sb) if n_sb > 1 else pl.ds(0, n_seg)
        uf = jnp.concatenate([u_ref[0, 0, q, rows, :] for q in range(S5_Q)], axis=1)
        ub = uf.astype(bf16)
        xr = xr_ref[...]
        xi = xi_ref[...]
        if with_y:
            xc = jnp.concatenate([xr, xi], axis=1).astype(bf16)
            y = _dot(ub, toep) + _dot(xc, cpow) + d8 * uf
            for q in range(S5_Q):
                y_ref[0, 0, q, rows, :] = y[:, q * LANES:(q + 1) * LANES]
        pj = _dot(ub, w8)
        xr_ref[...] = aqr * xr - aqi * xi + pj[:, :STATE_BLOCK]
        xi_ref[...] = aqr * xi + aqi * xr + pj[:, STATE_BLOCK:]

    if chain:
        xr_ref[...] = jnp.zeros(xr_ref.shape, f32)
        xi_ref[...] = jnp.zeros(xi_ref.shape, f32)
        for sb in range(n_sb):
            step(False, sb)
        apr = apr_ref[0]
        api = api_ref[0]

        def scan(c, carry):
            kr, ki = carry
            er = xr_ref[pl.ds(c, 1), :]
            ei = xi_ref[pl.ds(c, 1), :]
            xr_ref[pl.ds(c, 1), :] = kr
            xi_ref[pl.ds(c, 1), :] = ki
            return apr * kr - api * ki + er, apr * ki + api * kr + ei

        kr, ki = lax.fori_loop(0, n_seg, scan, (x0r_ref[0], x0i_ref[0]))
        sr_ref[0] = kr
        si_ref[0] = ki
        for sb in range(n_sb):
            step(True, sb)
    else:
        xr_ref[...] = x0r_ref[0]
        xi_ref[...] = x0i_ref[0]
        for sb in range(n_sb):
            step(True, sb)
        sr_ref[0] = xr_ref[...]
        si_ref[0] = xi_ref[...]


def _s5(u, x0r, x0i, weights, n_seg, chain, name):
    bq, _, _, m, _ = u.shape
    n_sb = m // n_seg
    r0 = x0r.shape[1]
    per_k = lambda a: pl.BlockSpec((1,) + a.shape[1:], lambda b, k: (k, 0, 0))
    st = pl.BlockSpec((1, r0, STATE_BLOCK), lambda b, k: (b, 0, k))
    tok = lambda: pl.BlockSpec((1, 1, S5_Q, m, LANES), lambda b, k: (b, k, 0, 0, 0))
    sd = jax.ShapeDtypeStruct
    return pl.pallas_call(
        functools.partial(_s5_kernel, chain, n_seg, n_sb),
        grid=(bq, N_SSM_BLOCKS),
        in_specs=[tok()] + [per_k(a) for a in weights] + [st, st],
        out_specs=[tok(), st, st],
        out_shape=[sd(u.shape, f32), sd(x0r.shape, f32), sd(x0i.shape, f32)],
        scratch_shapes=[pltpu.VMEM((n_seg, STATE_BLOCK), f32)] * 2,
        compiler_params=_cparams("parallel", "parallel"),
        name=name,
    )(u, *weights, x0r, x0i)


def _out_kernel(x_ref, att_ref, ga_ref, ys_ref, gs_ref, wg_ref, bg_ref, wo_ref, y_ref, ys_scr):
    ga = ga_ref[0].astype(f32)
    a = att_ref[0].astype(f32) * (ga * jax.nn.sigmoid(ga))
    rows = x_ref.shape[1]
    for kb in range(N_SSM_BLOCKS):
        for qq in range(S5_Q):
            ys_scr[kb, pl.ds(qq, rows // S5_Q, stride=S5_Q), :] = ys_ref[0, kb, qq]
    ys = jnp.concatenate([ys_scr[kb] for kb in range(N_SSM_BLOCKS)], axis=1)
    z = 0.5 * ys * (1.0 + jnp.tanh(0.7978845608028654 * (ys + 0.044715 * (ys * ys * ys))))
    gate = jax.nn.sigmoid(_dot(z.astype(bf16), wg_ref[...]) + bg_ref[...])
    gs = gs_ref[0].astype(f32)
    s5 = z * gate * (gs * jax.nn.sigmoid(gs))
    cat = jnp.concatenate([a, s5], axis=1).astype(bf16)
    y_ref[0] = x_ref[0] + _dot(cat, wo_ref[...])


def _outproj(x, att, ga, ys, gs, wg, bg, wo, tile, name):
    bx, lx, d = x.shape
    full = lambda a: pl.BlockSpec(a.shape, lambda b, j: (0,) * a.ndim)
    row = lambda w: pl.BlockSpec((1, tile, w), lambda b, j: (b, j, 0))
    return pl.pallas_call(
        _out_kernel,
        grid=(bx, lx // tile),
        in_specs=[row(d), row(D_ATT), row(D_ATT), _s5_tokens_spec(tile, lambda b, j: (b, 0, 0, j, 0)), row(D_SSM),
                  full(wg), full(bg), full(wo)],
        out_specs=row(d),
        out_shape=jax.ShapeDtypeStruct(x.shape, f32),
        scratch_shapes=[pltpu.VMEM((N_SSM_BLOCKS, tile, LANES), f32)],
        compiler_params=_cparams("parallel", "parallel"),
        name=name,
    )(x, att, ga, ys, gs, wg, bg, wo)


def _lower_tri(n):
    r = jnp.arange(n)
    return (r[None, :] <= r[:, None]).astype(bf16)


def kernel(x_prompt, x_sample, cache_k, cache_v, cache_logf, state_s5_re, state_s5_im, meta_tokens, norm_g,
           w_in, b_f, q_norm_g, k_norm_g, s5_a_re, s5_a_im, s5_log_dt, s5_b_re, s5_b_im, s5_c_re, s5_c_im,
           s5_d, w_glu, b_glu, w_out):
    assert norm_g.shape[0] == 1, "single-layer step"
    bp, lp, d = x_prompt.shape
    bs, ts, _ = x_sample.shape
    n_meta = meta_tokens.shape[0]
    past = cache_k.shape[2]
    G, P = N_GROUPS, SSM_STATE

    w = w_in[0]
    o3 = 3 * D_ATT
    o4 = o3 + N_HEADS
    wm = jnp.concatenate([w[:, :o3], w[:, o4:]], axis=1).astype(bf16)
    wf = jnp.repeat(w[:, o3:o4], AUG, axis=1).astype(bf16)
    bfr = jnp.repeat(b_f[0], AUG)[None, :].astype(f32)
    g_in = norm_g[0][None, :].astype(f32)
    qg = (jnp.tile(q_norm_g[0], N_HEADS) * (HEAD_DIM ** -0.5 * LOG2E))[None, :].astype(f32)
    kg = jnp.tile(k_norm_g[0], N_HEADS)[None, :].astype(f32)
    hid = jnp.arange(D_ATT) // HEAD_DIM
    eh = ((hid[:, None] == hid[None, :]).astype(f32) / HEAD_DIM).astype(bf16)
    sel = (jnp.arange(LANES)[None, :] == AUG * jnp.arange(N_HEADS)[:, None]).astype(bf16)
    m_bound = 8.0 * jnp.max(jnp.abs(q_norm_g[0])) * jnp.max(jnp.abs(k_norm_g[0])) * 1.02
    robust = m_bound > SHIFT_SAFE
    negm = jnp.full((1, LANES), -LOG2E, f32) * jnp.where(robust, 0.0, m_bound)
    params = (g_in, wm, wf, bfr, qg, kg, eh, negm, sel)

    tm = n_meta
    meta = _inproj(meta_tokens[None].astype(f32), jnp.zeros((1, 1, LANES), f32), params, tm, _lower_tri(tm), True,
                   "inproj_meta")
    (_, kb_m, vb_m, kf_m, vf_m, lft_m, edge_m, _, kcat_m, _, u_m, _) = meta
    cum_meta_end = edge_m[:, -1, 1:2, :]
    c0 = jnp.broadcast_to(cum_meta_end, (bp, 1, LANES))
    tile_p = min(ROW_TILE, lp)
    tb_p = min(LANES, tile_p)
    meta_cols = lambda a: jnp.pad(jnp.swapaxes(a, 1, 2), ((0, 0), (0, 0), (0, LANES - tm)))
    (q_p, kb_p, vb_p, kt_p, vt_p, lft_p, edge_p, qcat_p, kcat_p, ga_p, u_p, gs_p) = _inproj(
        x_prompt, c0, params, tile_p, _lower_tri(tb_p), True, "inproj_prompt",
        meta_t=(meta_cols(kf_m), meta_cols(vf_m), n_meta))
    ls = bs * ts
    tile_s = min(ROW_TILE, ls)
    (q_s, kb_s, vb_s, kf_s, vf_s, lft_s, _, _, _, ga_s, u_s, gs_s) = _inproj(
        x_sample.reshape(1, ls, d), jnp.zeros((1, 1, LANES), f32), params, tile_s,
        _lower_tri(min(LANES, tile_s)), False, "inproj_sample")

    tile_a = min(ATT_TILE, lp)
    nq = lp // tile_a
    per_tile = tile_a // tb_p
    edges = edge_p[:, :, :2, ::AUG]
    q_start = edges[:, 0::per_tile, 0, :]
    k_end = edges[:, per_tile - 1::per_tile, 1, :]
    dmax = q_start[:, :, None, :] - k_end[:, None, :, :]
    padm = lambda a: jnp.pad(a[0], ((0, LANES - tm), (0, 0)))
    kbm, vbm, kcm = padm(kb_m).T, padm(vb_m), padm(kcat_m).T

    def skip_tables(thr):
        jmin = jnp.sum((dmax < -thr).astype(jnp.int32), axis=2)
        jmin = jnp.minimum(jmin, jnp.arange(nq, dtype=jnp.int32)[None, :, None])
        mflag = (q_start - cum_meta_end[:, :, ::AUG] >= -thr).astype(jnp.int32)
        return jnp.transpose(jmin, (0, 2, 1)).reshape(-1), jnp.transpose(mflag, (0, 2, 1)).reshape(-1)

    def attend_bounded():
        return _prompt_attention(q_p, qcat_p, kb_p, vb_p, kcat_p, kbm, vbm, kcm, *skip_tables(EXP_ZERO), n_meta)

    def attend_robust():
        tables = skip_tables(EXP_ZERO + 2.0 * m_bound)
        shift = _row_shift(q_p, qcat_p, kb_p, kcat_p, kbm, kcm, *tables, n_meta)
        qcat_r = (qcat_p.astype(f32) + jnp.sum(shift.astype(f32), axis=1)).astype(bf16)
        return _prompt_attention(q_p, qcat_r, kb_p, vb_p, kcat_p, kbm, vbm, kcm, *tables, n_meta)

    att_p = lax.cond(robust, attend_robust, attend_bounded)

    lft_new = jnp.transpose(lft_s.reshape(N_HEADS, bs, ts), (1, 0, 2))
    s_len = -(-(past + ts) // LANES) * LANES
    lft = jnp.concatenate([jnp.transpose(cache_logf[0].astype(f32), (0, 2, 1)), lft_new,
                           jnp.zeros((bs, N_HEADS, s_len - past - ts), f32)], axis=2)
    cache_t = lambda c: jnp.transpose(c[0], (0, 2, 3, 1)).reshape(bs, D_ATT, past)
    att_s = _sample_attention(q_s.reshape(bs, ts, D_ATT), kb_s.reshape(bs, ts, D_ATT), vb_s.reshape(bs, ts, D_ATT),
                              cache_t(cache_k), cache_t(cache_v), lft)

    n_seg_p = min(PROMPT_SEGMENTS, lp // S5_Q)
    n_sb_p = lp // (S5_Q * n_seg_p)
    n_sq = n_sb_p.bit_length() - 1
    assert n_sb_p == 1 << n_sq and ts % S5_Q == 0
    tr = lambda a: jnp.transpose(a[0], (0, 2, 1)).astype(f32)
    aqr, aqi, apr, api, wr, wi, ctr, cti, kk = _s5_prep(
        s5_a_re[0].astype(f32), s5_a_im[0].astype(f32), s5_log_dt[0][:, None].astype(f32),
        tr(s5_b_re), tr(s5_b_im), s5_c_re[0].astype(f32), s5_c_im[0].astype(f32), n_sq)
    NB, G8, Q = N_SSM_BLOCKS, GROUPS_PER_BLOCK, S5_Q
    qgh = Q * LANES

    def spread(x, col_of_src, row_group, col_group):
        n_src = x.shape[-1]
        n_rows = x.shape[1]
        n_cols = col_of_src.shape[0]
        sel = (col_of_src[None, :] == jnp.arange(n_src)[:, None]).astype(bf16)
        keep = row_group(jnp.arange(n_rows))[:, None] == col_group(jnp.arange(n_cols))[None, :]
        y = jnp.einsum("krs,sc->krc", x.astype(bf16), sel, preferred_element_type=f32)
        return jnp.where(keep[None], y, 0.0).astype(bf16)

    cols_state = jnp.arange(STATE_BLOCK) % P
    cols_qgh = (jnp.arange(qgh) // LANES) * SSM_GROUP + jnp.arange(qgh) % SSM_GROUP
    group_of_qgh = lambda i: (i // SSM_GROUP) % G8
    by_block = lambda a: jnp.transpose(a.reshape(Q, NB, G8 * SSM_GROUP, P), (1, 0, 2, 3)).reshape(NB, qgh, P)
    w_part = lambda a: spread(by_block(a), cols_state, group_of_qgh, lambda i: i // P)
    w8 = jnp.concatenate([w_part(wr), w_part(wi)], axis=2)
    c_part = lambda a: spread(a.reshape(NB, STATE_BLOCK, Q * SSM_GROUP), cols_qgh, lambda i: i // P, group_of_qgh)
    cpow = jnp.concatenate([c_part(ctr), c_part(-cti)], axis=1)
    kk_rows = jnp.transpose(kk.reshape(NB, G8, Q, SSM_GROUP, Q * SSM_GROUP), (0, 2, 1, 3, 4)).reshape(NB, qgh, -1)
    toep = spread(kk_rows, cols_qgh, group_of_qgh, group_of_qgh)
    lane_k = lambda a: a.reshape(NB, 1, -1)
    d8 = jnp.tile(lane_k(s5_d[0].astype(f32)), (1, 1, Q))
    weights = lambda pr_, pi_: (w8, toep, cpow, lane_k(aqr), lane_k(aqi), lane_k(pr_), lane_k(pi_), d8)

    assert tm == ts
    n_small = -(-(bs + 1) // SUBLANES) * SUBLANES
    u_small = jnp.concatenate([u_s, u_m, jnp.zeros(_s5_tokens_shape(1, (n_small - bs - 1) * ts), f32)], axis=3)
    pad_state = lambda a: jnp.pad(a[0].reshape(bs, G * P).astype(f32), ((0, n_small - bs), (0, 0)))[None]
    ys_small, sr_small, si_small = _s5(u_small, pad_state(state_s5_re), pad_state(state_s5_im),
                                       weights(aqr, aqi), n_small, False, "s5_short")
    x0r = jnp.broadcast_to(sr_small[:, bs:bs + 1, :], (bp, 1, G * P))
    x0i = jnp.broadcast_to(si_small[:, bs:bs + 1, :], (bp, 1, G * P))
    ys_p, sr_p, si_p = _s5(u_p, x0r, x0i, weights(apr, api), n_seg_p, True, "s5_prompt")

    wg = w_glu[0].astype(bf16)
    bg = b_glu[0][None, :].astype(f32)
    wo = w_out[0].astype(bf16)
    y_prompt = _outproj(x_prompt, att_p, ga_p, ys_p, gs_p, wg, bg, wo, min(OUT_TILE, lp), "outproj_prompt")
    y_sample = _outproj(x_sample.reshape(1, ls, d), att_s.reshape(1, ls, D_ATT), ga_s, ys_small[:, :, :, :ls // S5_Q],
                        gs_s, wg, bg, wo, tile_s, "outproj_sample").reshape(bs, ts, d)

    heads_t = lambda a: jnp.transpose(a.reshape(bp, N_HEADS, HEAD_DIM, n_meta + lp), (0, 3, 1, 2))[None]
    new_k_prompt = heads_t(kt_p)
    new_v_prompt = heads_t(vt_p)
    lft_all = jnp.concatenate([jnp.broadcast_to(lft_m, (bp, N_HEADS, n_meta)), lft_p], axis=2)
    new_logf_prompt = jnp.transpose(lft_all, (0, 2, 1))[None]
    new_s5_re_prompt = sr_p.reshape(1, bp, G, P)
    new_s5_im_prompt = si_p.reshape(1, bp, G, P)
    new_k_sample = kf_s.reshape(1, bs, ts, N_HEADS, HEAD_DIM)
    new_v_sample = vf_s.reshape(1, bs, ts, N_HEADS, HEAD_DIM)
    new_logf_sample = jnp.transpose(lft_new, (0, 2, 1))[None]
    new_s5_re_sample = sr_small[:, :bs].reshape(1, bs, G, P)
    new_s5_im_sample = si_small[:, :bs].reshape(1, bs, G, P)
    return (y_prompt, y_sample, new_k_prompt, new_v_prompt, new_logf_prompt, new_s5_re_prompt, new_s5_im_prompt,
            new_k_sample, new_v_sample, new_logf_sample, new_s5_re_sample, new_s5_im_sample)
```

```python
import functools

import jax
import jax.numpy as jnp
from jax import lax
from jax.experimental import pallas as pl
from jax.experimental.pallas import tpu as pltpu

N_HEADS = 8
HEAD_DIM = 64
D_ATT = N_HEADS * HEAD_DIM
SSM_GROUP = 16
SSM_STATE = 64
D_SSM = 512
N_GROUPS = D_SSM // SSM_GROUP
EPS = 1e-6
NEG_INF = -1e30

LANES = 128
MXU_TILE = 256
SUBLANES = 8
AUG = LANES // N_HEADS
SHIFT_SLOT = 7
GROUPS_PER_BLOCK = LANES // SSM_GROUP
STATE_BLOCK = GROUPS_PER_BLOCK * SSM_STATE
N_SSM_BLOCKS = D_SSM // LANES
VMEM_LIMIT = 56 * 1024 * 1024
EXP_ZERO = 105.0
SHIFT_SAFE = 32.0

ROW_TILE = 512
OUT_TILE = 1024
ATT_TILE = 256
ATT_GROUP = 32
ATT_UNROLL = 32
ATT_TAIL_UNROLL = 8
LOG2E = 1.4426950408889634
PROMPT_SEGMENTS = 512
S5_Q = 8

bf16 = jnp.bfloat16
f32 = jnp.float32


def _cparams(*sem):
    return pltpu.CompilerParams(dimension_semantics=sem, vmem_limit_bytes=VMEM_LIMIT)


def _split3(x):
    hi = x.astype(bf16).astype(f32)
    r1 = x - hi
    mid = r1.astype(bf16).astype(f32)
    lo = (r1 - mid).astype(bf16).astype(f32)
    return hi, mid, lo


def _dot(a, b):
    return jnp.dot(a, b, preferred_element_type=f32)


def _dot_nt(a, b):
    return lax.dot_general(a, b, (((1,), (1,)), ((), ())), preferred_element_type=f32)


def _s5_tokens_shape(b, l):
    return (b, N_SSM_BLOCKS, S5_Q, l // S5_Q, LANES)


def _s5_tokens_spec(rows, index_map, **kw):
    return pl.BlockSpec((1, N_SSM_BLOCKS, S5_Q, rows // S5_Q, LANES), index_map, **kw)


def _inproj_kernel(chain, n_blk, join, *refs):
    n_in = 12
    if chain:
        c0_ref, carry_ref = refs[1], refs[-5 if join else -2]

        @pl.when(pl.program_id(1) == 0)
        def _():
            carry_ref[...] = c0_ref[0]
    if not join:
        _inproj_tile(chain, n_blk, False, *refs)
        return
    kmt_ref, vmt_ref = refs[n_in:n_in + 2]
    refs = refs[:n_in] + refs[n_in + 2:]
    kb_ref, kf_ref, vf_ref, kcat_ref = refs[n_in + 1], refs[n_in + 3], refs[n_in + 4], refs[n_in + 8]
    prev_ref, prevc_ref, tails_ref = refs[-3:]
    j = pl.program_id(1)
    n_tiles = pl.num_programs(1) - 2
    lane_t = lax.broadcasted_iota(jnp.int32, (D_ATT, LANES), 1)

    def emit():
        kb_ref[0] = prev_ref[0].T.astype(bf16)
        kcat_ref[0] = prevc_ref[...].T.astype(bf16)
        for idx, t_ref in enumerate((kf_ref, vf_ref)):
            rolled = pltpu.roll(prev_ref[idx].T, join, axis=1)
            first = jnp.where(lane_t < join, tails_ref[idx], rolled[:, :LANES])
            t_ref[0] = first if rolled.shape[1] == LANES else jnp.concatenate([first, rolled[:, LANES:]], axis=1)
            tails_ref[idx] = jnp.where(j > 0, rolled[:, :LANES], tails_ref[idx])

    @pl.when(j == 0)
    def _():
        tails_ref[0] = kmt_ref[0]
        tails_ref[1] = vmt_ref[0]
        prev_ref[...] = jnp.zeros(prev_ref.shape, f32)
        prevc_ref[...] = jnp.zeros(prevc_ref.shape, f32)

    @pl.when(j < n_tiles)
    def _():
        emit()
        _inproj_tile(chain, n_blk, True, *refs)

    pl.when(j == n_tiles)(emit)

    @pl.when(j == n_tiles + 1)
    def _():
        for idx, t_ref in enumerate((kf_ref, vf_ref)):
            t_ref[0] = jnp.zeros(t_ref.shape[1:], f32)
            t_ref[0, :, :LANES] = tails_ref[idx]


def _inproj_tile(chain, n_blk, join, x_ref, c0_ref, g_ref, wm_ref, wf_ref, bf_ref, qg_ref, kg_ref, eh_ref,
                 tri_ref, negm_ref, sel_ref, q_ref, kb_ref, vb_ref, kf_ref, vf_ref, lft_ref, edge_ref,
                 qcat_ref, kcat_ref, ga_ref, u_ref, gs_ref, carry_ref, us_ref, prev_ref=None, prevc_ref=None,
                 tails_ref=None):
    x = x_ref[0]
    ms = jnp.mean(x * x, axis=-1, keepdims=True)
    xn = (x * lax.rsqrt(ms + EPS) * g_ref[...]).astype(bf16)
    z = _dot(xn, wm_ref[...])
    q = z[:, 0 * D_ATT:1 * D_ATT]
    k = z[:, 1 * D_ATT:2 * D_ATT]
    v = z[:, 2 * D_ATT:3 * D_ATT]
    eh = eh_ref[...]
    qn = q * lax.rsqrt(_dot((q * q).astype(bf16), eh) + EPS) * qg_ref[...]
    kn = k * lax.rsqrt(_dot((k * k).astype(bf16), eh) + EPS) * kg_ref[...]
    q_ref[0] = qn.astype(bf16)
    vb_ref[0] = v.astype(bf16)
    if join:
        prev_ref[0] = kn
        prev_ref[1] = v
    else:
        kb_ref[0] = kn.astype(bf16)
        kf_ref[0] = kn
        vf_ref[0] = v
    ga_ref[0] = z[:, 3 * D_ATT:4 * D_ATT].astype(bf16)
    gs_ref[0] = z[:, 4 * D_ATT + D_SSM:].astype(bf16)
    for kb in range(N_SSM_BLOCKS):
        us_ref[kb] = z[:, 4 * D_ATT + kb * LANES:4 * D_ATT + (kb + 1) * LANES]
    for kb in range(N_SSM_BLOCKS):
        for qq in range(S5_Q):
            u_ref[0, kb, qq] = us_ref[kb, pl.ds(qq, x.shape[0] // S5_Q, stride=S5_Q), :]

    zf = _dot(xn, wf_ref[...]) + bf_ref[...]
    lf = jnp.minimum(zf, 0.0) - jnp.log1p(jnp.exp(-jnp.abs(zf)))
    sel = sel_ref[...]
    lft_ref[0] = sum(_dot_nt(sel, part.astype(bf16)) for part in _split3(lf))

    carry = carry_ref[...] if chain else jnp.zeros((1, LANES), f32)
    tri = tri_ref[...]
    tb = x.shape[0] // n_blk
    cums = []
    for r in range(n_blk):
        hi, mid, lo = _split3(lf[r * tb:(r + 1) * tb])
        c = _dot(tri, hi.astype(bf16)) + _dot(tri, mid.astype(bf16)) + _dot(tri, lo.astype(bf16)) + carry
        cums.append(c)
        carry = c[tb - 1:tb, :]
        edge_ref[0, r] = jnp.concatenate([c[0:1, :], carry, jnp.zeros((SUBLANES - 2, LANES), f32)], axis=0)
    cum = cums[0] if n_blk == 1 else jnp.concatenate(cums, axis=0)
    if chain:
        carry_ref[...] = carry

    slot = lax.broadcasted_iota(jnp.int32, (1, LANES), 1) % AUG
    hi, mid, lo = _split3(cum * LOG2E)
    qc = jnp.where(slot == 0, hi, jnp.where(slot == 1, mid, jnp.where(slot == 2, lo,
                   jnp.where(slot <= 6, 1.0, 0.0))))
    kc = jnp.where(slot <= 2, 1.0, jnp.where(slot == 3, -hi, jnp.where(slot == 4, -mid,
                   jnp.where(slot == 5, -lo, jnp.where(slot == 6, negm_ref[...],
                             jnp.where(slot <= SHIFT_SLOT + 2, 1.0, 0.0))))))
    qcat_ref[0] = qc.astype(bf16)
    if join:
        prevc_ref[...] = kc
    else:
        kcat_ref[0] = kc.astype(bf16)


def _inproj(x, c0, params, tile, tri, chain, name, meta_t=None):
    bx, lx, d = x.shape
    n_tiles = lx // tile
    n_blk = tile // tri.shape[0]
    join = 0 if meta_t is None else meta_t[2]
    last = n_tiles - 1
    tj = (lambda j: jnp.minimum(j, last)) if join else (lambda j: j)
    tprev = lambda j: jnp.maximum(j - 1, 0)
    full = lambda a: pl.BlockSpec(a.shape, lambda b, j: (0,) * a.ndim)
    row = lambda w: pl.BlockSpec((1, tile, w), lambda b, j: (b, tj(j), 0))
    sd = jax.ShapeDtypeStruct
    rows = lambda w, t: (row(w), sd((bx, lx, w), t))
    kv = (rows(D_ATT, f32) if not join else
          (pl.BlockSpec((1, D_ATT, tile), lambda b, j: (b, 0, tprev(j))), sd((bx, D_ATT, join + lx), f32)))
    cols = lambda w: (pl.BlockSpec((1, w, tile), lambda b, j: (b, 0, jnp.minimum(tprev(j), last))), sd((bx, w, lx), bf16))
    outs = [rows(D_ATT, bf16), cols(D_ATT) if join else rows(D_ATT, bf16), rows(D_ATT, bf16), kv, kv,
            (pl.BlockSpec((1, N_HEADS, tile), lambda b, j: (b, 0, tj(j))), sd((bx, N_HEADS, lx), f32)),
            (pl.BlockSpec((1, n_blk, SUBLANES, LANES), lambda b, j: (b, tj(j), 0, 0)),
             sd((bx, n_tiles * n_blk, SUBLANES, LANES), f32)),
            rows(LANES, bf16), cols(LANES) if join else rows(LANES, bf16), rows(D_ATT, bf16),
            (_s5_tokens_spec(tile, lambda b, j: (b, 0, 0, tj(j), 0)), sd(_s5_tokens_shape(bx, lx), f32)),
            rows(D_SSM, bf16)]
    g_in, wm, wf, bfr, qg, kg, eh, negm, sel = params
    operands = (g_in, wm, wf, bfr, qg, kg, eh, tri, negm, sel) + (tuple(meta_t[:2]) if join else ())
    scratch = [pltpu.VMEM((1, LANES), f32), pltpu.VMEM((N_SSM_BLOCKS, tile, LANES), f32)]
    if join:
        scratch += [pltpu.VMEM((2, tile, D_ATT), f32), pltpu.VMEM((tile, LANES), f32), pltpu.VMEM((2, D_ATT, LANES), f32)]
    return pl.pallas_call(
        functools.partial(_inproj_kernel, chain, n_blk, join),
        grid=(bx, n_tiles + (2 if join else 0)),
        in_specs=[row(d), pl.BlockSpec((1, 1, LANES), lambda b, j: (b, 0, 0))] + [full(a) for a in operands],
        out_specs=[o[0] for o in outs],
        out_shape=[o[1] for o in outs],
        scratch_shapes=scratch,
        compiler_params=_cparams("parallel", "arbitrary"),
        name=name,
    )(x, c0, *operands)


def _attn_kernel(tile, nq, gq, n_meta, jmin_ref, mflag_ref, q_ref, qc_ref, kb_ref, vb_ref, kc_ref,
                 kbm_ref, vbm_ref, kcm_ref, o_ref, acc_ref, bnd_ref):
    b = pl.program_id(0)
    p = pl.program_id(1)
    g = pl.program_id(2)
    lane = lax.broadcasted_iota(jnp.int32, (1, LANES), 1)
    mcol = lax.broadcasted_iota(jnp.int32, (tile, LANES), 1)
    col = lax.broadcasted_iota(jnp.int32, (tile, tile), 1)
    row = lax.broadcasted_iota(jnp.int32, (tile, tile), 0)

    @pl.when(g == 0)
    def _():
        bnd_ref[0] = jnp.where(col <= row, -NEG_INF, NEG_INF)
        bnd_ref[1] = jnp.full((tile, tile), -NEG_INF, f32)
        bnd_ref[2] = jnp.full((tile, tile), NEG_INF, f32)

    def rows_of(j):
        return pl.ds(pl.multiple_of(j * tile, tile), tile)

    n_slots = 2 * gq
    base = (b * N_HEADS + 2 * p) * nq + g * gq

    def table_index(t):
        return base + (t // gq) * nq + t % gq

    def clear(t, carry):
        acc_ref[t] = jnp.zeros(acc_ref.shape[1:], f32)
        return carry

    lax.fori_loop(0, n_slots, clear, 0)

    def head_masks(hh):
        own = lane // HEAD_DIM == hh
        return own.astype(bf16), (~own).astype(bf16), (lane // AUG == 2 * p + hh).astype(bf16)

    def scores(tc, kt, kct):
        m_q, _, m_c = head_masks(tc // gq)
        rows = rows_of(tc % gq)
        q_op = jnp.concatenate([q_ref[0, rows, :] * m_q, qc_ref[0, rows, :] * m_c], axis=1)
        return _dot(q_op, jnp.concatenate([kt, kct], axis=0))

    def pv(tc, pe, vs):
        m_q, m_o, _ = head_masks(tc // gq)
        return _dot(pe, vs * m_q + m_o)

    def item_probs(t, j):
        tc = jnp.minimum(t, n_slots - 1)
        kind = jnp.where(t < n_slots, jnp.where(j == g * gq + tc % gq, 0, 1), 2)
        s = scores(tc, kb_ref[0, :, rows_of(j)], kc_ref[0, :, rows_of(j)])
        return jnp.exp2(jnp.minimum(s, bnd_ref[kind])).astype(bf16), tc

    def item_pv(pe, tc, j):
        acc_ref[tc] += pv(tc, pe, vb_ref[0, rows_of(j), :])

    def advance(t, j):
        last = jnp.logical_and(t < n_slots, j == g * gq + t % gq)
        t2 = jnp.where(last, t + 1, t)
        j2 = jnp.where(last, jmin_ref[table_index(jnp.minimum(t + 1, n_slots - 1))],
                       jnp.where(t < n_slots, j + 1, j))
        return t2.astype(jnp.int32), j2.astype(jnp.int32)

    def body(unroll, trip, carry):
        pe, tp, jp, t, j = carry
        for _ in range(unroll):
            pn, tn = item_probs(t, j)
            item_pv(pe, tp, jp)
            pe, tp, jp = pn, tn, j
            t, j = advance(t, j)
        return pe, tp, jp, t, j

    n_items = sum(g * gq + t % gq - jmin_ref[table_index(t)] + 1 for t in range(n_slots))
    long_trips = (n_items - 1) // ATT_UNROLL
    short_trips = (n_items - 1 - long_trips * ATT_UNROLL + ATT_TAIL_UNROLL - 1) // ATT_TAIL_UNROLL
    j0 = jmin_ref[base]
    pe0, t0 = item_probs(jnp.int32(0), j0)
    carry = (pe0, t0, j0) + advance(jnp.int32(0), j0)
    carry = lax.fori_loop(0, long_trips, functools.partial(body, ATT_UNROLL), carry)
    pe, tp, jp, _, _ = lax.fori_loop(0, short_trips, functools.partial(body, ATT_TAIL_UNROLL), carry)
    item_pv(pe, tp, jp)

    def meta_tile(t, carry):
        @pl.when(mflag_ref[table_index(t)] > 0)
        def _():
            s = jnp.where(mcol < n_meta, scores(t, kbm_ref[...], kcm_ref[...]), NEG_INF)
            acc_ref[t] += pv(t, jnp.exp2(s).astype(bf16), vbm_ref[...])
        return carry

    lax.fori_loop(0, n_slots, meta_tile, 0)

    for r in range(gq):
        a0 = acc_ref[r]
        a1 = acc_ref[gq + r]
        own = lane < HEAD_DIM
        sums = pltpu.roll(jnp.where(own, a1, a0), HEAD_DIM, axis=1)
        o_ref[0, r * tile:(r + 1) * tile, :] = (jnp.where(own, a0, a1) / sums).astype(bf16)


def _prompt_attention(q, qcat, kb, vb, kcat, kbm, vbm, kcm, jmin, mflag, n_meta):
    b, l, _ = q.shape
    tile = min(ATT_TILE, l)
    nq = l // tile
    gq = min(ATT_GROUP, nq)
    rows = gq * tile
    grid_spec = pltpu.PrefetchScalarGridSpec(
        num_scalar_prefetch=2,
        grid=(b, N_HEADS // 2, nq // gq),
        in_specs=[
            pl.BlockSpec((1, rows, LANES), lambda b, p, i, *_: (b, i, p)),
            pl.BlockSpec((1, rows, LANES), lambda b, p, i, *_: (b, i, 0)),
            pl.BlockSpec((1, LANES, l), lambda b, p, i, *_: (b, p, 0)),
            pl.BlockSpec((1, l, LANES), lambda b, p, i, *_: (b, 0, p)),
            pl.BlockSpec((1, LANES, l), lambda b, p, i, *_: (b, 0, 0)),
            pl.BlockSpec((LANES, LANES), lambda b, p, i, *_: (p, 0)),
            pl.BlockSpec((LANES, LANES), lambda b, p, i, *_: (0, p)),
            pl.BlockSpec((LANES, LANES), lambda b, p, i, *_: (0, 0)),
        ],
        out_specs=pl.BlockSpec((1, rows, LANES), lambda b, p, i, *_: (b, i, p)),
        scratch_shapes=[pltpu.VMEM((2 * gq, tile, LANES), f32), pltpu.VMEM((3, tile, tile), f32)],
    )
    return pl.pallas_call(
        functools.partial(_attn_kernel, tile, nq, gq, n_meta),
        grid_spec=grid_spec,
        out_shape=jax.ShapeDtypeStruct((b, l, D_ATT), bf16),
        compiler_params=_cparams("parallel", "parallel", "arbitrary"),
        name="prompt_attention",
    )(jmin, mflag, q, qcat, kb, vb, kcat, kbm, vbm, kcm)


def _rowshift_kernel(tile, nq, gq, n_meta, jmin_ref, mflag_ref, q_ref, qc_ref, kb_ref, kc_ref, kbm_ref, kcm_ref,
                     o_ref):
    b = pl.program_id(0)
    p = pl.program_id(1)
    g = pl.program_id(2)
    lane = lax.broadcasted_iota(jnp.int32, (1, LANES), 1)
    slot = lane % AUG
    mcol = lax.broadcasted_iota(jnp.int32, (tile, LANES), 1)
    col = lax.broadcasted_iota(jnp.int32, (tile, tile), 1)
    row = lax.broadcasted_iota(jnp.int32, (tile, tile), 0)

    def rows_of(j):
        return pl.ds(pl.multiple_of(j * tile, tile), tile)

    def row_max(x):
        return jnp.max(x, axis=1, keepdims=True)

    for hh in range(2):
        h = 2 * p + hh
        base = (b * N_HEADS + h) * nq + g * gq
        m_q = (lane // HEAD_DIM == hh).astype(bf16)
        mine = lane // AUG == h
        m_c = mine.astype(bf16)

        def per_tile(r, carry):
            i = g * gq + r
            q_op = jnp.concatenate([q_ref[0, rows_of(r), :] * m_q, qc_ref[0, rows_of(r), :] * m_c], axis=1)

            def scores(kt, kct):
                return _dot(q_op, jnp.concatenate([kt, kct], axis=0))

            m = row_max(jnp.where(col <= row, scores(kb_ref[0, :, rows_of(i)], kc_ref[0, :, rows_of(i)]), NEG_INF))
            m = lax.fori_loop(jmin_ref[base + r], i, lambda j, m: jnp.maximum(
                m, row_max(scores(kb_ref[0, :, rows_of(j)], kc_ref[0, :, rows_of(j)]))), m)
            mm = row_max(jnp.where(mcol < n_meta, scores(kbm_ref[...], kcm_ref[...]), NEG_INF))
            m = jnp.where(mflag_ref[base + r] > 0, jnp.maximum(m, mm), m)
            hi, mid, lo = _split3(jnp.broadcast_to(-m, (tile, LANES)))
            patch = jnp.where(mine & (slot == SHIFT_SLOT), hi, jnp.where(mine & (slot == SHIFT_SLOT + 1), mid,
                              jnp.where(mine & (slot == SHIFT_SLOT + 2), lo, 0.0))).astype(bf16)
            if hh == 0:
                o_ref[0, 0, rows_of(r), :] = patch
            else:
                o_ref[0, 0, rows_of(r), :] = o_ref[0, 0, rows_of(r), :] + patch
            return carry

        lax.fori_loop(0, gq, per_tile, 0)


def _row_shift(q, qcat, kb, kcat, kbm, kcm, jmin, mflag, n_meta):
    b, l, _ = q.shape
    tile = min(ATT_TILE, l)
    nq = l // tile
    gq = min(ATT_GROUP, nq)
    rows = gq * tile
    grid_spec = pltpu.PrefetchScalarGridSpec(
        num_scalar_prefetch=2,
        grid=(b, N_HEADS // 2, nq // gq),
        in_specs=[
            pl.BlockSpec((1, rows, LANES), lambda b, p, i, *_: (b, i, p)),
            pl.BlockSpec((1, rows, LANES), lambda b, p, i, *_: (b, i, 0)),
            pl.BlockSpec((1, LANES, l), lambda b, p, i, *_: (b, p, 0)),
            pl.BlockSpec((1, LANES, l), lambda b, p, i, *_: (b, 0, 0)),
            pl.BlockSpec((LANES, LANES), lambda b, p, i, *_: (p, 0)),
            pl.BlockSpec((LANES, LANES), lambda b, p, i, *_: (0, 0)),
        ],
        out_specs=pl.BlockSpec((1, 1, rows, LANES), lambda b, p, i, *_: (b, p, i, 0)),
    )
    return pl.pallas_call(
        functools.partial(_rowshift_kernel, tile, nq, gq, n_meta),
        grid_spec=grid_spec,
        out_shape=jax.ShapeDtypeStruct((b, N_HEADS // 2, l, LANES), bf16),
        compiler_params=_cparams("parallel", "parallel", "arbitrary"),
        name="row_shift",
    )(jmin, mflag, q, qcat, kb, kcat, kbm, kcm)


def _sattn_kernel(past, tn, q_ref, kn_ref, vn_ref, ckt_ref, cvt_ref, lft_ref, o_ref):
    for i in range(q_ref.shape[0]):
        _sattn_one(past, tn, i, q_ref, kn_ref, vn_ref, ckt_ref, cvt_ref, lft_ref, o_ref)


def _sattn_one(past, tn, i, q_ref, kn_ref, vn_ref, ckt_ref, cvt_ref, lft_ref, o_ref):
    s_len = lft_ref.shape[-1]
    rows = N_HEADS * tn
    c = lft_ref[i]
    lane_h = lax.broadcasted_iota(jnp.int32, c.shape, 1)
    sh = 1
    while sh < s_len:
        c = c + jnp.where(lane_h >= sh, pltpu.roll(c, sh, axis=1), 0.0)
        sh *= 2
    cexp = jnp.concatenate([jnp.broadcast_to(c[h:h + 1, :], (tn, s_len)) for h in range(N_HEADS)], axis=0)
    rix = lax.broadcasted_iota(jnp.int32, (rows, s_len), 0)
    lix = lax.broadcasted_iota(jnp.int32, (rows, s_len), 1)
    qpos = past + rix % tn
    cq = jnp.sum(jnp.where(lix == qpos, cexp, 0.0), axis=1, keepdims=True)
    pad = jnp.zeros((s_len - past - tn, D_ATT), bf16)
    k_new = jnp.concatenate([kn_ref[i], pad], axis=0)
    v_new = jnp.concatenate([vn_ref[i], pad], axis=0)
    qrep = jnp.concatenate([q_ref[i].astype(f32)] * N_HEADS, axis=0)
    r5 = lax.broadcasted_iota(jnp.int32, (rows, D_ATT), 0)
    l5 = lax.broadcasted_iota(jnp.int32, (rows, D_ATT), 1)
    qrows = jnp.where(l5 // HEAD_DIM == r5 // tn, qrep, 0.0).astype(bf16)
    s = jnp.concatenate([_dot(qrows, ckt_ref[i].astype(bf16)), _dot_nt(qrows, k_new)], axis=1)
    s = s + (cq - cexp) * LOG2E
    s = jnp.where(lix <= qpos, s, NEG_INF)
    m = jnp.max(s, axis=1, keepdims=True)
    pe = jnp.exp2(s - m)
    den = jnp.sum(pe, axis=1, keepdims=True)
    pb = pe.astype(bf16)
    o = (_dot_nt(pb[:, :past], cvt_ref[i].astype(bf16)) + _dot(pb[:, past:], v_new)) / den
    lo = lax.broadcasted_iota(jnp.int32, (tn, D_ATT), 1)
    out = jnp.zeros((tn, D_ATT), f32)
    for h in range(N_HEADS):
        out = out + jnp.where(lo // HEAD_DIM == h, o[h * tn:(h + 1) * tn, :], 0.0)
    o_ref[i] = out.astype(bf16)


def _sample_attention(q, kn, vn, ckt, cvt, lft):
    bs, tn, _ = q.shape
    past = ckt.shape[2]
    s_len = lft.shape[-1]
    per_step = 2 if bs % 2 == 0 else 1
    blk = lambda n, w: pl.BlockSpec((per_step, n, w), lambda b: (b, 0, 0))
    return pl.pallas_call(
        functools.partial(_sattn_kernel, past, tn),
        grid=(bs // per_step,),
        in_specs=[blk(tn, D_ATT), blk(tn, D_ATT), blk(tn, D_ATT), blk(D_ATT, past), blk(D_ATT, past),
                  blk(N_HEADS, s_len)],
        out_specs=blk(tn, D_ATT),
        out_shape=jax.ShapeDtypeStruct((bs, tn, D_ATT), bf16),
        compiler_params=_cparams("parallel"),
        name="sample_attention",
    )(q, kn, vn, ckt, cvt, lft)


def _s5prep_kernel(n_sq, are_ref, aim_ref, ldt_ref, btr_ref, bti_ref, cre_ref, cim_ref,
                   aqr_ref, aqi_ref, apr_ref, api_ref, wr_ref, wi_ref, ctr_ref, cti_ref, kk_ref):
    are = are_ref[...]
    aim = aim_ref[...]
    dt = jnp.exp(ldt_ref[...])
    mag = jnp.exp(are * dt)
    ph = aim * dt
    abr = mag * jnp.cos(ph)
    abi = mag * jnp.sin(ph)
    nr = abr - 1.0
    den = are * are + aim * aim
    fr = (nr * are + abi * aim) / den
    fi = (abi * are - nr * aim) / den
    pw = [(jnp.ones_like(abr), jnp.zeros_like(abr))]
    for _ in range(S5_Q):
        pr, pi = pw[-1]
        pw.append((pr * abr - pi * abi, pr * abi + pi * abr))
    aqr_ref[...], aqi_ref[...] = pw[S5_Q]
    pr, pi = pw[S5_Q]
    for _ in range(n_sq):
        pr, pi = pr * pr - pi * pi, 2.0 * pr * pi
    apr_ref[...] = pr
    api_ref[...] = pi
    p_states = are.shape[1]
    ident = (lax.broadcasted_iota(jnp.int32, (p_states, p_states), 0)
             == lax.broadcasted_iota(jnp.int32, (p_states, p_states), 1)).astype(bf16)
    lane = lax.broadcasted_iota(jnp.int32, (SSM_GROUP, S5_Q * SSM_GROUP), 1)
    hi = functools.partial(lax.dot_general, dimension_numbers=(((1,), (1,)), ((), ())),
                           precision=lax.Precision.HIGHEST, preferred_element_type=f32)
    for g in range(are.shape[0]):
        row = slice(g, g + 1)
        btr = btr_ref[g]
        bti = bti_ref[g]
        bbr = fr[row] * btr - fi[row] * bti
        bbi = fr[row] * bti + fi[row] * btr
        cre = cre_ref[g]
        cim = cim_ref[g]
        car_all, cai_all = [cre], [cim]
        for q in range(S5_Q):
            er, ei = pw[S5_Q - 1 - q]
            wr_ref[q, g] = er[row] * bbr - ei[row] * bbi
            wi_ref[q, g] = er[row] * bbi + ei[row] * bbr
            gr, gi = pw[q + 1]
            car_all.append(cre * gr[row] - cim * gi[row])
            cai_all.append(cre * gi[row] + cim * gr[row])
        ctr_ref[g] = _dot_nt(ident, jnp.concatenate(car_all[1:], axis=0).astype(bf16))
        cti_ref[g] = _dot_nt(ident, jnp.concatenate(cai_all[1:], axis=0).astype(bf16))
        kt = (hi(bbr, jnp.concatenate(car_all[:S5_Q], axis=0))
              - hi(bbi, jnp.concatenate(cai_all[:S5_Q], axis=0)))
        kk_ref[g] = jnp.concatenate(
            [kt if qi == 0 else jnp.where(lane >= qi * SSM_GROUP, pltpu.roll(kt, qi * SSM_GROUP, axis=1), 0.0)
             for qi in range(S5_Q)], axis=0)


def _s5_prep(a_re, a_im, log_dt, bt_re, bt_im, c_re, c_im, n_sq):
    g, p = a_re.shape
    sd = jax.ShapeDtypeStruct
    qh = S5_Q * SSM_GROUP
    return pl.pallas_call(
        functools.partial(_s5prep_kernel, n_sq),
        out_shape=[sd((g, p), f32)] * 4 + [sd((S5_Q,) + bt_re.shape, f32)] * 2 + [sd((g, p, qh), f32)] * 2
        + [sd((g, qh, qh), f32)],
        name="s5_prep",
    )(a_re, a_im, log_dt, bt_re, bt_im, c_re, c_im)


def _s5_kernel(chain, n_seg, n_sb, u_ref, w8_ref, toep_ref, cpow_ref, aqr_ref, aqi_ref, apr_ref, api_ref, d8_ref,
               x0r_ref, x0i_ref, y_ref, sr_ref, si_ref, xr_ref, xi_ref):
    aqr = aqr_ref[0]
    aqi = aqi_ref[0]
    w8 = w8_ref[0]
    toep = toep_ref[0]
    cpow = cpow_ref[0]
    d8 = d8_ref[0]

    def step(with_y, sb):
        rows = pl.ds(sb, n_seg, stride=n_sb) if n_sb > 1 else pl.ds(0, n_seg)
        uf = jnp.concatenate([u_ref[0, 0, q, rows, :] for q in range(S5_Q)], axis=1)
        ub = uf.astype(bf16)
        xr = xr_ref[...]
        xi = xi_ref[...]
        if with_y:
            xc = jnp.concatenate([xr, xi], axis=1).astype(bf16)
            lag = jnp.concatenate([_dot(ub[:, :(c + 1) * MXU_TILE], toep[:(c + 1) * MXU_TILE, c * MXU_TILE:(c + 1) * MXU_TILE])
                                   for c in range(S5_Q * LANES // MXU_TILE)], axis=1)
            y = lag + _dot(xc, cpow) + d8 * uf
            for q in range(S5_Q):
                y_ref[0, 0, q, rows, :] = y[:, q * LANES:(q + 1) * LANES]
        pj = _dot(ub, w8)
        xr_ref[...] = aqr * xr - aqi * xi + pj[:, :STATE_BLOCK]
        xi_ref[...] = aqr * xi + aqi * xr + pj[:, STATE_BLOCK:]

    if chain:
        xr_ref[...] = jnp.zeros(xr_ref.shape, f32)
        xi_ref[...] = jnp.zeros(xi_ref.shape, f32)
        for sb in range(n_sb):
            step(False, sb)
        apr = apr_ref[0]
        api = api_ref[0]

        def scan(c, carry):
            kr, ki = carry
            er = xr_ref[pl.ds(c, 1), :]
            ei = xi_ref[pl.ds(c, 1), :]
            xr_ref[pl.ds(c, 1), :] = kr
            xi_ref[pl.ds(c, 1), :] = ki
            return apr * kr - api * ki + er, apr * ki + api * kr + ei

        kr, ki = lax.fori_loop(0, n_seg, scan, (x0r_ref[0], x0i_ref[0]))
        sr_ref[0] = kr
        si_ref[0] = ki
        for sb in range(n_sb):
            step(True, sb)
    else:
        xr_ref[...] = x0r_ref[0]
        xi_ref[...] = x0i_ref[0]
        for sb in range(n_sb):
            step(True, sb)
        sr_ref[0] = xr_ref[...]
        si_ref[0] = xi_ref[...]


def _s5(u, x0r, x0i, weights, n_seg, chain, name):
    bq, _, _, m, _ = u.shape
    n_sb = m // n_seg
    r0 = x0r.shape[1]
    per_k = lambda a: pl.BlockSpec((1,) + a.shape[1:], lambda b, k: (k, 0, 0))
    st = pl.BlockSpec((1, r0, STATE_BLOCK), lambda b, k: (b, 0, k))
    tok = lambda: pl.BlockSpec((1, 1, S5_Q, m, LANES), lambda b, k: (b, k, 0, 0, 0))
    sd = jax.ShapeDtypeStruct
    return pl.pallas_call(
        functools.partial(_s5_kernel, chain, n_seg, n_sb),
        grid=(bq, N_SSM_BLOCKS),
        in_specs=[tok()] + [per_k(a) for a in weights] + [st, st],
        out_specs=[tok(), st, st],
        out_shape=[sd(u.shape, f32), sd(x0r.shape, f32), sd(x0i.shape, f32)],
        scratch_shapes=[pltpu.VMEM((n_seg, STATE_BLOCK), f32)] * 2,
        compiler_params=_cparams("parallel", "parallel"),
        name=name,
    )(u, *weights, x0r, x0i)


def _out_kernel(x_ref, att_ref, ga_ref, ys_ref, gs_ref, wg_ref, bg_ref, wo_ref, y_ref, ys_scr):
    ga = ga_ref[0].astype(f32)
    a = att_ref[0].astype(f32) * (ga * jax.nn.sigmoid(ga))
    rows = x_ref.shape[1]
    for kb in range(N_SSM_BLOCKS):
        for qq in range(S5_Q):
            ys_scr[kb, pl.ds(qq, rows // S5_Q, stride=S5_Q), :] = ys_ref[0, kb, qq]
    ys = jnp.concatenate([ys_scr[kb] for kb in range(N_SSM_BLOCKS)], axis=1)
    z = 0.5 * ys * (1.0 + jnp.tanh(0.7978845608028654 * (ys + 0.044715 * (ys * ys * ys))))
    gate = jax.nn.sigmoid(_dot(z.astype(bf16), wg_ref[...]) + bg_ref[...])
    gs = gs_ref[0].astype(f32)
    s5 = z * gate * (gs * jax.nn.sigmoid(gs))
    cat = jnp.concatenate([a, s5], axis=1).astype(bf16)
    y_ref[0] = x_ref[0] + _dot(cat, wo_ref[...])


def _outproj(x, att, ga, ys, gs, wg, bg, wo, tile, name):
    bx, lx, d = x.shape
    full = lambda a: pl.BlockSpec(a.shape, lambda b, j: (0,) * a.ndim)
    row = lambda w: pl.BlockSpec((1, tile, w), lambda b, j: (b, j, 0))
    return pl.pallas_call(
        _out_kernel,
        grid=(bx, lx // tile),
        in_specs=[row(d), row(D_ATT), row(D_ATT), _s5_tokens_spec(tile, lambda b, j: (b, 0, 0, j, 0)), row(D_SSM),
                  full(wg), full(bg), full(wo)],
        out_specs=row(d),
        out_shape=jax.ShapeDtypeStruct(x.shape, f32),
        scratch_shapes=[pltpu.VMEM((N_SSM_BLOCKS, tile, LANES), f32)],
        compiler_params=_cparams("parallel", "parallel"),
        name=name,
    )(x, att, ga, ys, gs, wg, bg, wo)


def _lower_tri(n):
    r = jnp.arange(n)
    return (r[None, :] <= r[:, None]).astype(bf16)


def kernel(x_prompt, x_sample, cache_k, cache_v, cache_logf, state_s5_re, state_s5_im, meta_tokens, norm_g,
           w_in, b_f, q_norm_g, k_norm_g, s5_a_re, s5_a_im, s5_log_dt, s5_b_re, s5_b_im, s5_c_re, s5_c_im,
           s5_d, w_glu, b_glu, w_out):
    assert norm_g.shape[0] == 1, "single-layer step"
    bp, lp, d = x_prompt.shape
    bs, ts, _ = x_sample.shape
    n_meta = meta_tokens.shape[0]
    past = cache_k.shape[2]
    G, P = N_GROUPS, SSM_STATE

    w = w_in[0]
    o3 = 3 * D_ATT
    o4 = o3 + N_HEADS
    wm = jnp.concatenate([w[:, :o3], w[:, o4:]], axis=1).astype(bf16)
    wf = jnp.repeat(w[:, o3:o4], AUG, axis=1).astype(bf16)
    bfr = jnp.repeat(b_f[0], AUG)[None, :].astype(f32)
    g_in = norm_g[0][None, :].astype(f32)
    qg = (jnp.tile(q_norm_g[0], N_HEADS) * (HEAD_DIM ** -0.5 * LOG2E))[None, :].astype(f32)
    kg = jnp.tile(k_norm_g[0], N_HEADS)[None, :].astype(f32)
    hid = jnp.arange(D_ATT) // HEAD_DIM
    eh = ((hid[:, None] == hid[None, :]).astype(f32) / HEAD_DIM).astype(bf16)
    sel = (jnp.arange(LANES)[None, :] == AUG * jnp.arange(N_HEADS)[:, None]).astype(bf16)
    m_bound = 8.0 * jnp.max(jnp.abs(q_norm_g[0])) * jnp.max(jnp.abs(k_norm_g[0])) * 1.02
    robust = m_bound > SHIFT_SAFE
    negm = jnp.full((1, LANES), -LOG2E, f32) * jnp.where(robust, 0.0, m_bound)
    params = (g_in, wm, wf, bfr, qg, kg, eh, negm, sel)

    tm = n_meta
    meta = _inproj(meta_tokens[None].astype(f32), jnp.zeros((1, 1, LANES), f32), params, tm, _lower_tri(tm), True,
                   "inproj_meta")
    (_, kb_m, vb_m, kf_m, vf_m, lft_m, edge_m, _, kcat_m, _, u_m, _) = meta
    cum_meta_end = edge_m[:, -1, 1:2, :]
    c0 = jnp.broadcast_to(cum_meta_end, (bp, 1, LANES))
    tile_p = min(ROW_TILE, lp)
    tb_p = min(LANES, tile_p)
    meta_cols = lambda a: jnp.pad(jnp.swapaxes(a, 1, 2), ((0, 0), (0, 0), (0, LANES - tm)))
    (q_p, kb_p, vb_p, kt_p, vt_p, lft_p, edge_p, qcat_p, kcat_p, ga_p, u_p, gs_p) = _inproj(
        x_prompt, c0, params, tile_p, _lower_tri(tb_p), True, "inproj_prompt",
        meta_t=(meta_cols(kf_m), meta_cols(vf_m), n_meta))
    ls = bs * ts
    tile_s = min(ROW_TILE, ls)
    (q_s, kb_s, vb_s, kf_s, vf_s, lft_s, _, _, _, ga_s, u_s, gs_s) = _inproj(
        x_sample.reshape(1, ls, d), jnp.zeros((1, 1, LANES), f32), params, tile_s,
        _lower_tri(min(LANES, tile_s)), False, "inproj_sample")

    tile_a = min(ATT_TILE, lp)
    nq = lp // tile_a
    per_tile = tile_a // tb_p
    edges = edge_p[:, :, :2, ::AUG]
    q_start = edges[:, 0::per_tile, 0, :]
    k_end = edges[:, per_tile - 1::per_tile, 1, :]
    dmax = q_start[:, :, None, :] - k_end[:, None, :, :]
    padm = lambda a: jnp.pad(a[0], ((0, LANES - tm), (0, 0)))
    kbm, vbm, kcm = padm(kb_m).T, padm(vb_m), padm(kcat_m).T

    def skip_tables(thr):
        jmin = jnp.sum((dmax < -thr).astype(jnp.int32), axis=2)
        jmin = jnp.minimum(jmin, jnp.arange(nq, dtype=jnp.int32)[None, :, None])
        mflag = (q_start - cum_meta_end[:, :, ::AUG] >= -thr).astype(jnp.int32)
        return jnp.transpose(jmin, (0, 2, 1)).reshape(-1), jnp.transpose(mflag, (0, 2, 1)).reshape(-1)

    def attend_bounded():
        return _prompt_attention(q_p, qcat_p, kb_p, vb_p, kcat_p, kbm, vbm, kcm, *skip_tables(EXP_ZERO), n_meta)

    def attend_robust():
        tables = skip_tables(EXP_ZERO + 2.0 * m_bound)
        shift = _row_shift(q_p, qcat_p, kb_p, kcat_p, kbm, kcm, *tables, n_meta)
        qcat_r = (qcat_p.astype(f32) + jnp.sum(shift.astype(f32), axis=1)).astype(bf16)
        return _prompt_attention(q_p, qcat_r, kb_p, vb_p, kcat_p, kbm, vbm, kcm, *tables, n_meta)

    att_p = lax.cond(robust, attend_robust, attend_bounded)

    lft_new = jnp.transpose(lft_s.reshape(N_HEADS, bs, ts), (1, 0, 2))
    s_len = -(-(past + ts) // LANES) * LANES
    lft = jnp.concatenate([jnp.transpose(cache_logf[0].astype(f32), (0, 2, 1)), lft_new,
                           jnp.zeros((bs, N_HEADS, s_len - past - ts), f32)], axis=2)
    cache_t = lambda c: jnp.transpose(c[0], (0, 2, 3, 1)).reshape(bs, D_ATT, past)
    att_s = _sample_attention(q_s.reshape(bs, ts, D_ATT), kb_s.reshape(bs, ts, D_ATT), vb_s.reshape(bs, ts, D_ATT),
                              cache_t(cache_k), cache_t(cache_v), lft)

    n_seg_p = min(PROMPT_SEGMENTS, lp // S5_Q)
    n_sb_p = lp // (S5_Q * n_seg_p)
    n_sq = n_sb_p.bit_length() - 1
    assert n_sb_p == 1 << n_sq and ts % S5_Q == 0
    tr = lambda a: jnp.transpose(a[0], (0, 2, 1)).astype(f32)
    aqr, aqi, apr, api, wr, wi, ctr, cti, kk = _s5_prep(
        s5_a_re[0].astype(f32), s5_a_im[0].astype(f32), s5_log_dt[0][:, None].astype(f32),
        tr(s5_b_re), tr(s5_b_im), s5_c_re[0].astype(f32), s5_c_im[0].astype(f32), n_sq)
    NB, G8, Q = N_SSM_BLOCKS, GROUPS_PER_BLOCK, S5_Q
    qgh = Q * LANES

    def spread(x, col_of_src, row_group, col_group):
        n_src = x.shape[-1]
        n_rows = x.shape[1]
        n_cols = col_of_src.shape[0]
        sel = (col_of_src[None, :] == jnp.arange(n_src)[:, None]).astype(bf16)
        keep = row_group(jnp.arange(n_rows))[:, None] == col_group(jnp.arange(n_cols))[None, :]
        y = jnp.einsum("krs,sc->krc", x.astype(bf16), sel, preferred_element_type=f32)
        return jnp.where(keep[None], y, 0.0).astype(bf16)

    cols_state = jnp.arange(STATE_BLOCK) % P
    cols_qgh = (jnp.arange(qgh) // LANES) * SSM_GROUP + jnp.arange(qgh) % SSM_GROUP
    group_of_qgh = lambda i: (i // SSM_GROUP) % G8
    by_block = lambda a: jnp.transpose(a.reshape(Q, NB, G8 * SSM_GROUP, P), (1, 0, 2, 3)).reshape(NB, qgh, P)
    w_part = lambda a: spread(by_block(a), cols_state, group_of_qgh, lambda i: i // P)
    w8 = jnp.concatenate([w_part(wr), w_part(wi)], axis=2)
    c_part = lambda a: spread(a.reshape(NB, STATE_BLOCK, Q * SSM_GROUP), cols_qgh, lambda i: i // P, group_of_qgh)
    cpow = jnp.concatenate([c_part(ctr), c_part(-cti)], axis=1)
    kk_rows = jnp.transpose(kk.reshape(NB, G8, Q, SSM_GROUP, Q * SSM_GROUP), (0, 2, 1, 3, 4)).reshape(NB, qgh, -1)
    toep = spread(kk_rows, cols_qgh, group_of_qgh, group_of_qgh)
    lane_k = lambda a: a.reshape(NB, 1, -1)
    d8 = jnp.tile(lane_k(s5_d[0].astype(f32)), (1, 1, Q))
    weights = lambda pr_, pi_: (w8, toep, cpow, lane_k(aqr), lane_k(aqi), lane_k(pr_), lane_k(pi_), d8)

    assert tm == ts
    n_small = -(-(bs + 1) // SUBLANES) * SUBLANES
    u_small = jnp.concatenate([u_s, u_m, jnp.zeros(_s5_tokens_shape(1, (n_small - bs - 1) * ts), f32)], axis=3)
    pad_state = lambda a: jnp.pad(a[0].reshape(bs, G * P).astype(f32), ((0, n_small - bs), (0, 0)))[None]
    ys_small, sr_small, si_small = _s5(u_small, pad_state(state_s5_re), pad_state(state_s5_im),
                                       weights(aqr, aqi), n_small, False, "s5_short")
    x0r = jnp.broadcast_to(sr_small[:, bs:bs + 1, :], (bp, 1, G * P))
    x0i = jnp.broadcast_to(si_small[:, bs:bs + 1, :], (bp, 1, G * P))
    ys_p, sr_p, si_p = _s5(u_p, x0r, x0i, weights(apr, api), n_seg_p, True, "s5_prompt")

    wg = w_glu[0].astype(bf16)
    bg = b_glu[0][None, :].astype(f32)
    wo = w_out[0].astype(bf16)
    y_prompt = _outproj(x_prompt, att_p, ga_p, ys_p, gs_p, wg, bg, wo, min(OUT_TILE, lp), "outproj_prompt")
    y_sample = _outproj(x_sample.reshape(1, ls, d), att_s.reshape(1, ls, D_ATT), ga_s, ys_small[:, :, :, :ls // S5_Q],
                        gs_s, wg, bg, wo, tile_s, "outproj_sample").reshape(bs, ts, d)

    heads_t = lambda a: jnp.transpose(a.reshape(bp, N_HEADS, HEAD_DIM, n_meta + lp), (0, 3, 1, 2))[None]
    new_k_prompt = heads_t(kt_p)
    new_v_prompt = heads_t(vt_p)
    lft_all = jnp.concatenate([jnp.broadcast_to(lft_m, (bp, N_HEADS, n_meta)), lft_p], axis=2)
    new_logf_prompt = jnp.transpose(lft_all, (0, 2, 1))[None]
    new_s5_re_prompt = sr_p.reshape(1, bp, G, P)
    new_s5_im_prompt = si_p.reshape(1, bp, G, P)
    new_k_sample = kf_s.reshape(1, bs, ts, N_HEADS, HEAD_DIM)
    new_v_sample = vf_s.reshape(1, bs, ts, N_HEADS, HEAD_DIM)
    new_logf_sample = jnp.transpose(lft_new, (0, 2, 1))[None]
    new_s5_re_sample = sr_small[:, :bs].reshape(1, bs, G, P)
    new_s5_im_sample = si_small[:, :bs].reshape(1, bs, G, P)
    return (y_prompt, y_sample, new_k_prompt, new_v_prompt, new_logf_prompt, new_s5_re_prompt, new_s5_im_prompt,
            new_k_sample, new_v_sample, new_logf_sample, new_s5_re_sample, new_s5_im_sample)
```

```python
import functools

import jax
import jax.numpy as jnp
from jax import lax
from jax.experimental import pallas as pl
from jax.experimental.pallas import tpu as pltpu

N_HEADS = 8
HEAD_DIM = 64
D_ATT = N_HEADS * HEAD_DIM
SSM_GROUP = 16
SSM_STATE = 64
D_SSM = 512
N_GROUPS = D_SSM // SSM_GROUP
EPS = 1e-6
NEG_INF = -1e30

LANES = 128
MXU_TILE = 256
SUBLANES = 8
AUG = LANES // N_HEADS
SHIFT_SLOT = 7
GROUPS_PER_BLOCK = LANES // SSM_GROUP
STATE_BLOCK = GROUPS_PER_BLOCK * SSM_STATE
N_SSM_BLOCKS = D_SSM // LANES
VMEM_LIMIT = 56 * 1024 * 1024
EXP_ZERO = 105.0
SHIFT_SAFE = 32.0

ROW_TILE = 512
OUT_TILE = 1024
ATT_TILE = 256
ATT_GROUP = 32
ATT_UNROLL = 32
ATT_TAIL_UNROLL = 8
LOG2E = 1.4426950408889634
PROMPT_SEGMENTS = 512
S5_Q = 8

bf16 = jnp.bfloat16
f32 = jnp.float32


def _cparams(*sem):
    return pltpu.CompilerParams(dimension_semantics=sem, vmem_limit_bytes=VMEM_LIMIT)


def _split3(x):
    hi = x.astype(bf16).astype(f32)
    r1 = x - hi
    mid = r1.astype(bf16).astype(f32)
    lo = (r1 - mid).astype(bf16).astype(f32)
    return hi, mid, lo


def _dot(a, b):
    return jnp.dot(a, b, preferred_element_type=f32)


def _dot_nt(a, b):
    return lax.dot_general(a, b, (((1,), (1,)), ((), ())), preferred_element_type=f32)


def _s5_tokens_shape(b, l):
    return (b, N_SSM_BLOCKS, S5_Q, l // S5_Q, LANES)


def _s5_tokens_spec(rows, index_map, **kw):
    return pl.BlockSpec((1, N_SSM_BLOCKS, S5_Q, rows // S5_Q, LANES), index_map, **kw)


def _inproj_kernel(chain, n_blk, join, *refs):
    n_in = 12
    if chain:
        c0_ref, carry_ref = refs[1], refs[-5 if join else -2]

        @pl.when(pl.program_id(1) == 0)
        def _():
            carry_ref[...] = c0_ref[0]
    if not join:
        _inproj_tile(chain, n_blk, False, *refs)
        return
    kmt_ref, vmt_ref = refs[n_in:n_in + 2]
    refs = refs[:n_in] + refs[n_in + 2:]
    kb_ref, kf_ref, vf_ref, kcat_ref = refs[n_in + 1], refs[n_in + 3], refs[n_in + 4], refs[n_in + 8]
    prev_ref, prevc_ref, tails_ref = refs[-3:]
    j = pl.program_id(1)
    n_tiles = pl.num_programs(1) - 2
    lane_t = lax.broadcasted_iota(jnp.int32, (D_ATT, LANES), 1)

    def emit():
        kb_ref[0] = prev_ref[0].T.astype(bf16)
        kcat_ref[0] = prevc_ref[...].T.astype(bf16)
        for idx, t_ref in enumerate((kf_ref, vf_ref)):
            rolled = pltpu.roll(prev_ref[idx].T, join, axis=1)
            first = jnp.where(lane_t < join, tails_ref[idx], rolled[:, :LANES])
            t_ref[0] = first if rolled.shape[1] == LANES else jnp.concatenate([first, rolled[:, LANES:]], axis=1)
            tails_ref[idx] = jnp.where(j > 0, rolled[:, :LANES], tails_ref[idx])

    @pl.when(j == 0)
    def _():
        tails_ref[0] = kmt_ref[0]
        tails_ref[1] = vmt_ref[0]
        prev_ref[...] = jnp.zeros(prev_ref.shape, f32)
        prevc_ref[...] = jnp.zeros(prevc_ref.shape, f32)

    @pl.when(j < n_tiles)
    def _():
        emit()
        _inproj_tile(chain, n_blk, True, *refs)

    pl.when(j == n_tiles)(emit)

    @pl.when(j == n_tiles + 1)
    def _():
        for idx, t_ref in enumerate((kf_ref, vf_ref)):
            t_ref[0] = jnp.zeros(t_ref.shape[1:], f32)
            t_ref[0, :, :LANES] = tails_ref[idx]


def _inproj_tile(chain, n_blk, join, x_ref, c0_ref, g_ref, wm_ref, wf_ref, bf_ref, qg_ref, kg_ref, eh_ref,
                 tri_ref, negm_ref, sel_ref, q_ref, kb_ref, vb_ref, kf_ref, vf_ref, lft_ref, edge_ref,
                 qcat_ref, kcat_ref, ga_ref, u_ref, gs_ref, carry_ref, us_ref, prev_ref=None, prevc_ref=None,
                 tails_ref=None):
    x = x_ref[0]
    ms = jnp.mean(x * x, axis=-1, keepdims=True)
    xn = (x * lax.rsqrt(ms + EPS) * g_ref[...]).astype(bf16)
    z = _dot(xn, wm_ref[...])
    q = z[:, 0 * D_ATT:1 * D_ATT]
    k = z[:, 1 * D_ATT:2 * D_ATT]
    v = z[:, 2 * D_ATT:3 * D_ATT]
    eh = eh_ref[...]

    def head_mean(sq):
        sq = sq.astype(bf16)
        return jnp.concatenate([_dot(sq[:, c:c + MXU_TILE], eh[c:c + MXU_TILE, c:c + MXU_TILE])
                                for c in range(0, D_ATT, MXU_TILE)], axis=1)

    qn = q * lax.rsqrt(head_mean(q * q) + EPS) * qg_ref[...]
    kn = k * lax.rsqrt(head_mean(k * k) + EPS) * kg_ref[...]
    q_ref[0] = qn.astype(bf16)
    vb_ref[0] = v.astype(bf16)
    if join:
        prev_ref[0] = kn
        prev_ref[1] = v
    else:
        kb_ref[0] = kn.astype(bf16)
        kf_ref[0] = kn
        vf_ref[0] = v
    ga_ref[0] = z[:, 3 * D_ATT:4 * D_ATT].astype(bf16)
    gs_ref[0] = z[:, 4 * D_ATT + D_SSM:].astype(bf16)
    for kb in range(N_SSM_BLOCKS):
        us_ref[kb] = z[:, 4 * D_ATT + kb * LANES:4 * D_ATT + (kb + 1) * LANES]
    for kb in range(N_SSM_BLOCKS):
        for qq in range(S5_Q):
            u_ref[0, kb, qq] = us_ref[kb, pl.ds(qq, x.shape[0] // S5_Q, stride=S5_Q), :]

    zf = _dot(xn, wf_ref[...]) + bf_ref[...]
    lf = jnp.minimum(zf, 0.0) - jnp.log1p(jnp.exp(-jnp.abs(zf)))
    sel = sel_ref[...]
    lft_ref[0] = sum(_dot_nt(sel, part.astype(bf16)) for part in _split3(lf))

    carry = carry_ref[...] if chain else jnp.zeros((1, LANES), f32)
    tri = tri_ref[...]
    tb = x.shape[0] // n_blk
    cums = []
    for r in range(n_blk):
        hi, mid, lo = _split3(lf[r * tb:(r + 1) * tb])
        c = _dot(tri, hi.astype(bf16)) + _dot(tri, mid.astype(bf16)) + _dot(tri, lo.astype(bf16)) + carry
        cums.append(c)
        carry = c[tb - 1:tb, :]
        edge_ref[0, r] = jnp.concatenate([c[0:1, :], carry, jnp.zeros((SUBLANES - 2, LANES), f32)], axis=0)
    cum = cums[0] if n_blk == 1 else jnp.concatenate(cums, axis=0)
    if chain:
        carry_ref[...] = carry

    slot = lax.broadcasted_iota(jnp.int32, (1, LANES), 1) % AUG
    hi, mid, lo = _split3(cum * LOG2E)
    qc = jnp.where(slot == 0, hi, jnp.where(slot == 1, mid, jnp.where(slot == 2, lo,
                   jnp.where(slot <= 6, 1.0, 0.0))))
    kc = jnp.where(slot <= 2, 1.0, jnp.where(slot == 3, -hi, jnp.where(slot == 4, -mid,
                   jnp.where(slot == 5, -lo, jnp.where(slot == 6, negm_ref[...],
                             jnp.where(slot <= SHIFT_SLOT + 2, 1.0, 0.0))))))
    qcat_ref[0] = qc.astype(bf16)
    if join:
        prevc_ref[...] = kc
    else:
        kcat_ref[0] = kc.astype(bf16)


def _inproj(x, c0, params, tile, tri, chain, name, meta_t=None):
    bx, lx, d = x.shape
    n_tiles = lx // tile
    n_blk = tile // tri.shape[0]
    join = 0 if meta_t is None else meta_t[2]
    last = n_tiles - 1
    tj = (lambda j: jnp.minimum(j, last)) if join else (lambda j: j)
    tprev = lambda j: jnp.maximum(j - 1, 0)
    full = lambda a: pl.BlockSpec(a.shape, lambda b, j: (0,) * a.ndim)
    row = lambda w: pl.BlockSpec((1, tile, w), lambda b, j: (b, tj(j), 0))
    sd = jax.ShapeDtypeStruct
    rows = lambda w, t: (row(w), sd((bx, lx, w), t))
    kv = (rows(D_ATT, f32) if not join else
          (pl.BlockSpec((1, D_ATT, tile), lambda b, j: (b, 0, tprev(j))), sd((bx, D_ATT, join + lx), f32)))
    cols = lambda w: (pl.BlockSpec((1, w, tile), lambda b, j: (b, 0, jnp.minimum(tprev(j), last))), sd((bx, w, lx), bf16))
    outs = [rows(D_ATT, bf16), cols(D_ATT) if join else rows(D_ATT, bf16), rows(D_ATT, bf16), kv, kv,
            (pl.BlockSpec((1, N_HEADS, tile), lambda b, j: (b, 0, tj(j))), sd((bx, N_HEADS, lx), f32)),
            (pl.BlockSpec((1, n_blk, SUBLANES, LANES), lambda b, j: (b, tj(j), 0, 0)),
             sd((bx, n_tiles * n_blk, SUBLANES, LANES), f32)),
            rows(LANES, bf16), cols(LANES) if join else rows(LANES, bf16), rows(D_ATT, bf16),
            (_s5_tokens_spec(tile, lambda b, j: (b, 0, 0, tj(j), 0)), sd(_s5_tokens_shape(bx, lx), f32)),
            rows(D_SSM, bf16)]
    g_in, wm, wf, bfr, qg, kg, eh, negm, sel = params
    operands = (g_in, wm, wf, bfr, qg, kg, eh, tri, negm, sel) + (tuple(meta_t[:2]) if join else ())
    scratch = [pltpu.VMEM((1, LANES), f32), pltpu.VMEM((N_SSM_BLOCKS, tile, LANES), f32)]
    if join:
        scratch += [pltpu.VMEM((2, tile, D_ATT), f32), pltpu.VMEM((tile, LANES), f32), pltpu.VMEM((2, D_ATT, LANES), f32)]
    return pl.pallas_call(
        functools.partial(_inproj_kernel, chain, n_blk, join),
        grid=(bx, n_tiles + (2 if join else 0)),
        in_specs=[row(d), pl.BlockSpec((1, 1, LANES), lambda b, j: (b, 0, 0))] + [full(a) for a in operands],
        out_specs=[o[0] for o in outs],
        out_shape=[o[1] for o in outs],
        scratch_shapes=scratch,
        compiler_params=_cparams("parallel", "arbitrary"),
        name=name,
    )(x, c0, *operands)


def _attn_kernel(tile, nq, gq, n_meta, jmin_ref, mflag_ref, q_ref, qc_ref, kb_ref, vb_ref, kc_ref,
                 kbm_ref, vbm_ref, kcm_ref, o_ref, acc_ref, bnd_ref):
    b = pl.program_id(0)
    p = pl.program_id(1)
    g = pl.program_id(2)
    lane = lax.broadcasted_iota(jnp.int32, (1, LANES), 1)
    mcol = lax.broadcasted_iota(jnp.int32, (tile, LANES), 1)
    col = lax.broadcasted_iota(jnp.int32, (tile, tile), 1)
    row = lax.broadcasted_iota(jnp.int32, (tile, tile), 0)

    @pl.when(g == 0)
    def _():
        bnd_ref[0] = jnp.where(col <= row, -NEG_INF, NEG_INF)
        bnd_ref[1] = jnp.full((tile, tile), -NEG_INF, f32)
        bnd_ref[2] = jnp.full((tile, tile), NEG_INF, f32)

    def rows_of(j):
        return pl.ds(pl.multiple_of(j * tile, tile), tile)

    n_slots = 2 * gq
    base = (b * N_HEADS + 2 * p) * nq + g * gq

    def table_index(t):
        return base + (t // gq) * nq + t % gq

    def clear(t, carry):
        acc_ref[t] = jnp.zeros(acc_ref.shape[1:], f32)
        return carry

    lax.fori_loop(0, n_slots, clear, 0)

    def head_masks(hh):
        own = lane // HEAD_DIM == hh
        return own.astype(bf16), (~own).astype(bf16), (lane // AUG == 2 * p + hh).astype(bf16)

    def scores(tc, kt, kct):
        m_q, _, m_c = head_masks(tc // gq)
        rows = rows_of(tc % gq)
        q_op = jnp.concatenate([q_ref[0, rows, :] * m_q, qc_ref[0, rows, :] * m_c], axis=1)
        return _dot(q_op, jnp.concatenate([kt, kct], axis=0))

    def pv(tc, pe, vs):
        m_q, m_o, _ = head_masks(tc // gq)
        return _dot(pe, vs * m_q + m_o)

    def item_probs(t, j):
        tc = jnp.minimum(t, n_slots - 1)
        kind = jnp.where(t < n_slots, jnp.where(j == g * gq + tc % gq, 0, 1), 2)
        s = scores(tc, kb_ref[0, :, rows_of(j)], kc_ref[0, :, rows_of(j)])
        return jnp.exp2(jnp.minimum(s, bnd_ref[kind])).astype(bf16), tc

    def item_pv(pe, tc, j):
        acc_ref[tc] += pv(tc, pe, vb_ref[0, rows_of(j), :])

    def advance(t, j):
        last = jnp.logical_and(t < n_slots, j == g * gq + t % gq)
        t2 = jnp.where(last, t + 1, t)
        j2 = jnp.where(last, jmin_ref[table_index(jnp.minimum(t + 1, n_slots - 1))],
                       jnp.where(t < n_slots, j + 1, j))
        return t2.astype(jnp.int32), j2.astype(jnp.int32)

    def body(unroll, trip, carry):
        pe, tp, jp, t, j = carry
        for _ in range(unroll):
            pn, tn = item_probs(t, j)
            item_pv(pe, tp, jp)
            pe, tp, jp = pn, tn, j
            t, j = advance(t, j)
        return pe, tp, jp, t, j

    n_items = sum(g * gq + t % gq - jmin_ref[table_index(t)] + 1 for t in range(n_slots))
    long_trips = (n_items - 1) // ATT_UNROLL
    short_trips = (n_items - 1 - long_trips * ATT_UNROLL + ATT_TAIL_UNROLL - 1) // ATT_TAIL_UNROLL
    j0 = jmin_ref[base]
    pe0, t0 = item_probs(jnp.int32(0), j0)
    carry = (pe0, t0, j0) + advance(jnp.int32(0), j0)
    carry = lax.fori_loop(0, long_trips, functools.partial(body, ATT_UNROLL), carry)
    pe, tp, jp, _, _ = lax.fori_loop(0, short_trips, functools.partial(body, ATT_TAIL_UNROLL), carry)
    item_pv(pe, tp, jp)

    def meta_tile(t, carry):
        @pl.when(mflag_ref[table_index(t)] > 0)
        def _():
            s = jnp.where(mcol < n_meta, scores(t, kbm_ref[...], kcm_ref[...]), NEG_INF)
            acc_ref[t] += pv(t, jnp.exp2(s).astype(bf16), vbm_ref[...])
        return carry

    lax.fori_loop(0, n_slots, meta_tile, 0)

    for r in range(gq):
        a0 = acc_ref[r]
        a1 = acc_ref[gq + r]
        own = lane < HEAD_DIM
        sums = pltpu.roll(jnp.where(own, a1, a0), HEAD_DIM, axis=1)
        o_ref[0, r * tile:(r + 1) * tile, :] = (jnp.where(own, a0, a1) / sums).astype(bf16)


def _prompt_attention(q, qcat, kb, vb, kcat, kbm, vbm, kcm, jmin, mflag, n_meta):
    b, l, _ = q.shape
    tile = min(ATT_TILE, l)
    nq = l // tile
    gq = min(ATT_GROUP, nq)
    rows = gq * tile
    grid_spec = pltpu.PrefetchScalarGridSpec(
        num_scalar_prefetch=2,
        grid=(b, N_HEADS // 2, nq // gq),
        in_specs=[
            pl.BlockSpec((1, rows, LANES), lambda b, p, i, *_: (b, i, p)),
            pl.BlockSpec((1, rows, LANES), lambda b, p, i, *_: (b, i, 0)),
            pl.BlockSpec((1, LANES, l), lambda b, p, i, *_: (b, p, 0)),
            pl.BlockSpec((1, l, LANES), lambda b, p, i, *_: (b, 0, p)),
            pl.BlockSpec((1, LANES, l), lambda b, p, i, *_: (b, 0, 0)),
            pl.BlockSpec((LANES, LANES), lambda b, p, i, *_: (p, 0)),
            pl.BlockSpec((LANES, LANES), lambda b, p, i, *_: (0, p)),
            pl.BlockSpec((LANES, LANES), lambda b, p, i, *_: (0, 0)),
        ],
        out_specs=pl.BlockSpec((1, rows, LANES), lambda b, p, i, *_: (b, i, p)),
        scratch_shapes=[pltpu.VMEM((2 * gq, tile, LANES), f32), pltpu.VMEM((3, tile, tile), f32)],
    )
    return pl.pallas_call(
        functools.partial(_attn_kernel, tile, nq, gq, n_meta),
        grid_spec=grid_spec,
        out_shape=jax.ShapeDtypeStruct((b, l, D_ATT), bf16),
        compiler_params=_cparams("parallel", "parallel", "arbitrary"),
        name="prompt_attention",
    )(jmin, mflag, q, qcat, kb, vb, kcat, kbm, vbm, kcm)


def _rowshift_kernel(tile, nq, gq, n_meta, jmin_ref, mflag_ref, q_ref, qc_ref, kb_ref, kc_ref, kbm_ref, kcm_ref,
                     o_ref):
    b = pl.program_id(0)
    p = pl.program_id(1)
    g = pl.program_id(2)
    lane = lax.broadcasted_iota(jnp.int32, (1, LANES), 1)
    slot = lane % AUG
    mcol = lax.broadcasted_iota(jnp.int32, (tile, LANES), 1)
    col = lax.broadcasted_iota(jnp.int32, (tile, tile), 1)
    row = lax.broadcasted_iota(jnp.int32, (tile, tile), 0)

    def rows_of(j):
        return pl.ds(pl.multiple_of(j * tile, tile), tile)

    def row_max(x):
        return jnp.max(x, axis=1, keepdims=True)

    for hh in range(2):
        h = 2 * p + hh
        base = (b * N_HEADS + h) * nq + g * gq
        m_q = (lane // HEAD_DIM == hh).astype(bf16)
        mine = lane // AUG == h
        m_c = mine.astype(bf16)

        def per_tile(r, carry):
            i = g * gq + r
            q_op = jnp.concatenate([q_ref[0, rows_of(r), :] * m_q, qc_ref[0, rows_of(r), :] * m_c], axis=1)

            def scores(kt, kct):
                return _dot(q_op, jnp.concatenate([kt, kct], axis=0))

            m = row_max(jnp.where(col <= row, scores(kb_ref[0, :, rows_of(i)], kc_ref[0, :, rows_of(i)]), NEG_INF))
            m = lax.fori_loop(jmin_ref[base + r], i, lambda j, m: jnp.maximum(
                m, row_max(scores(kb_ref[0, :, rows_of(j)], kc_ref[0, :, rows_of(j)]))), m)
            mm = row_max(jnp.where(mcol < n_meta, scores(kbm_ref[...], kcm_ref[...]), NEG_INF))
            m = jnp.where(mflag_ref[base + r] > 0, jnp.maximum(m, mm), m)
            hi, mid, lo = _split3(jnp.broadcast_to(-m, (tile, LANES)))
            patch = jnp.where(mine & (slot == SHIFT_SLOT), hi, jnp.where(mine & (slot == SHIFT_SLOT + 1), mid,
                              jnp.where(mine & (slot == SHIFT_SLOT + 2), lo, 0.0))).astype(bf16)
            if hh == 0:
                o_ref[0, 0, rows_of(r), :] = patch
            else:
                o_ref[0, 0, rows_of(r), :] = o_ref[0, 0, rows_of(r), :] + patch
            return carry

        lax.fori_loop(0, gq, per_tile, 0)


def _row_shift(q, qcat, kb, kcat, kbm, kcm, jmin, mflag, n_meta):
    b, l, _ = q.shape
    tile = min(ATT_TILE, l)
    nq = l // tile
    gq = min(ATT_GROUP, nq)
    rows = gq * tile
    grid_spec = pltpu.PrefetchScalarGridSpec(
        num_scalar_prefetch=2,
        grid=(b, N_HEADS // 2, nq // gq),
        in_specs=[
            pl.BlockSpec((1, rows, LANES), lambda b, p, i, *_: (b, i, p)),
            pl.BlockSpec((1, rows, LANES), lambda b, p, i, *_: (b, i, 0)),
            pl.BlockSpec((1, LANES, l), lambda b, p, i, *_: (b, p, 0)),
            pl.BlockSpec((1, LANES, l), lambda b, p, i, *_: (b, 0, 0)),
            pl.BlockSpec((LANES, LANES), lambda b, p, i, *_: (p, 0)),
            pl.BlockSpec((LANES, LANES), lambda b, p, i, *_: (0, 0)),
        ],
        out_specs=pl.BlockSpec((1, 1, rows, LANES), lambda b, p, i, *_: (b, p, i, 0)),
    )
    return pl.pallas_call(
        functools.partial(_rowshift_kernel, tile, nq, gq, n_meta),
        grid_spec=grid_spec,
        out_shape=jax.ShapeDtypeStruct((b, N_HEADS // 2, l, LANES), bf16),
        compiler_params=_cparams("parallel", "parallel", "arbitrary"),
        name="row_shift",
    )(jmin, mflag, q, qcat, kb, kcat, kbm, kcm)


def _sattn_kernel(past, tn, q_ref, kn_ref, vn_ref, ckt_ref, cvt_ref, lft_ref, o_ref):
    for i in range(q_ref.shape[0]):
        _sattn_one(past, tn, i, q_ref, kn_ref, vn_ref, ckt_ref, cvt_ref, lft_ref, o_ref)


def _sattn_one(past, tn, i, q_ref, kn_ref, vn_ref, ckt_ref, cvt_ref, lft_ref, o_ref):
    s_len = lft_ref.shape[-1]
    rows = N_HEADS * tn
    c = lft_ref[i]
    lane_h = lax.broadcasted_iota(jnp.int32, c.shape, 1)
    sh = 1
    while sh < s_len:
        c = c + jnp.where(lane_h >= sh, pltpu.roll(c, sh, axis=1), 0.0)
        sh *= 2
    cexp = jnp.concatenate([jnp.broadcast_to(c[h:h + 1, :], (tn, s_len)) for h in range(N_HEADS)], axis=0)
    rix = lax.broadcasted_iota(jnp.int32, (rows, s_len), 0)
    lix = lax.broadcasted_iota(jnp.int32, (rows, s_len), 1)
    qpos = past + rix % tn
    cq = jnp.sum(jnp.where(lix == qpos, cexp, 0.0), axis=1, keepdims=True)
    pad = jnp.zeros((s_len - past - tn, D_ATT), bf16)
    k_new = jnp.concatenate([kn_ref[i], pad], axis=0)
    v_new = jnp.concatenate([vn_ref[i], pad], axis=0)
    qrep = jnp.concatenate([q_ref[i].astype(f32)] * N_HEADS, axis=0)
    r5 = lax.broadcasted_iota(jnp.int32, (rows, D_ATT), 0)
    l5 = lax.broadcasted_iota(jnp.int32, (rows, D_ATT), 1)
    qrows = jnp.where(l5 // HEAD_DIM == r5 // tn, qrep, 0.0).astype(bf16)
    s = jnp.concatenate([_dot(qrows, ckt_ref[i].astype(bf16)), _dot_nt(qrows, k_new)], axis=1)
    s = s + (cq - cexp) * LOG2E
    s = jnp.where(lix <= qpos, s, NEG_INF)
    m = jnp.max(s, axis=1, keepdims=True)
    pe = jnp.exp2(s - m)
    den = jnp.sum(pe, axis=1, keepdims=True)
    pb = pe.astype(bf16)
    o = (_dot_nt(pb[:, :past], cvt_ref[i].astype(bf16)) + _dot(pb[:, past:], v_new)) / den
    lo = lax.broadcasted_iota(jnp.int32, (tn, D_ATT), 1)
    out = jnp.zeros((tn, D_ATT), f32)
    for h in range(N_HEADS):
        out = out + jnp.where(lo // HEAD_DIM == h, o[h * tn:(h + 1) * tn, :], 0.0)
    o_ref[i] = out.astype(bf16)


def _sample_attention(q, kn, vn, ckt, cvt, lft):
    bs, tn, _ = q.shape
    past = ckt.shape[2]
    s_len = lft.shape[-1]
    per_step = 2 if bs % 2 == 0 else 1
    blk = lambda n, w: pl.BlockSpec((per_step, n, w), lambda b: (b, 0, 0))
    return pl.pallas_call(
        functools.partial(_sattn_kernel, past, tn),
        grid=(bs // per_step,),
        in_specs=[blk(tn, D_ATT), blk(tn, D_ATT), blk(tn, D_ATT), blk(D_ATT, past), blk(D_ATT, past),
                  blk(N_HEADS, s_len)],
        out_specs=blk(tn, D_ATT),
        out_shape=jax.ShapeDtypeStruct((bs, tn, D_ATT), bf16),
        compiler_params=_cparams("parallel"),
        name="sample_attention",
    )(q, kn, vn, ckt, cvt, lft)


def _s5prep_kernel(n_sq, are_ref, aim_ref, ldt_ref, btr_ref, bti_ref, cre_ref, cim_ref,
                   aqr_ref, aqi_ref, apr_ref, api_ref, wr_ref, wi_ref, ctr_ref, cti_ref, kk_ref):
    are = are_ref[...]
    aim = aim_ref[...]
    dt = jnp.exp(ldt_ref[...])
    mag = jnp.exp(are * dt)
    ph = aim * dt
    abr = mag * jnp.cos(ph)
    abi = mag * jnp.sin(ph)
    nr = abr - 1.0
    den = are * are + aim * aim
    fr = (nr * are + abi * aim) / den
    fi = (abi * are - nr * aim) / den
    pw = [(jnp.ones_like(abr), jnp.zeros_like(abr))]
    for _ in range(S5_Q):
        pr, pi = pw[-1]
        pw.append((pr * abr - pi * abi, pr * abi + pi * abr))
    aqr_ref[...], aqi_ref[...] = pw[S5_Q]
    pr, pi = pw[S5_Q]
    for _ in range(n_sq):
        pr, pi = pr * pr - pi * pi, 2.0 * pr * pi
    apr_ref[...] = pr
    api_ref[...] = pi
    p_states = are.shape[1]
    ident = (lax.broadcasted_iota(jnp.int32, (p_states, p_states), 0)
             == lax.broadcasted_iota(jnp.int32, (p_states, p_states), 1)).astype(bf16)
    lane = lax.broadcasted_iota(jnp.int32, (SSM_GROUP, S5_Q * SSM_GROUP), 1)
    hi = functools.partial(lax.dot_general, dimension_numbers=(((1,), (1,)), ((), ())),
                           precision=lax.Precision.HIGHEST, preferred_element_type=f32)
    for g in range(are.shape[0]):
        row = slice(g, g + 1)
        btr = btr_ref[g]
        bti = bti_ref[g]
        bbr = fr[row] * btr - fi[row] * bti
        bbi = fr[row] * bti + fi[row] * btr
        cre = cre_ref[g]
        cim = cim_ref[g]
        car_all, cai_all = [cre], [cim]
        for q in range(S5_Q):
            er, ei = pw[S5_Q - 1 - q]
            wr_ref[q, g] = er[row] * bbr - ei[row] * bbi
            wi_ref[q, g] = er[row] * bbi + ei[row] * bbr
            gr, gi = pw[q + 1]
            car_all.append(cre * gr[row] - cim * gi[row])
            cai_all.append(cre * gi[row] + cim * gr[row])
        ctr_ref[g] = _dot_nt(ident, jnp.concatenate(car_all[1:], axis=0).astype(bf16))
        cti_ref[g] = _dot_nt(ident, jnp.concatenate(cai_all[1:], axis=0).astype(bf16))
        kt = (hi(bbr, jnp.concatenate(car_all[:S5_Q], axis=0))
              - hi(bbi, jnp.concatenate(cai_all[:S5_Q], axis=0)))
        kk_ref[g] = jnp.concatenate(
            [kt if qi == 0 else jnp.where(lane >= qi * SSM_GROUP, pltpu.roll(kt, qi * SSM_GROUP, axis=1), 0.0)
             for qi in range(S5_Q)], axis=0)


def _s5_prep(a_re, a_im, log_dt, bt_re, bt_im, c_re, c_im, n_sq):
    g, p = a_re.shape
    sd = jax.ShapeDtypeStruct
    qh = S5_Q * SSM_GROUP
    return pl.pallas_call(
        functools.partial(_s5prep_kernel, n_sq),
        out_shape=[sd((g, p), f32)] * 4 + [sd((S5_Q,) + bt_re.shape, f32)] * 2 + [sd((g, p, qh), f32)] * 2
        + [sd((g, qh, qh), f32)],
        name="s5_prep",
    )(a_re, a_im, log_dt, bt_re, bt_im, c_re, c_im)


def _s5_kernel(chain, n_seg, n_sb, u_ref, w8_ref, toep_ref, cpow_ref, aqr_ref, aqi_ref, apr_ref, api_ref, d8_ref,
               x0r_ref, x0i_ref, y_ref, sr_ref, si_ref, xr_ref, xi_ref):
    aqr = aqr_ref[0]
    aqi = aqi_ref[0]
    w8 = w8_ref[0]
    toep = toep_ref[0]
    cpow = cpow_ref[0]
    d8 = d8_ref[0]

    def step(with_y, sb):
        rows = pl.ds(sb, n_seg, stride=n_sb) if n_sb > 1 else pl.ds(0, n_seg)
        uf = jnp.concatenate([u_ref[0, 0, q, rows, :] for q in range(S5_Q)], axis=1)
        ub = uf.astype(bf16)
        xr = xr_ref[...]
        xi = xi_ref[...]
        if with_y:
            xc = jnp.concatenate([xr, xi], axis=1).astype(bf16)
            lag = jnp.concatenate([_dot(ub[:, :(c + 1) * MXU_TILE], toep[:(c + 1) * MXU_TILE, c * MXU_TILE:(c + 1) * MXU_TILE])
                                   for c in range(S5_Q * LANES // MXU_TILE)], axis=1)
            y = lag + _dot(xc, cpow) + d8 * uf
            for q in range(S5_Q):
                y_ref[0, 0, q, rows, :] = y[:, q * LANES:(q + 1) * LANES]
        pj = _dot(ub, w8)
        xr_ref[...] = aqr * xr - aqi * xi + pj[:, :STATE_BLOCK]
        xi_ref[...] = aqr * xi + aqi * xr + pj[:, STATE_BLOCK:]

    if chain:
        xr_ref[...] = jnp.zeros(xr_ref.shape, f32)
        xi_ref[...] = jnp.zeros(xi_ref.shape, f32)
        for sb in range(n_sb):
            step(False, sb)
        apr = apr_ref[0]
        api = api_ref[0]

        def scan(c, carry):
            kr, ki = carry
            er = xr_ref[pl.ds(c, 1), :]
            ei = xi_ref[pl.ds(c, 1), :]
            xr_ref[pl.ds(c, 1), :] = kr
            xi_ref[pl.ds(c, 1), :] = ki
            return apr * kr - api * ki + er, apr * ki + api * kr + ei

        kr, ki = lax.fori_loop(0, n_seg, scan, (x0r_ref[0], x0i_ref[0]))
        sr_ref[0] = kr
        si_ref[0] = ki
        for sb in range(n_sb):
            step(True, sb)
    else:
        xr_ref[...] = x0r_ref[0]
        xi_ref[...] = x0i_ref[0]
        for sb in range(n_sb):
            step(True, sb)
        sr_ref[0] = xr_ref[...]
        si_ref[0] = xi_ref[...]


def _s5(u, x0r, x0i, weights, n_seg, chain, name):
    bq, _, _, m, _ = u.shape
    n_sb = m // n_seg
    r0 = x0r.shape[1]
    per_k = lambda a: pl.BlockSpec((1,) + a.shape[1:], lambda b, k: (k, 0, 0))
    st = pl.BlockSpec((1, r0, STATE_BLOCK), lambda b, k: (b, 0, k))
    tok = lambda: pl.BlockSpec((1, 1, S5_Q, m, LANES), lambda b, k: (b, k, 0, 0, 0))
    sd = jax.ShapeDtypeStruct
    return pl.pallas_call(
        functools.partial(_s5_kernel, chain, n_seg, n_sb),
        grid=(bq, N_SSM_BLOCKS),
        in_specs=[tok()] + [per_k(a) for a in weights] + [st, st],
        out_specs=[tok(), st, st],
        out_shape=[sd(u.shape, f32), sd(x0r.shape, f32), sd(x0i.shape, f32)],
        scratch_shapes=[pltpu.VMEM((n_seg, STATE_BLOCK), f32)] * 2,
        compiler_params=_cparams("parallel", "parallel"),
        name=name,
    )(u, *weights, x0r, x0i)


def _out_kernel(x_ref, att_ref, ga_ref, ys_ref, gs_ref, wg_ref, bg_ref, wo_ref, y_ref, ys_scr):
    ga = ga_ref[0].astype(f32)
    a = att_ref[0].astype(f32) * (ga * jax.nn.sigmoid(ga))
    rows = x_ref.shape[1]
    for kb in range(N_SSM_BLOCKS):
        for qq in range(S5_Q):
            ys_scr[kb, pl.ds(qq, rows // S5_Q, stride=S5_Q), :] = ys_ref[0, kb, qq]
    ys = jnp.concatenate([ys_scr[kb] for kb in range(N_SSM_BLOCKS)], axis=1)
    z = 0.5 * ys * (1.0 + jnp.tanh(0.7978845608028654 * (ys + 0.044715 * (ys * ys * ys))))
    gate = jax.nn.sigmoid(_dot(z.astype(bf16), wg_ref[...]) + bg_ref[...])
    gs = gs_ref[0].astype(f32)
    s5 = z * gate * (gs * jax.nn.sigmoid(gs))
    cat = jnp.concatenate([a, s5], axis=1).astype(bf16)
    y_ref[0] = x_ref[0] + _dot(cat, wo_ref[...])


def _outproj(x, att, ga, ys, gs, wg, bg, wo, tile, name):
    bx, lx, d = x.shape
    full = lambda a: pl.BlockSpec(a.shape, lambda b, j: (0,) * a.ndim)
    row = lambda w: pl.BlockSpec((1, tile, w), lambda b, j: (b, j, 0))
    return pl.pallas_call(
        _out_kernel,
        grid=(bx, lx // tile),
        in_specs=[row(d), row(D_ATT), row(D_ATT), _s5_tokens_spec(tile, lambda b, j: (b, 0, 0, j, 0)), row(D_SSM),
                  full(wg), full(bg), full(wo)],
        out_specs=row(d),
        out_shape=jax.ShapeDtypeStruct(x.shape, f32),
        scratch_shapes=[pltpu.VMEM((N_SSM_BLOCKS, tile, LANES), f32)],
        compiler_params=_cparams("parallel", "parallel"),
        name=name,
    )(x, att, ga, ys, gs, wg, bg, wo)


def _lower_tri(n):
    r = jnp.arange(n)
    return (r[None, :] <= r[:, None]).astype(bf16)


def kernel(x_prompt, x_sample, cache_k, cache_v, cache_logf, state_s5_re, state_s5_im, meta_tokens, norm_g,
           w_in, b_f, q_norm_g, k_norm_g, s5_a_re, s5_a_im, s5_log_dt, s5_b_re, s5_b_im, s5_c_re, s5_c_im,
           s5_d, w_glu, b_glu, w_out):
    assert norm_g.shape[0] == 1, "single-layer step"
    bp, lp, d = x_prompt.shape
    bs, ts, _ = x_sample.shape
    n_meta = meta_tokens.shape[0]
    past = cache_k.shape[2]
    G, P = N_GROUPS, SSM_STATE

    w = w_in[0]
    o3 = 3 * D_ATT
    o4 = o3 + N_HEADS
    wm = jnp.concatenate([w[:, :o3], w[:, o4:]], axis=1).astype(bf16)
    wf = jnp.repeat(w[:, o3:o4], AUG, axis=1).astype(bf16)
    bfr = jnp.repeat(b_f[0], AUG)[None, :].astype(f32)
    g_in = norm_g[0][None, :].astype(f32)
    qg = (jnp.tile(q_norm_g[0], N_HEADS) * (HEAD_DIM ** -0.5 * LOG2E))[None, :].astype(f32)
    kg = jnp.tile(k_norm_g[0], N_HEADS)[None, :].astype(f32)
    hid = jnp.arange(D_ATT) // HEAD_DIM
    eh = ((hid[:, None] == hid[None, :]).astype(f32) / HEAD_DIM).astype(bf16)
    sel = (jnp.arange(LANES)[None, :] == AUG * jnp.arange(N_HEADS)[:, None]).astype(bf16)
    m_bound = 8.0 * jnp.max(jnp.abs(q_norm_g[0])) * jnp.max(jnp.abs(k_norm_g[0])) * 1.02
    robust = m_bound > SHIFT_SAFE
    negm = jnp.full((1, LANES), -LOG2E, f32) * jnp.where(robust, 0.0, m_bound)
    params = (g_in, wm, wf, bfr, qg, kg, eh, negm, sel)

    tm = n_meta
    meta = _inproj(meta_tokens[None].astype(f32), jnp.zeros((1, 1, LANES), f32), params, tm, _lower_tri(tm), True,
                   "inproj_meta")
    (_, kb_m, vb_m, kf_m, vf_m, lft_m, edge_m, _, kcat_m, _, u_m, _) = meta
    cum_meta_end = edge_m[:, -1, 1:2, :]
    c0 = jnp.broadcast_to(cum_meta_end, (bp, 1, LANES))
    tile_p = min(ROW_TILE, lp)
    tb_p = min(LANES, tile_p)
    meta_cols = lambda a: jnp.pad(jnp.swapaxes(a, 1, 2), ((0, 0), (0, 0), (0, LANES - tm)))
    (q_p, kb_p, vb_p, kt_p, vt_p, lft_p, edge_p, qcat_p, kcat_p, ga_p, u_p, gs_p) = _inproj(
        x_prompt, c0, params, tile_p, _lower_tri(tb_p), True, "inproj_prompt",
        meta_t=(meta_cols(kf_m), meta_cols(vf_m), n_meta))
    ls = bs * ts
    tile_s = min(ROW_TILE, ls)
    (q_s, kb_s, vb_s, kf_s, vf_s, lft_s, _, _, _, ga_s, u_s, gs_s) = _inproj(
        x_sample.reshape(1, ls, d), jnp.zeros((1, 1, LANES), f32), params, tile_s,
        _lower_tri(min(LANES, tile_s)), False, "inproj_sample")

    tile_a = min(ATT_TILE, lp)
    nq = lp // tile_a
    per_tile = tile_a // tb_p
    edges = edge_p[:, :, :2, ::AUG]
    q_start = edges[:, 0::per_tile, 0, :]
    k_end = edges[:, per_tile - 1::per_tile, 1, :]
    dmax = q_start[:, :, None, :] - k_end[:, None, :, :]
    padm = lambda a: jnp.pad(a[0], ((0, LANES - tm), (0, 0)))
    kbm, vbm, kcm = padm(kb_m).T, padm(vb_m), padm(kcat_m).T

    def skip_tables(thr):
        jmin = jnp.sum((dmax < -thr).astype(jnp.int32), axis=2)
        jmin = jnp.minimum(jmin, jnp.arange(nq, dtype=jnp.int32)[None, :, None])
        mflag = (q_start - cum_meta_end[:, :, ::AUG] >= -thr).astype(jnp.int32)
        return jnp.transpose(jmin, (0, 2, 1)).reshape(-1), jnp.transpose(mflag, (0, 2, 1)).reshape(-1)

    def attend_bounded():
        return _prompt_attention(q_p, qcat_p, kb_p, vb_p, kcat_p, kbm, vbm, kcm, *skip_tables(EXP_ZERO), n_meta)

    def attend_robust():
        tables = skip_tables(EXP_ZERO + 2.0 * m_bound)
        shift = _row_shift(q_p, qcat_p, kb_p, kcat_p, kbm, kcm, *tables, n_meta)
        qcat_r = (qcat_p.astype(f32) + jnp.sum(shift.astype(f32), axis=1)).astype(bf16)
        return _prompt_attention(q_p, qcat_r, kb_p, vb_p, kcat_p, kbm, vbm, kcm, *tables, n_meta)

    att_p = lax.cond(robust, attend_robust, attend_bounded)

    lft_new = jnp.transpose(lft_s.reshape(N_HEADS, bs, ts), (1, 0, 2))
    s_len = -(-(past + ts) // LANES) * LANES
    lft = jnp.concatenate([jnp.transpose(cache_logf[0].astype(f32), (0, 2, 1)), lft_new,
                           jnp.zeros((bs, N_HEADS, s_len - past - ts), f32)], axis=2)
    cache_t = lambda c: jnp.transpose(c[0], (0, 2, 3, 1)).reshape(bs, D_ATT, past)
    att_s = _sample_attention(q_s.reshape(bs, ts, D_ATT), kb_s.reshape(bs, ts, D_ATT), vb_s.reshape(bs, ts, D_ATT),
                              cache_t(cache_k), cache_t(cache_v), lft)

    n_seg_p = min(PROMPT_SEGMENTS, lp // S5_Q)
    n_sb_p = lp // (S5_Q * n_seg_p)
    n_sq = n_sb_p.bit_length() - 1
    assert n_sb_p == 1 << n_sq and ts % S5_Q == 0
    tr = lambda a: jnp.transpose(a[0], (0, 2, 1)).astype(f32)
    aqr, aqi, apr, api, wr, wi, ctr, cti, kk = _s5_prep(
        s5_a_re[0].astype(f32), s5_a_im[0].astype(f32), s5_log_dt[0][:, None].astype(f32),
        tr(s5_b_re), tr(s5_b_im), s5_c_re[0].astype(f32), s5_c_im[0].astype(f32), n_sq)
    NB, G8, Q = N_SSM_BLOCKS, GROUPS_PER_BLOCK, S5_Q
    qgh = Q * LANES

    def spread(x, col_of_src, row_group, col_group):
        n_src = x.shape[-1]
        n_rows = x.shape[1]
        n_cols = col_of_src.shape[0]
        sel = (col_of_src[None, :] == jnp.arange(n_src)[:, None]).astype(bf16)
        keep = row_group(jnp.arange(n_rows))[:, None] == col_group(jnp.arange(n_cols))[None, :]
        y = jnp.einsum("krs,sc->krc", x.astype(bf16), sel, preferred_element_type=f32)
        return jnp.where(keep[None], y, 0.0).astype(bf16)

    cols_state = jnp.arange(STATE_BLOCK) % P
    cols_qgh = (jnp.arange(qgh) // LANES) * SSM_GROUP + jnp.arange(qgh) % SSM_GROUP
    group_of_qgh = lambda i: (i // SSM_GROUP) % G8
    by_block = lambda a: jnp.transpose(a.reshape(Q, NB, G8 * SSM_GROUP, P), (1, 0, 2, 3)).reshape(NB, qgh, P)
    w_part = lambda a: spread(by_block(a), cols_state, group_of_qgh, lambda i: i // P)
    w8 = jnp.concatenate([w_part(wr), w_part(wi)], axis=2)
    c_part = lambda a: spread(a.reshape(NB, STATE_BLOCK, Q * SSM_GROUP), cols_qgh, lambda i: i // P, group_of_qgh)
    cpow = jnp.concatenate([c_part(ctr), c_part(-cti)], axis=1)
    kk_rows = jnp.transpose(kk.reshape(NB, G8, Q, SSM_GROUP, Q * SSM_GROUP), (0, 2, 1, 3, 4)).reshape(NB, qgh, -1)
    toep = spread(kk_rows, cols_qgh, group_of_qgh, group_of_qgh)
    lane_k = lambda a: a.reshape(NB, 1, -1)
    d8 = jnp.tile(lane_k(s5_d[0].astype(f32)), (1, 1, Q))
    weights = lambda pr_, pi_: (w8, toep, cpow, lane_k(aqr), lane_k(aqi), lane_k(pr_), lane_k(pi_), d8)

    assert tm == ts
    n_small = -(-(bs + 1) // SUBLANES) * SUBLANES
    u_small = jnp.concatenate([u_s, u_m, jnp.zeros(_s5_tokens_shape(1, (n_small - bs - 1) * ts), f32)], axis=3)
    pad_state = lambda a: jnp.pad(a[0].reshape(bs, G * P).astype(f32), ((0, n_small - bs), (0, 0)))[None]
    ys_small, sr_small, si_small = _s5(u_small, pad_state(state_s5_re), pad_state(state_s5_im),
                                       weights(aqr, aqi), n_small, False, "s5_short")
    x0r = jnp.broadcast_to(sr_small[:, bs:bs + 1, :], (bp, 1, G * P))
    x0i = jnp.broadcast_to(si_small[:, bs:bs + 1, :], (bp, 1, G * P))
    ys_p, sr_p, si_p = _s5(u_p, x0r, x0i, weights(apr, api), n_seg_p, True, "s5_prompt")

    wg = w_glu[0].astype(bf16)
    bg = b_glu[0][None, :].astype(f32)
    wo = w_out[0].astype(bf16)
    y_prompt = _outproj(x_prompt, att_p, ga_p, ys_p, gs_p, wg, bg, wo, min(OUT_TILE, lp), "outproj_prompt")
    y_sample = _outproj(x_sample.reshape(1, ls, d), att_s.reshape(1, ls, D_ATT), ga_s, ys_small[:, :, :, :ls // S5_Q],
                        gs_s, wg, bg, wo, tile_s, "outproj_sample").reshape(bs, ts, d)

    heads_t = lambda a: jnp.transpose(a.reshape(bp, N_HEADS, HEAD_DIM, n_meta + lp), (0, 3, 1, 2))[None]
    new_k_prompt = heads_t(kt_p)
    new_v_prompt = heads_t(vt_p)
    lft_all = jnp.concatenate([jnp.broadcast_to(lft_m, (bp, N_HEADS, n_meta)), lft_p], axis=2)
    new_logf_prompt = jnp.transpose(lft_all, (0, 2, 1))[None]
    new_s5_re_prompt = sr_p.reshape(1, bp, G, P)
    new_s5_im_prompt = si_p.reshape(1, bp, G, P)
    new_k_sample = kf_s.reshape(1, bs, ts, N_HEADS, HEAD_DIM)
    new_v_sample = vf_s.reshape(1, bs, ts, N_HEADS, HEAD_DIM)
    new_logf_sample = jnp.transpose(lft_new, (0, 2, 1))[None]
    new_s5_re_sample = sr_small[:, :bs].reshape(1, bs, G, P)
    new_s5_im_sample = si_small[:, :bs].reshape(1, bs, G, P)
    return (y_prompt, y_sample, new_k_prompt, new_v_prompt, new_logf_prompt, new_s5_re_prompt, new_s5_im_prompt,
            new_k_sample, new_v_sample, new_logf_sample, new_s5_re_sample, new_s5_im_sample)
```
